```python
import math
import jax, jax.numpy as jnp
from jax import lax
import numpy as np

D_MODEL = 1024
BATCH = 8
SEQ = 4096
DEPTH = 2

N_A = max(DEPTH // 2, 1)
N_B = DEPTH - N_A
N_DENSE = (DEPTH + 1) // 2
N_MOE = DEPTH // 2

POOL_WINDOWS = (2, 4, 8, 16)
N_POOL_GROUPS = len(POOL_WINDOWS)
POOL_GROUP_DIM = D_MODEL // N_POOL_GROUPS
POOL_MAX = max(POOL_WINDOWS)

HEAD_DIM = 128
N_HEADS = D_MODEL // HEAD_DIM
BLOCK = 256
TOP_BLOCKS = 3
Q_CHUNK = 16
ROT_DIM = HEAD_DIM // 4
ROPE_THETA = 500000.0

D_FF = 2816
N_EXPERTS = 8
TOP_K = 2
D_FF_EXPERT = 3584

EPS = 1e-6

kernel_name = "yoco_pool_moba_moe_hybrid"


def rms_norm(x, g):
    xf = x.astype(jnp.float32)
    var = jnp.mean(xf * xf, axis=-1, keepdims=True)
    return (xf * lax.rsqrt(var + EPS) * g.astype(jnp.float32)).astype(x.dtype)


def rope_partial(x, pos):
    half = ROT_DIM // 2
    inv = ROPE_THETA ** (-jnp.arange(half, dtype=jnp.float32) * 2.0 / ROT_DIM)
    ang = pos.astype(jnp.float32)[:, None] * inv[None, :]
    cos, sin = jnp.cos(ang), jnp.sin(ang)
    xf = x.astype(jnp.float32)
    x1 = xf[..., :half]
    x2 = xf[..., half:ROT_DIM]
    out = jnp.concatenate([x1 * cos - x2 * sin, x2 * cos + x1 * sin, xf[..., ROT_DIM:]], axis=-1)
    return out.astype(x.dtype)


def pool_mixer(x, w_pool, scale):
    B, S, D = x.shape
    xf = x.astype(jnp.float32)
    csp = jnp.pad(jnp.cumsum(xf, axis=1), ((0, 0), (POOL_MAX, 0), (0, 0)))
    t = jnp.arange(S)
    groups = []
    for g, w in enumerate(POOL_WINDOWS):
        sl = slice(g * POOL_GROUP_DIM, (g + 1) * POOL_GROUP_DIM)
        win_sum = csp[:, POOL_MAX:POOL_MAX + S, sl] - csp[:, POOL_MAX - w:POOL_MAX - w + S, sl]
        count = jnp.minimum(t + 1, w).astype(jnp.float32)[None, :, None]
        groups.append(win_sum / count - xf[:, :, sl])
    pooled = jnp.stack(groups, axis=2)
    y = jnp.einsum('bsgc,gcd->bsgd', pooled, w_pool.astype(jnp.float32))
    y = y.reshape(B, S, D) * scale.astype(jnp.float32)
    return y.astype(x.dtype)


def swiglu(x, wg, wu, wd):
    return (jax.nn.silu(x @ wg) * (x @ wu)) @ wd


def moe_swiglu(x, w_router, wg, wu, wd):
    B, S, D = x.shape
    t = x.reshape(B * S, D)
    logits = (t @ w_router).astype(jnp.float32)
    top_v, top_i = lax.top_k(logits, TOP_K)
    top_w = jax.nn.softmax(top_v, axis=-1)
    gates = jnp.sum(jax.nn.one_hot(top_i, N_EXPERTS, dtype=jnp.float32) * top_w[..., None], axis=1)
    y = jnp.zeros((B * S, D), jnp.float32)
    for e in range(N_EXPERTS):
        ye = swiglu(t, wg[e], wu[e], wd[e]).astype(jnp.float32)
        y = y + gates[:, e:e + 1] * ye
    return y.astype(x.dtype).reshape(B, S, D)


def shared_kv(h, kv_norm, wk, wv, pos):
    B, S, D = h.shape
    hn = rms_norm(h, kv_norm)
    k = (hn @ wk).reshape(B, S, N_HEADS, HEAD_DIM).transpose(0, 2, 1, 3)
    v = (hn @ wv).reshape(B, S, N_HEADS, HEAD_DIM).transpose(0, 2, 1, 3)
    k = rope_partial(k, pos)
    n_blocks = -(-S // BLOCK)
    pad = n_blocks * BLOCK - S
    k = jnp.pad(k, ((0, 0), (0, 0), (0, pad), (0, 0)))
    v = jnp.pad(v, ((0, 0), (0, 0), (0, pad), (0, 0)))
    kb = k.reshape(B, N_HEADS, n_blocks, BLOCK, HEAD_DIM)
    vb = v.reshape(B, N_HEADS, n_blocks, BLOCK, HEAD_DIM)
    k_mean = jnp.mean(kb.astype(jnp.float32), axis=3)
    return kb, vb, k_mean


def moba_attention(q, kb, vb, k_mean):
    B, H, S, hd = q.shape
    n_blocks = kb.shape[2]
    ks = min(TOP_BLOCKS, n_blocks)
    scale = hd ** -0.5
    q_blk = jnp.arange(S) // BLOCK
    gate = jnp.einsum('bhsd,bhnd->bhsn', q.astype(jnp.float32), k_mean)
    fully_past = jnp.arange(n_blocks)[None, :] < q_blk[:, None]
    gate = jnp.where(fully_past[None, None], gate, -jnp.inf)
    _, sel = lax.top_k(gate, ks)
    slot_valid = jnp.arange(ks)[None, :] < q_blk[:, None]
    bi = jnp.arange(B)[:, None, None, None]
    hi = jnp.arange(H)[None, :, None, None]

    def chunk(c):
        start = c * Q_CHUNK
        qc = lax.dynamic_slice_in_dim(q, start, Q_CHUNK, axis=2)
        selc = lax.dynamic_slice_in_dim(sel, start, Q_CHUNK, axis=2)
        validc = lax.dynamic_slice_in_dim(slot_valid, start, Q_CHUNK, axis=0)
        kg = kb[bi, hi, selc]
        vg = vb[bi, hi, selc]
        s_sel = jnp.einsum('bhqd,bhqnkd->bhqnk', qc, kg).astype(jnp.float32) * scale
        s_sel = jnp.where(validc[None, None, :, :, None], s_sel, -jnp.inf)
        ob = start // BLOCK
        ko = lax.dynamic_index_in_dim(kb, ob, axis=2, keepdims=False)
        vo = lax.dynamic_index_in_dim(vb, ob, axis=2, keepdims=False)
        s_own = jnp.einsum('bhqd,bhkd->bhqk', qc, ko).astype(jnp.float32) * scale
        kpos = ob * BLOCK + jnp.arange(BLOCK)
        qpos = start + jnp.arange(Q_CHUNK)
        s_own = jnp.where((kpos[None, :] <= qpos[:, None])[None, None], s_own, -jnp.inf)
        logits = jnp.concatenate([s_sel.reshape(B, H, Q_CHUNK, ks * BLOCK), s_own], axis=-1)
        p = jax.nn.softmax(logits, axis=-1).astype(vb.dtype)
        p_sel = p[..., :ks * BLOCK].reshape(B, H, Q_CHUNK, ks, BLOCK)
        p_own = p[..., ks * BLOCK:]
        return (jnp.einsum('bhqnk,bhqnkd->bhqd', p_sel, vg)
                + jnp.einsum('bhqk,bhkd->bhqd', p_own, vo))

    outs = lax.map(chunk, jnp.arange(S // Q_CHUNK))
    return outs.transpose(1, 2, 0, 3, 4).reshape(B, H, S, hd)


def setup_inputs(seed: int = 0) -> dict:
    key = jax.random.key(seed)
    ks = jax.random.split(key, 20)
    D = D_MODEL
    f32 = jnp.float32
    nrm = lambda k, shape, fan: jax.random.normal(k, shape, f32) * (fan ** -0.5)
    return {
        "x": jax.random.normal(ks[0], (BATCH, SEQ, D), f32),
        "attn_norm": 1.0 + 0.02 * jax.random.normal(ks[1], (DEPTH, D), f32),
        "ffn_norm": 1.0 + 0.02 * jax.random.normal(ks[2], (DEPTH, D), f32),
        "pool_w": nrm(ks[3], (N_A, N_POOL_GROUPS, POOL_GROUP_DIM, POOL_GROUP_DIM), POOL_GROUP_DIM),
        "pool_scale": 1.0 + 0.02 * jax.random.normal(ks[4], (N_A, D), f32),
        "wq": nrm(ks[5], (N_B, D, D), D),
        "wo": nrm(ks[6], (N_B, D, D), D),
        "kv_norm": 1.0 + 0.02 * jax.random.normal(ks[7], (D,), f32),
        "wk": nrm(ks[8], (D, D), D),
        "wv": nrm(ks[9], (D, D), D),
        "ffn_w_gate": nrm(ks[10], (N_DENSE, D, D_FF), D),
        "ffn_w_up": nrm(ks[11], (N_DENSE, D, D_FF), D),
        "ffn_w_down": nrm(ks[12], (N_DENSE, D_FF, D), D_FF),
        "router_w": nrm(ks[13], (N_MOE, D, N_EXPERTS), D),
        "moe_w_gate": nrm(ks[14], (N_MOE, N_EXPERTS, D, D_FF_EXPERT), D),
        "moe_w_up": nrm(ks[15], (N_MOE, N_EXPERTS, D, D_FF_EXPERT), D),
        "moe_w_down": nrm(ks[16], (N_MOE, N_EXPERTS, D_FF_EXPERT, D), D_FF_EXPERT),
        "final_norm": 1.0 + 0.02 * jax.random.normal(ks[17], (D,), f32),
    }


def reference(x, attn_norm, ffn_norm, pool_w, pool_scale, wq, wo, kv_norm, wk, wv,
              ffn_w_gate, ffn_w_up, ffn_w_down, router_w, moe_w_gate, moe_w_up, moe_w_down,
              final_norm):
    B, S, D = x.shape
    pos = jnp.arange(S)
    h = x
    kb = vb = k_mean = None
    for l in range(DEPTH):
        hn = rms_norm(h, attn_norm[l])
        if l < N_A:
            h = h + pool_mixer(hn, pool_w[l], pool_scale[l])
        else:
            j = l - N_A
            q = (hn @ wq[j]).reshape(B, S, N_HEADS, HEAD_DIM).transpose(0, 2, 1, 3)
            q = rope_partial(q, pos)
            o = moba_attention(q, kb, vb, k_mean)
            o = o.transpose(0, 2, 1, 3).reshape(B, S, D)
            h = h + o @ wo[j]
        hn = rms_norm(h, ffn_norm[l])
        if l % 2 == 0:
            i = l // 2
            h = h + swiglu(hn, ffn_w_gate[i], ffn_w_up[i], ffn_w_down[i])
        else:
            i = l // 2
            h = h + moe_swiglu(hn, router_w[i], moe_w_gate[i], moe_w_up[i], moe_w_down[i])
        if l == N_A - 1:
            kb, vb, k_mean = shared_kv(h, kv_norm, wk, wv, pos)
    return rms_norm(h, final_norm)
```

```python
import functools

import jax
import jax.numpy as jnp
from jax import lax
from jax.experimental import pallas as pl
from jax.experimental.pallas import tpu as pltpu

EPS = 1e-6
POOL_WINDOWS = (2, 4, 8, 16)
POOL_HALO = 16
HEAD_DIM = 128
BLOCK = 256
TOP_BLOCKS = 3
ROT_DIM = HEAD_DIM // 4
ROPE_THETA = 500000.0
TOP_K = 2

VMEM_LIMIT = 56 * 1024 * 1024
NEG_INF = float("-inf")

f32 = jnp.float32
bf16 = jnp.bfloat16


def _resident(shape):
    n = len(shape)
    return pl.BlockSpec(shape, lambda *_: (0,) * n, pipeline_mode=pl.Buffered(1))


def _rms_scale(xf):
    return lax.rsqrt(jnp.mean(xf * xf, axis=-1, keepdims=True) + EPS)


def _silu(g):
    return g * (1.0 / (1.0 + jnp.exp(-g)))


def _params(*sem):
    return pltpu.CompilerParams(dimension_semantics=sem, vmem_limit_bytes=VMEM_LIMIT)


def _pool_kernel(x_ref, halo_ref, g_attn_ref, w_ref, scale_ref, g_ffn_ref, h_ref, hn_ref, ext_ref):
    i = pl.program_id(1)
    rows = x_ref.shape[1]
    d = x_ref.shape[2]
    c = d // len(POOL_WINDOWS)
    g_attn = g_attn_ref[...]
    x = x_ref[0]
    xn = x * _rms_scale(x) * g_attn
    halo = halo_ref[0]
    halo = halo * _rms_scale(halo) * g_attn
    ext_ref[0:POOL_HALO, :] = jnp.where(i == 0, 0.0, halo)
    ext_ref[POOL_HALO:, :] = xn
    t = i * rows + lax.broadcasted_iota(jnp.int32, (rows, 1), 0)
    ys = []
    for g, w in enumerate(POOL_WINDOWS):
        c0 = g * c
        s = ext_ref[pl.ds(POOL_HALO, rows), c0:c0 + c]
        for k in range(1, w):
            s = s + ext_ref[pl.ds(POOL_HALO - k, rows), c0:c0 + c]
        cnt = jnp.minimum(t + 1, w).astype(f32)
        pooled = s / cnt - xn[:, c0:c0 + c]
        ys.append(jnp.dot(pooled.astype(bf16), w_ref[g], preferred_element_type=f32))
    y = jnp.concatenate(ys, axis=-1) * scale_ref[...]
    h = x + y
    h_ref[0] = h
    hn_ref[0] = (h * _rms_scale(h) * g_ffn_ref[...]).astype(bf16)


def _pool_mixer(x, g_attn, w_pool, scale, g_ffn, rows):
    b, s, d = x.shape
    c = d // len(POOL_WINDOWS)
    hb = rows // POOL_HALO
    return pl.pallas_call(
        _pool_kernel,
        grid=(b, s // rows),
        in_specs=[
            pl.BlockSpec((1, rows, d), lambda bi, i: (bi, i, 0)),
            pl.BlockSpec((1, POOL_HALO, d), lambda bi, i: (bi, jnp.maximum(i * hb - 1, 0), 0)),
            _resident((1, d)),
            _resident((len(POOL_WINDOWS), c, c)),
            _resident((1, d)),
            _resident((1, d)),
        ],
        out_specs=[
            pl.BlockSpec((1, rows, d), lambda bi, i: (bi, i, 0)),
            pl.BlockSpec((1, rows, d), lambda bi, i: (bi, i, 0)),
        ],
        out_shape=[jax.ShapeDtypeStruct((b, s, d), f32), jax.ShapeDtypeStruct((b, s, d), bf16)],
        scratch_shapes=[pltpu.VMEM((rows + POOL_HALO, d), f32)],
        compiler_params=_params("arbitrary", "arbitrary"),
        name="pool_mixer",
    )(x, x, g_attn, w_pool, scale, g_ffn)


def _ffn_chunks(f, size):
    out, c0 = [], 0
    while c0 < f:
        out.append((c0, min(size, f - c0)))
        c0 += size
    return out


def _dense_ffn_kernel(h_ref, hn_ref, wg_ref, wu_ref, wd_ref, o_ref, *, chunk):
    hn = hn_ref[...]
    acc = h_ref[...]
    for c0, cw in _ffn_chunks(wg_ref.shape[1], chunk):
        g = jnp.dot(hn, wg_ref[:, c0:c0 + cw], preferred_element_type=f32)
        u = jnp.dot(hn, wu_ref[:, c0:c0 + cw], preferred_element_type=f32)
        a = (_silu(g) * u).astype(bf16)
        acc = acc + jnp.dot(a, wd_ref[c0:c0 + cw, :], preferred_element_type=f32)
    o_ref[...] = acc


def _dense_ffn(h, hn, wg, wu, wd, rows, chunk):
    t, d = h.shape
    f = wg.shape[1]
    return pl.pallas_call(
        functools.partial(_dense_ffn_kernel, chunk=chunk),
        grid=(t // rows,),
        in_specs=[
            pl.BlockSpec((rows, d), lambda i: (i, 0)),
            pl.BlockSpec((rows, d), lambda i: (i, 0)),
            _resident((d, f)),
            _resident((d, f)),
            _resident((f, d)),
        ],
        out_specs=pl.BlockSpec((rows, d), lambda i: (i, 0)),
        out_shape=jax.ShapeDtypeStruct((t, d), f32),
        compiler_params=_params("arbitrary"),
        name="dense_ffn",
    )(h, hn, wg, wu, wd)


def _rope(x, cos, sin_signed, lane):
    d = x.shape[-1]
    half = ROT_DIM // 2
    partner = jnp.where(lane < half, pltpu.roll(x, d - half, 1), pltpu.roll(x, half, 1))
    return x * cos + partner * sin_signed


def _qkv_kernel(h_ref, g_q_ref, g_kv_ref, wq_ref, wk_ref, wv_ref, cos_ref, sin_ref,
                qt_ref, k_ref, vt_ref, kmean_ref):
    i = pl.program_id(1)
    rows, d = h_ref.shape[1], h_ref.shape[2]
    heads = d // HEAD_DIM
    h = h_ref[0]
    xhat = h * _rms_scale(h)
    hq = (xhat * g_q_ref[...]).astype(bf16)
    hkv = (xhat * g_kv_ref[...]).astype(bf16)
    q = jnp.dot(hq, wq_ref[...], preferred_element_type=f32)
    k = jnp.dot(hkv, wk_ref[...], preferred_element_type=f32)
    v = jnp.dot(hkv, wv_ref[...], preferred_element_type=f32)
    cos = jnp.concatenate([cos_ref[...]] * heads, axis=-1)
    sin = jnp.concatenate([sin_ref[...]] * heads, axis=-1)
    lane = lax.broadcasted_iota(jnp.int32, (rows, d), 1) % HEAD_DIM
    q = _rope(q, cos, sin, lane) * (HEAD_DIM ** -0.5)
    k = _rope(k, cos, sin, lane)
    qt_ref[0] = q.T.astype(bf16)
    k_ref[0] = k.astype(bf16)
    vt_ref[0] = v.T.astype(bf16)

    nb = kmean_ref.shape[1]
    blk_iota = lax.broadcasted_iota(jnp.int32, (nb, d), 0)

    @pl.when(i == 0)
    def _():
        kmean_ref[0] = jnp.zeros((nb, d), f32)

    km = kmean_ref[0]
    for j in range(rows // BLOCK):
        mean_j = jnp.mean(k[j * BLOCK:(j + 1) * BLOCK, :], axis=0, keepdims=True)
        km = jnp.where(blk_iota == i * (rows // BLOCK) + j, mean_j, km)
    kmean_ref[0] = km


def _qkv_proj(h, g_q, g_kv, wq, wk, wv, cos, sin, rows):
    b, s, d = h.shape
    nb = s // BLOCK
    return pl.pallas_call(
        _qkv_kernel,
        grid=(b, s // rows),
        in_specs=[
            pl.BlockSpec((1, rows, d), lambda bi, i: (bi, i, 0)),
            _resident((1, d)),
            _resident((1, d)),
            _resident((d, d)),
            _resident((d, d)),
            _resident((d, d)),
            pl.BlockSpec((rows, HEAD_DIM), lambda bi, i: (i, 0)),
            pl.BlockSpec((rows, HEAD_DIM), lambda bi, i: (i, 0)),
        ],
        out_specs=[
            pl.BlockSpec((1, d, rows), lambda bi, i: (bi, 0, i)),
            pl.BlockSpec((1, rows, d), lambda bi, i: (bi, i, 0)),
            pl.BlockSpec((1, d, rows), lambda bi, i: (bi, 0, i)),
            pl.BlockSpec((1, nb, d), lambda bi, i: (bi, 0, 0)),
        ],
        out_shape=[
            jax.ShapeDtypeStruct((b, d, s), bf16),
            jax.ShapeDtypeStruct((b, s, d), bf16),
            jax.ShapeDtypeStruct((b, d, s), bf16),
            jax.ShapeDtypeStruct((b, nb, d), f32),
        ],
        compiler_params=_params("arbitrary", "arbitrary"),
        name="qkv_proj",
    )(h, g_q, g_kv, wq, wk, wv, cos, sin)


def _moba_kernel(qt_ref, k_ref, vt_ref, kmean_ref, o_ref, sel_ref):
    j = pl.program_id(2)
    nb = kmean_ref.shape[1]
    qt = qt_ref[0]
    gate = jnp.dot(kmean_ref[0], qt.astype(f32), precision=lax.Precision.HIGHEST,
                   preferred_element_type=f32)
    blk = lax.broadcasted_iota(jnp.int32, (nb, BLOCK), 0)
    past = blk < j
    gate = jnp.where(past, gate, NEG_INF)
    rank = jnp.zeros((nb, BLOCK), jnp.int32)
    for m in range(nb):
        gm = gate[m:m + 1, :]
        ahead = jnp.where(gm > gate, 1, jnp.where(gm == gate, jnp.where(blk > m, 1, 0), 0))
        rank = rank + ahead
    sel_ref[...] = jnp.where(past, jnp.where(rank < TOP_BLOCKS, 1.0, 0.0), 0.0)

    start = pl.multiple_of(j * BLOCK, BLOCK)
    s = jnp.dot(k_ref[0, pl.ds(start, BLOCK), :], qt, preferred_element_type=f32)
    kpos = lax.broadcasted_iota(jnp.int32, (BLOCK, BLOCK), 0)
    qpos = lax.broadcasted_iota(jnp.int32, (BLOCK, BLOCK), 1)
    s = jnp.where(kpos <= qpos, s, NEG_INF)
    m0 = jnp.max(s, axis=0, keepdims=True)
    p = jnp.exp(s - m0)
    l0 = jnp.sum(p, axis=0, keepdims=True)
    acc0 = jnp.dot(vt_ref[0, :, pl.ds(start, BLOCK)], p.astype(bf16), preferred_element_type=f32)

    def body(n, carry):
        m_run, l_run, acc = carry
        off = pl.multiple_of(n * BLOCK, BLOCK)
        s = jnp.dot(k_ref[0, pl.ds(off, BLOCK), :], qt, preferred_element_type=f32)
        chosen = sel_ref[pl.ds(n, 1), :] > 0.0
        s = jnp.where(chosen, s, NEG_INF)
        m_new = jnp.maximum(m_run, jnp.max(s, axis=0, keepdims=True))
        alpha = jnp.exp(m_run - m_new)
        p = jnp.exp(s - m_new)
        l_new = alpha * l_run + jnp.sum(p, axis=0, keepdims=True)
        acc = alpha * acc + jnp.dot(vt_ref[0, :, pl.ds(off, BLOCK)], p.astype(bf16),
                                    preferred_element_type=f32)
        return m_new, l_new, acc

    _, l_fin, acc = lax.fori_loop(0, j, body, (m0, l0, acc0))
    o_ref[0] = (acc / l_fin).T.astype(o_ref.dtype)


def _moba_attention(qt, k, vt, kmean):
    b, d, s = qt.shape
    heads = d // HEAD_DIM
    nb = s // BLOCK
    return pl.pallas_call(
        _moba_kernel,
        grid=(b, heads, nb),
        in_specs=[
            pl.BlockSpec((1, HEAD_DIM, BLOCK), lambda bi, h, j: (bi, h, j)),
            pl.BlockSpec((1, s, HEAD_DIM), lambda bi, h, j: (bi, 0, h)),
            pl.BlockSpec((1, HEAD_DIM, s), lambda bi, h, j: (bi, h, 0)),
            pl.BlockSpec((1, nb, HEAD_DIM), lambda bi, h, j: (bi, 0, h)),
        ],
        out_specs=pl.BlockSpec((1, BLOCK, HEAD_DIM), lambda bi, h, j: (bi, j, h)),
        out_shape=jax.ShapeDtypeStruct((b, s, d), bf16),
        scratch_shapes=[pltpu.VMEM((nb, BLOCK), f32)],
        compiler_params=_params("arbitrary", "arbitrary", "arbitrary"),
        name="moba_attn",
    )(qt, k, vt, kmean)


def _out_router_kernel(o_ref, h_ref, wo_ref, g_ref, wr_ref, h_out_ref, hn_ref, idx_ref, gate_ref):
    rows = o_ref.shape[0]
    n_exp = wr_ref.shape[1]
    h = h_ref[...] + jnp.dot(o_ref[...], wo_ref[...], preferred_element_type=f32)
    h_out_ref[...] = h
    hn = h * _rms_scale(h) * g_ref[...]
    hn_ref[...] = hn
    logits = jnp.dot(hn, wr_ref[...], precision=lax.Precision.HIGHEST, preferred_element_type=f32)
    e = lax.broadcasted_iota(jnp.int32, (rows, n_exp), 1)
    v1 = jnp.max(logits, axis=-1, keepdims=True)
    i1 = jnp.min(jnp.where(logits == v1, e, n_exp), axis=-1, keepdims=True)
    rest = jnp.where(e == i1, NEG_INF, logits)
    v2 = jnp.max(rest, axis=-1, keepdims=True)
    i2 = jnp.min(jnp.where(rest == v2, e, n_exp), axis=-1, keepdims=True)
    e2 = jnp.exp(v2 - v1)
    denom = 1.0 + e2
    col = lax.broadcasted_iota(jnp.int32, (rows, TOP_K), 1)
    idx_ref[...] = jnp.where(col == 0, i1, i2)
    gate_ref[...] = jnp.where(col == 0, 1.0 / denom, e2 / denom)


def _out_router(o, h, wo, g_ffn, w_router, rows):
    t, d = h.shape
    n_exp = w_router.shape[1]
    return pl.pallas_call(
        _out_router_kernel,
        grid=(t // rows,),
        in_specs=[
            pl.BlockSpec((rows, d), lambda i: (i, 0)),
            pl.BlockSpec((rows, d), lambda i: (i, 0)),
            _resident((d, d)),
            _resident((1, d)),
            _resident((d, n_exp)),
        ],
        out_specs=[
            pl.BlockSpec((rows, d), lambda i: (i, 0)),
            pl.BlockSpec((rows, d), lambda i: (i, 0)),
            pl.BlockSpec((rows, TOP_K), lambda i: (i, 0)),
            pl.BlockSpec((rows, TOP_K), lambda i: (i, 0)),
        ],
        out_shape=[
            jax.ShapeDtypeStruct((t, d), f32),
            jax.ShapeDtypeStruct((t, d), f32),
            jax.ShapeDtypeStruct((t, TOP_K), jnp.int32),
            jax.ShapeDtypeStruct((t, TOP_K), f32),
        ],
        compiler_params=_params("arbitrary"),
        name="out_router",
    )(o, h, wo, g_ffn, w_router)


def _moe_gemm_kernel(tile_expert_ref, tile_valid_ref, src_ref, x_hbm, wg_ref, wu_ref, wd_ref,
                     y_ref, xf_ref, xb_ref, acc_ref, sem):
    i = pl.program_id(0)
    f = pl.program_id(1)
    rows = xf_ref.shape[0]
    valid = tile_valid_ref[i] > 0

    @pl.when(jnp.logical_and(f == 0, valid))
    def _():
        def issue(r, carry):
            tok = src_ref[i, r]
            pltpu.make_async_copy(x_hbm.at[pl.ds(tok, 1), :], xf_ref.at[pl.ds(r, 1), :], sem).start()
            return carry
        lax.fori_loop(0, rows, issue, 0)
        pltpu.make_async_copy(x_hbm.at[pl.ds(0, rows), :], xf_ref, sem).wait()
        xb_ref[...] = xf_ref[...].astype(bf16)
        acc_ref[...] = jnp.zeros(acc_ref.shape, f32)

    @pl.when(valid)
    def _():
        xb = xb_ref[...]
        g = jnp.dot(xb, wg_ref[0], preferred_element_type=f32)
        u = jnp.dot(xb, wu_ref[0], preferred_element_type=f32)
        a = (_silu(g) * u).astype(bf16)
        acc_ref[...] += jnp.dot(a, wd_ref[0], preferred_element_type=f32)

    @pl.when(f == pl.num_programs(1) - 1)
    def _():
        y_ref[...] = jnp.where(valid, acc_ref[...], 0.0)


def _moe_gemm(tile_expert, tile_valid, src, x, wg, wu, wd, rows, chunk):
    n_tiles = src.shape[0]
    t, d = x.shape
    fe = wg.shape[2]
    grid_spec = pltpu.PrefetchScalarGridSpec(
        num_scalar_prefetch=3,
        grid=(n_tiles, fe // chunk),
        in_specs=[
            pl.BlockSpec(memory_space=pl.ANY),
            pl.BlockSpec((1, d, chunk), lambda i, f, te, tv, sr: (te[i], 0, f)),
            pl.BlockSpec((1, d, chunk), lambda i, f, te, tv, sr: (te[i], 0, f)),
            pl.BlockSpec((1, chunk, d), lambda i, f, te, tv, sr: (te[i], f, 0)),
        ],
        out_specs=pl.BlockSpec((rows, d), lambda i, f, te, tv, sr: (i, 0)),
        scratch_shapes=[
            pltpu.VMEM((rows, d), f32),
            pltpu.VMEM((rows, d), bf16),
            pltpu.VMEM((rows, d), f32),
            pltpu.SemaphoreType.DMA,
        ],
    )
    return pl.pallas_call(
        _moe_gemm_kernel,
        grid_spec=grid_spec,
        out_shape=jax.ShapeDtypeStruct((n_tiles * rows, d), f32),
        compiler_params=_params("arbitrary", "arbitrary"),
        name="moe_gemm",
    )(tile_expert, tile_valid, src, x, wg, wu, wd)


def _combine_kernel(pos_ref, h_ref, gate_ref, g_ref, y_hbm, o_ref, ybuf_ref, sem):
    i = pl.program_id(0)
    rows = h_ref.shape[0]

    def issue(r, carry):
        for k in range(TOP_K):
            p = pos_ref[i, k * rows + r]
            pltpu.make_async_copy(y_hbm.at[pl.ds(p, 1), :], ybuf_ref.at[k, pl.ds(r, 1), :], sem).start()
        return carry
    lax.fori_loop(0, rows, issue, 0)
    for k in range(TOP_K):
        pltpu.make_async_copy(y_hbm.at[pl.ds(0, rows), :], ybuf_ref.at[k], sem).wait()
    gate = gate_ref[...]
    h = h_ref[...]
    for k in range(TOP_K):
        h = h + gate[:, k:k + 1] * ybuf_ref[k]
    o_ref[...] = h * _rms_scale(h) * g_ref[...]


def _moe_combine(pos2d, h, gates, g_final, y_sorted, rows):
    t, d = h.shape
    grid_spec = pltpu.PrefetchScalarGridSpec(
        num_scalar_prefetch=1,
        grid=(t // rows,),
        in_specs=[
            pl.BlockSpec((rows, d), lambda i, ps: (i, 0)),
            pl.BlockSpec((rows, TOP_K), lambda i, ps: (i, 0)),
            pl.BlockSpec((1, d), lambda i, ps: (0, 0)),
            pl.BlockSpec(memory_space=pl.ANY),
        ],
        out_specs=pl.BlockSpec((rows, d), lambda i, ps: (i, 0)),
        scratch_shapes=[pltpu.VMEM((TOP_K, rows, d), f32), pltpu.SemaphoreType.DMA],
    )
    return pl.pallas_call(
        _combine_kernel,
        grid_spec=grid_spec,
        out_shape=jax.ShapeDtypeStruct((t, d), f32),
        compiler_params=_params("arbitrary"),
        name="moe_combine",
    )(pos2d, h, gates, g_final, y_sorted)


def _routing_plan(idx, n_exp, rows, comb_rows):
    t = idx.shape[0]
    pairs = t * TOP_K
    n_tiles = pairs // rows + n_exp
    e_flat = idx.T.reshape(pairs)
    onehot = (e_flat[:, None] == jnp.arange(n_exp, dtype=jnp.int32)[None, :]).astype(jnp.int32)
    csum = jnp.cumsum(onehot, axis=0)
    counts = csum[-1]
    rank = jnp.sum(onehot * csum, axis=1) - 1
    padded = ((counts + rows - 1) // rows) * rows
    pend = jnp.cumsum(padded)
    poff = pend - padded
    pos = poff[e_flat] + rank
    src_tok = jnp.zeros((n_tiles * rows,), jnp.int32).at[pos].set(
        jnp.arange(pairs, dtype=jnp.int32) % t, unique_indices=True)
    tile_start = jnp.arange(n_tiles, dtype=jnp.int32) * rows
    tile_expert = jnp.minimum(jnp.sum((tile_start[:, None] >= pend[None, :]).astype(jnp.int32), axis=1),
                              n_exp - 1)
    tile_valid = (tile_start < pend[-1]).astype(jnp.int32)
    last_valid = jnp.maximum(jnp.sum(tile_valid) - 1, 0)
    tile_expert = jnp.where(tile_valid > 0, tile_expert, tile_expert[last_valid])
    pos2d = pos.reshape(TOP_K, t // comb_rows, comb_rows).transpose(1, 0, 2).reshape(t // comb_rows,
                                                                                      TOP_K * comb_rows)
    return tile_expert, tile_valid, src_tok.reshape(n_tiles, rows), pos2d


def _rope_tables(s):
    half = ROT_DIM // 2
    inv = ROPE_THETA ** (-jnp.arange(half, dtype=f32) * 2.0 / ROT_DIM)
    ang = jnp.arange(s).astype(f32)[:, None] * inv[None, :]
    cos, sin = jnp.cos(ang), jnp.sin(ang)
    pad = HEAD_DIM - ROT_DIM
    cos_t = jnp.concatenate([cos, cos, jnp.ones((s, pad), f32)], axis=-1)
    sin_t = jnp.concatenate([-sin, sin, jnp.zeros((s, pad), f32)], axis=-1)
    return cos_t, sin_t


def kernel(x, attn_norm, ffn_norm, pool_w, pool_scale, wq, wo, kv_norm, wk, wv, ffn_w_gate, ffn_w_up,
           ffn_w_down, router_w, moe_w_gate, moe_w_up, moe_w_down, final_norm):
    b, s, d = x.shape
    t = b * s
    n_exp = router_w.shape[-1]
    assert attn_norm.shape[0] == 2 and pool_w.shape[0] == 1 and wq.shape[0] == 1 and router_w.shape[0] == 1
    assert s % 512 == 0 and d % (HEAD_DIM * 2) == 0

    tok_rows = 512
    moe_rows = 1024 if (t * TOP_K) % 1024 == 0 else 512
    moe_chunk = 512
    comb_rows = 256

    row = lambda v: v.reshape(1, -1)

    h1, hn1 = _pool_mixer(x, row(attn_norm[0]), pool_w[0].astype(bf16), row(pool_scale[0]),
                          row(ffn_norm[0]), tok_rows)
    h2 = _dense_ffn(h1.reshape(t, d), hn1.reshape(t, d), ffn_w_gate[0].astype(bf16),
                    ffn_w_up[0].astype(bf16), ffn_w_down[0].astype(bf16), tok_rows, 512)

    cos_t, sin_t = _rope_tables(s)
    qt, k, vt, kmean = _qkv_proj(h2.reshape(b, s, d), row(attn_norm[1]), row(kv_norm), wq[0].astype(bf16),
                                 wk.astype(bf16), wv.astype(bf16), cos_t, sin_t, tok_rows)
    o = _moba_attention(qt, k, vt, kmean)

    h3, hn3, idx, gates = _out_router(o.reshape(t, d), h2, wo[0].astype(bf16), row(ffn_norm[1]),
                                      router_w[0], tok_rows)

    tile_expert, tile_valid, src, pos2d = _routing_plan(idx, n_exp, moe_rows, comb_rows)
    y_sorted = _moe_gemm(tile_expert, tile_valid, src, hn3, moe_w_gate[0].astype(bf16),
                         moe_w_up[0].astype(bf16), moe_w_down[0].astype(bf16), moe_rows, moe_chunk)
    out = _moe_combine(pos2d, h3, gates, row(final_norm), y_sorted, comb_rows)
    return out.reshape(b, s, d)
```

```python
import functools

import jax
import jax.numpy as jnp
from jax import lax
from jax.experimental import pallas as pl
from jax.experimental.pallas import tpu as pltpu

EPS = 1e-6
POOL_WINDOWS = (2, 4, 8, 16)
POOL_HALO = 16
HEAD_DIM = 128
BLOCK = 256
TOP_BLOCKS = 3
ROT_DIM = HEAD_DIM // 4
ROPE_THETA = 500000.0
TOP_K = 2

VMEM_LIMIT = 56 * 1024 * 1024
DMA_UNROLL = 8
NEG_INF = float("-inf")

f32 = jnp.float32
bf16 = jnp.bfloat16


def _resident(shape):
    n = len(shape)
    return pl.BlockSpec(shape, lambda *_: (0,) * n, pipeline_mode=pl.Buffered(1))


def _rms_scale(xf):
    return lax.rsqrt(jnp.mean(xf * xf, axis=-1, keepdims=True) + EPS)


def _silu(g):
    return g * (1.0 / (1.0 + jnp.exp(-g)))


def _params(*sem, **kw):
    return pltpu.CompilerParams(dimension_semantics=sem, vmem_limit_bytes=VMEM_LIMIT, **kw)


def _pool_kernel(x_ref, halo_ref, g_attn_ref, w_ref, scale_ref, g_ffn_ref, h_ref, hn_ref, ext_ref):
    i = pl.program_id(1)
    rows = x_ref.shape[1]
    d = x_ref.shape[2]
    c = d // len(POOL_WINDOWS)
    g_attn = g_attn_ref[...]
    x = x_ref[0]
    xn = x * _rms_scale(x) * g_attn
    halo = halo_ref[0]
    halo = halo * _rms_scale(halo) * g_attn
    ext_ref[0:POOL_HALO, :] = jnp.where(i == 0, 0.0, halo)
    ext_ref[POOL_HALO:, :] = xn
    t = i * rows + lax.broadcasted_iota(jnp.int32, (rows, 1), 0)
    ys = []
    for g, w in enumerate(POOL_WINDOWS):
        c0 = g * c
        s = ext_ref[pl.ds(POOL_HALO, rows), c0:c0 + c]
        for k in range(1, w):
            s = s + ext_ref[pl.ds(POOL_HALO - k, rows), c0:c0 + c]
        cnt = jnp.minimum(t + 1, w).astype(f32)
        pooled = s / cnt - xn[:, c0:c0 + c]
        ys.append(jnp.dot(pooled.astype(bf16), w_ref[g], preferred_element_type=f32))
    y = jnp.concatenate(ys, axis=-1) * scale_ref[...]
    h = x + y
    h_ref[0] = h
    hn_ref[0] = (h * _rms_scale(h) * g_ffn_ref[...]).astype(bf16)


def _pool_mixer(x, g_attn, w_pool, scale, g_ffn, rows):
    b, s, d = x.shape
    c = d // len(POOL_WINDOWS)
    hb = rows // POOL_HALO
    return pl.pallas_call(
        _pool_kernel,
        grid=(b, s // rows),
        in_specs=[
            pl.BlockSpec((1, rows, d), lambda bi, i: (bi, i, 0)),
            pl.BlockSpec((1, POOL_HALO, d), lambda bi, i: (bi, jnp.maximum(i * hb - 1, 0), 0)),
            _resident((1, d)),
            _resident((len(POOL_WINDOWS), c, c)),
            _resident((1, d)),
            _resident((1, d)),
        ],
        out_specs=[
            pl.BlockSpec((1, rows, d), lambda bi, i: (bi, i, 0)),
            pl.BlockSpec((1, rows, d), lambda bi, i: (bi, i, 0)),
        ],
        out_shape=[jax.ShapeDtypeStruct((b, s, d), f32), jax.ShapeDtypeStruct((b, s, d), bf16)],
        scratch_shapes=[pltpu.VMEM((rows + POOL_HALO, d), f32)],
        compiler_params=_params("arbitrary", "arbitrary"),
        name="pool_mixer",
    )(x, x, g_attn, w_pool, scale, g_ffn)


def _ffn_chunks(f, size):
    out, c0 = [], 0
    while c0 < f:
        out.append((c0, min(size, f - c0)))
        c0 += size
    return out


def _dense_ffn_kernel(h_ref, hn_ref, wg_ref, wu_ref, wd_ref, o_ref, *, chunk):
    hn = hn_ref[...]
    acc = h_ref[...]
    for c0, cw in _ffn_chunks(wg_ref.shape[1], chunk):
        g = jnp.dot(hn, wg_ref[:, c0:c0 + cw], preferred_element_type=f32)
        u = jnp.dot(hn, wu_ref[:, c0:c0 + cw], preferred_element_type=f32)
        a = (_silu(g) * u).astype(bf16)
        acc = acc + jnp.dot(a, wd_ref[c0:c0 + cw, :], preferred_element_type=f32)
    o_ref[...] = acc


def _dense_ffn(h, hn, wg, wu, wd, rows, chunk):
    t, d = h.shape
    f = wg.shape[1]
    return pl.pallas_call(
        functools.partial(_dense_ffn_kernel, chunk=chunk),
        grid=(t // rows,),
        in_specs=[
            pl.BlockSpec((rows, d), lambda i: (i, 0)),
            pl.BlockSpec((rows, d), lambda i: (i, 0)),
            _resident((d, f)),
            _resident((d, f)),
            _resident((f, d)),
        ],
        out_specs=pl.BlockSpec((rows, d), lambda i: (i, 0)),
        out_shape=jax.ShapeDtypeStruct((t, d), f32),
        compiler_params=_params("arbitrary"),
        name="dense_ffn",
    )(h, hn, wg, wu, wd)


def _rope(x, cos, sin_signed, lane):
    d = x.shape[-1]
    half = ROT_DIM // 2
    partner = jnp.where(lane < half, pltpu.roll(x, d - half, 1), pltpu.roll(x, half, 1))
    return x * cos + partner * sin_signed


def _qkv_kernel(h_ref, g_q_ref, g_kv_ref, wq_ref, wk_ref, wv_ref, cos_ref, sin_ref,
                qt_ref, k_ref, vt_ref, kmean_ref):
    i = pl.program_id(1)
    rows, d = h_ref.shape[1], h_ref.shape[2]
    heads = d // HEAD_DIM
    h = h_ref[0]
    xhat = h * _rms_scale(h)
    hq = (xhat * g_q_ref[...]).astype(bf16)
    hkv = (xhat * g_kv_ref[...]).astype(bf16)
    q = jnp.dot(hq, wq_ref[...], preferred_element_type=f32)
    k = jnp.dot(hkv, wk_ref[...], preferred_element_type=f32)
    v = jnp.dot(hkv, wv_ref[...], preferred_element_type=f32)
    cos = jnp.concatenate([cos_ref[...]] * heads, axis=-1)
    sin = jnp.concatenate([sin_ref[...]] * heads, axis=-1)
    lane = lax.broadcasted_iota(jnp.int32, (rows, d), 1) % HEAD_DIM
    q = _rope(q, cos, sin, lane) * (HEAD_DIM ** -0.5)
    k = _rope(k, cos, sin, lane)
    qt_ref[0] = q.T.astype(bf16)
    k_ref[0] = k.astype(bf16)
    vt_ref[0] = v.T.astype(bf16)

    nb = kmean_ref.shape[1]
    blk_iota = lax.broadcasted_iota(jnp.int32, (nb, d), 0)

    @pl.when(i == 0)
    def _():
        kmean_ref[0] = jnp.zeros((nb, d), f32)

    km = kmean_ref[0]
    for j in range(rows // BLOCK):
        mean_j = jnp.mean(k[j * BLOCK:(j + 1) * BLOCK, :], axis=0, keepdims=True)
        km = jnp.where(blk_iota == i * (rows // BLOCK) + j, mean_j, km)
    kmean_ref[0] = km


def _qkv_proj(h, g_q, g_kv, wq, wk, wv, cos, sin, rows):
    b, s, d = h.shape
    nb = s // BLOCK
    return pl.pallas_call(
        _qkv_kernel,
        grid=(b, s // rows),
        in_specs=[
            pl.BlockSpec((1, rows, d), lambda bi, i: (bi, i, 0)),
            _resident((1, d)),
            _resident((1, d)),
            _resident((d, d)),
            _resident((d, d)),
            _resident((d, d)),
            pl.BlockSpec((rows, HEAD_DIM), lambda bi, i: (i, 0)),
            pl.BlockSpec((rows, HEAD_DIM), lambda bi, i: (i, 0)),
        ],
        out_specs=[
            pl.BlockSpec((1, d, rows), lambda bi, i: (bi, 0, i)),
            pl.BlockSpec((1, rows, d), lambda bi, i: (bi, i, 0)),
            pl.BlockSpec((1, d, rows), lambda bi, i: (bi, 0, i)),
            pl.BlockSpec((1, nb, d), lambda bi, i: (bi, 0, 0)),
        ],
        out_shape=[
            jax.ShapeDtypeStruct((b, d, s), bf16),
            jax.ShapeDtypeStruct((b, s, d), bf16),
            jax.ShapeDtypeStruct((b, d, s), bf16),
            jax.ShapeDtypeStruct((b, nb, d), f32),
        ],
        compiler_params=_params("arbitrary", "arbitrary"),
        name="qkv_proj",
    )(h, g_q, g_kv, wq, wk, wv, cos, sin)


def _moba_kernel(qt_ref, k_ref, vt_ref, kmean_ref, o_ref, *scratch, heads):
    sel_refs, m_refs, l_refs, acc_refs = (scratch[i * heads:(i + 1) * heads] for i in range(4))
    j = pl.program_id(1)
    nb = kmean_ref.shape[1]
    hd = HEAD_DIM
    blk = lax.broadcasted_iota(jnp.int32, (nb, BLOCK), 0)
    past = blk < j
    kpos = lax.broadcasted_iota(jnp.int32, (BLOCK, BLOCK), 0)
    qpos = lax.broadcasted_iota(jnp.int32, (BLOCK, BLOCK), 1)
    causal = kpos <= qpos
    start = pl.multiple_of(j * BLOCK, BLOCK)

    def q_head(h):
        return qt_ref[0, h * hd:(h + 1) * hd, :]

    def scores(off):
        return [jnp.dot(k_ref[0, pl.ds(off, BLOCK), h * hd:(h + 1) * hd], q_head(h),
                        preferred_element_type=f32) for h in range(heads)]

    def weighted_values(ps, off):
        return [jnp.dot(vt_ref[0, h * hd:(h + 1) * hd, pl.ds(off, BLOCK)], ps[h],
                        preferred_element_type=f32) for h in range(heads)]

    for h in range(heads):
        gate = jnp.dot(kmean_ref[0, :, h * hd:(h + 1) * hd], q_head(h).astype(f32),
                       precision=lax.Precision.HIGHEST, preferred_element_type=f32)
        gate = jnp.where(past, gate, NEG_INF)
        rank = jnp.zeros((nb, BLOCK), jnp.int32)
        for m in range(nb):
            gm = gate[m:m + 1, :]
            rank = rank + jnp.where(gm > gate, 1, jnp.where(gm == gate, jnp.where(blk > m, 1, 0), 0))
        sel_refs[h][...] = jnp.where(past, jnp.where(rank < TOP_BLOCKS, 1.0, 0.0), 0.0)

    ps = []
    for h, s in enumerate(scores(start)):
        s = jnp.where(causal, s, NEG_INF)
        m0 = jnp.max(s, axis=0, keepdims=True)
        p = jnp.exp(s - m0)
        m_refs[h][...] = m0
        l_refs[h][...] = jnp.sum(p, axis=0, keepdims=True)
        ps.append(p.astype(bf16))
    for h, pv in enumerate(weighted_values(ps, start)):
        acc_refs[h][...] = pv

    def body(n, carry):
        off = pl.multiple_of(n * BLOCK, BLOCK)
        ps, alphas = [], []
        for h, s in enumerate(scores(off)):
            s = jnp.where(sel_refs[h][pl.ds(n, 1), :] > 0.0, s, NEG_INF)
            m_old = m_refs[h][...]
            m_new = jnp.maximum(m_old, jnp.max(s, axis=0, keepdims=True))
            alpha = jnp.exp(m_old - m_new)
            p = jnp.exp(s - m_new)
            m_refs[h][...] = m_new
            l_refs[h][...] = alpha * l_refs[h][...] + jnp.sum(p, axis=0, keepdims=True)
            ps.append(p.astype(bf16))
            alphas.append(alpha)
        for h, pv in enumerate(weighted_values(ps, off)):
            acc_refs[h][...] = alphas[h] * acc_refs[h][...] + pv
        return carry

    lax.fori_loop(0, j, body, 0)
    for h in range(heads):
        o_ref[0, :, h * hd:(h + 1) * hd] = (acc_refs[h][...] / l_refs[h][...]).T.astype(o_ref.dtype)


def _moba_attention(qt, k, vt, kmean):
    b, d, s = qt.shape
    heads = d // HEAD_DIM
    nb = s // BLOCK
    return pl.pallas_call(
        functools.partial(_moba_kernel, heads=heads),
        grid=(b, nb),
        in_specs=[
            pl.BlockSpec((1, d, BLOCK), lambda bi, j: (bi, 0, j)),
            pl.BlockSpec((1, s, d), lambda bi, j: (bi, 0, 0), pipeline_mode=pl.Buffered(1)),
            pl.BlockSpec((1, d, s), lambda bi, j: (bi, 0, 0), pipeline_mode=pl.Buffered(1)),
            pl.BlockSpec((1, nb, d), lambda bi, j: (bi, 0, 0)),
        ],
        out_specs=pl.BlockSpec((1, BLOCK, d), lambda bi, j: (bi, j, 0)),
        out_shape=jax.ShapeDtypeStruct((b, s, d), bf16),
        scratch_shapes=(
            [pltpu.VMEM((nb, BLOCK), f32)] * heads
            + [pltpu.VMEM((1, BLOCK), f32)] * (2 * heads)
            + [pltpu.VMEM((HEAD_DIM, BLOCK), f32)] * heads
        ),
        compiler_params=_params("arbitrary", "arbitrary"),
        name="moba_attn",
    )(qt, k, vt, kmean)


def _out_router_kernel(o_ref, h_ref, wo_ref, g_ref, wr_ref, h_out_ref, hn_ref, idx_ref, gate_ref):
    rows = o_ref.shape[0]
    n_exp = wr_ref.shape[1]
    h = h_ref[...] + jnp.dot(o_ref[...], wo_ref[...], preferred_element_type=f32)
    h_out_ref[...] = h
    hn = h * _rms_scale(h) * g_ref[...]
    hn_ref[...] = hn
    logits = jnp.dot(hn, wr_ref[...], precision=lax.Precision.HIGHEST, preferred_element_type=f32)
    e = lax.broadcasted_iota(jnp.int32, (rows, n_exp), 1)
    v1 = jnp.max(logits, axis=-1, keepdims=True)
    i1 = jnp.min(jnp.where(logits == v1, e, n_exp), axis=-1, keepdims=True)
    rest = jnp.where(e == i1, NEG_INF, logits)
    v2 = jnp.max(rest, axis=-1, keepdims=True)
    i2 = jnp.min(jnp.where(rest == v2, e, n_exp), axis=-1, keepdims=True)
    e2 = jnp.exp(v2 - v1)
    denom = 1.0 + e2
    col = lax.broadcasted_iota(jnp.int32, (rows, TOP_K), 1)
    idx_ref[...] = jnp.where(col == 0, i1, i2)
    gate_ref[...] = jnp.where(col == 0, 1.0 / denom, e2 / denom)


def _out_router(o, h, wo, g_ffn, w_router, rows):
    t, d = h.shape
    n_exp = w_router.shape[1]
    return pl.pallas_call(
        _out_router_kernel,
        grid=(t // rows,),
        in_specs=[
            pl.BlockSpec((rows, d), lambda i: (i, 0)),
            pl.BlockSpec((rows, d), lambda i: (i, 0)),
            _resident((d, d)),
            _resident((1, d)),
            _resident((d, n_exp)),
        ],
        out_specs=[
            pl.BlockSpec((rows, d), lambda i: (i, 0)),
            pl.BlockSpec((rows, d), lambda i: (i, 0)),
            pl.BlockSpec((rows, TOP_K), lambda i: (i, 0)),
            pl.BlockSpec((rows, TOP_K), lambda i: (i, 0)),
        ],
        out_shape=[
            jax.ShapeDtypeStruct((t, d), f32),
            jax.ShapeDtypeStruct((t, d), f32),
            jax.ShapeDtypeStruct((t, TOP_K), jnp.int32),
            jax.ShapeDtypeStruct((t, TOP_K), f32),
        ],
        compiler_params=_params("arbitrary"),
        name="out_router",
    )(o, h, wo, g_ffn, w_router)


def _moe_gemm_kernel(tile_expert_ref, tile_valid_ref, src_ref, x_hbm, wg_ref, wu_ref, wd_ref,
                     y_ref, xf_ref, xb_ref, acc_ref, sem):
    i = pl.program_id(0)
    f = pl.program_id(1)
    rows = xf_ref.shape[1]
    valid = tile_valid_ref[i] > 0
    slot = i % 2

    def gather_rows(tile, buf):
        def issue(c, carry):
            for u in range(DMA_UNROLL):
                r = c * DMA_UNROLL + u
                tok = src_ref[tile, r]
                pltpu.make_async_copy(x_hbm.at[pl.ds(tok, 1), :], xf_ref.at[buf, pl.ds(r, 1), :],
                                      sem.at[buf]).start()
            return carry
        lax.fori_loop(0, rows // DMA_UNROLL, issue, 0)

    @pl.when(jnp.logical_and(f == 0, jnp.logical_and(i == 0, valid)))
    def _():
        gather_rows(0, 0)

    @pl.when(jnp.logical_and(f == 0, valid))
    def _():
        pltpu.make_async_copy(x_hbm.at[pl.ds(0, rows), :], xf_ref.at[slot], sem.at[slot]).wait()
        xb_ref[...] = xf_ref[slot].astype(bf16)
        acc_ref[...] = jnp.zeros(acc_ref.shape, f32)

        @pl.when(tile_valid_ref[i + 1] > 0)
        def _():
            gather_rows(i + 1, 1 - slot)

    @pl.when(valid)
    def _():
        xb = xb_ref[...]
        g = jnp.dot(xb, wg_ref[0], preferred_element_type=f32)
        u = jnp.dot(xb, wu_ref[0], preferred_element_type=f32)
        a = (_silu(g) * u).astype(bf16)
        acc_ref[...] += jnp.dot(a, wd_ref[0], preferred_element_type=f32)

    @pl.when(f == pl.num_programs(1) - 1)
    def _():
        y_ref[...] = jnp.where(valid, acc_ref[...], 0.0)


def _moe_gemm(tile_expert, tile_valid, src, x, wg, wu, wd, rows, chunk):
    n_tiles = src.shape[0]
    t, d = x.shape
    fe = wg.shape[2]
    grid_spec = pltpu.PrefetchScalarGridSpec(
        num_scalar_prefetch=3,
        grid=(n_tiles, fe // chunk),
        in_specs=[
            pl.BlockSpec(memory_space=pl.ANY),
            pl.BlockSpec((1, d, chunk), lambda i, f, te, tv, sr: (te[i], 0, f)),
            pl.BlockSpec((1, d, chunk), lambda i, f, te, tv, sr: (te[i], 0, f)),
            pl.BlockSpec((1, chunk, d), lambda i, f, te, tv, sr: (te[i], f, 0)),
        ],
        out_specs=pl.BlockSpec((rows, d), lambda i, f, te, tv, sr: (i, 0)),
        scratch_shapes=[
            pltpu.VMEM((2, rows, d), f32),
            pltpu.VMEM((rows, d), bf16),
            pltpu.VMEM((rows, d), f32),
            pltpu.SemaphoreType.DMA((2,)),
        ],
    )
    return pl.pallas_call(
        _moe_gemm_kernel,
        grid_spec=grid_spec,
        out_shape=jax.ShapeDtypeStruct((n_tiles * rows, d), f32),
        compiler_params=_params("arbitrary", "arbitrary", disable_bounds_checks=True),
        name="moe_gemm",
    )(tile_expert, tile_valid, src, x, wg, wu, wd)


def _combine_kernel(pos_ref, h_ref, gate_ref, g_ref, y_hbm, o_ref, ybuf_ref, sem):
    i = pl.program_id(0)
    rows = h_ref.shape[0]
    slot = i % 2

    def gather_rows(step, buf):
        def issue(c, carry):
            for u in range(DMA_UNROLL):
                r = c * DMA_UNROLL + u
                for k in range(TOP_K):
                    p = pos_ref[step, k * rows + r]
                    pltpu.make_async_copy(y_hbm.at[pl.ds(p, 1), :], ybuf_ref.at[buf, k, pl.ds(r, 1), :],
                                          sem.at[buf]).start()
            return carry
        lax.fori_loop(0, rows // DMA_UNROLL, issue, 0)

    @pl.when(i == 0)
    def _():
        gather_rows(0, 0)

    @pl.when(i + 1 < pl.num_programs(0))
    def _():
        gather_rows(i + 1, 1 - slot)

    for k in range(TOP_K):
        pltpu.make_async_copy(y_hbm.at[pl.ds(0, rows), :], ybuf_ref.at[slot, k], sem.at[slot]).wait()
    gate = gate_ref[...]
    h = h_ref[...]
    for k in range(TOP_K):
        h = h + gate[:, k:k + 1] * ybuf_ref[slot, k]
    o_ref[...] = h * _rms_scale(h) * g_ref[...]


def _moe_combine(pos2d, h, gates, g_final, y_sorted, rows):
    t, d = h.shape
    grid_spec = pltpu.PrefetchScalarGridSpec(
        num_scalar_prefetch=1,
        grid=(t // rows,),
        in_specs=[
            pl.BlockSpec((rows, d), lambda i, ps: (i, 0)),
            pl.BlockSpec((rows, TOP_K), lambda i, ps: (i, 0)),
            pl.BlockSpec((1, d), lambda i, ps: (0, 0)),
            pl.BlockSpec(memory_space=pl.ANY),
        ],
        out_specs=pl.BlockSpec((rows, d), lambda i, ps: (i, 0)),
        scratch_shapes=[pltpu.VMEM((2, TOP_K, rows, d), f32), pltpu.SemaphoreType.DMA((2,))],
    )
    return pl.pallas_call(
        _combine_kernel,
        grid_spec=grid_spec,
        out_shape=jax.ShapeDtypeStruct((t, d), f32),
        compiler_params=_params("arbitrary", disable_bounds_checks=True),
        name="moe_combine",
    )(pos2d, h, gates, g_final, y_sorted)


def _routing_plan(idx, n_exp, rows, comb_rows):
    t = idx.shape[0]
    pairs = t * TOP_K
    n_tiles = -(-pairs // rows) + n_exp
    e_flat = idx.T.reshape(pairs)
    onehot = (e_flat[:, None] == jnp.arange(n_exp, dtype=jnp.int32)[None, :]).astype(jnp.int32)
    csum = jnp.cumsum(onehot, axis=0)
    counts = csum[-1]
    rank = jnp.sum(onehot * csum, axis=1) - 1
    padded = ((counts + rows - 1) // rows) * rows
    pend = jnp.cumsum(padded)
    poff = pend - padded
    pos = poff[e_flat] + rank
    src_tok = jnp.zeros((n_tiles * rows,), jnp.int32).at[pos].set(
        jnp.arange(pairs, dtype=jnp.int32) % t, unique_indices=True)
    tile_start = jnp.arange(n_tiles, dtype=jnp.int32) * rows
    tile_expert = jnp.minimum(jnp.sum((tile_start[:, None] >= pend[None, :]).astype(jnp.int32), axis=1),
                              n_exp - 1)
    tile_valid = (tile_start < pend[-1]).astype(jnp.int32)
    last_valid = jnp.maximum(jnp.sum(tile_valid) - 1, 0)
    tile_expert = jnp.where(tile_valid > 0, tile_expert, tile_expert[last_valid])
    pos2d = pos.reshape(TOP_K, t // comb_rows, comb_rows).transpose(1, 0, 2).reshape(t // comb_rows,
                                                                                      TOP_K * comb_rows)
    return tile_expert, tile_valid, src_tok.reshape(n_tiles, rows), pos2d


def _rope_tables(s):
    half = ROT_DIM // 2
    inv = ROPE_THETA ** (-jnp.arange(half, dtype=f32) * 2.0 / ROT_DIM)
    ang = jnp.arange(s).astype(f32)[:, None] * inv[None, :]
    cos, sin = jnp.cos(ang), jnp.sin(ang)
    pad = HEAD_DIM - ROT_DIM
    cos_t = jnp.concatenate([cos, cos, jnp.ones((s, pad), f32)], axis=-1)
    sin_t = jnp.concatenate([-sin, sin, jnp.zeros((s, pad), f32)], axis=-1)
    return cos_t, sin_t


def kernel(x, attn_norm, ffn_norm, pool_w, pool_scale, wq, wo, kv_norm, wk, wv, ffn_w_gate, ffn_w_up,
           ffn_w_down, router_w, moe_w_gate, moe_w_up, moe_w_down, final_norm):
    b, s, d = x.shape
    t = b * s
    n_exp = router_w.shape[-1]
    assert attn_norm.shape[0] == 2 and pool_w.shape[0] == 1 and wq.shape[0] == 1 and router_w.shape[0] == 1
    assert s % 512 == 0 and d % (HEAD_DIM * 2) == 0

    tok_rows = 512
    moe_rows = 1024 if (t * TOP_K) % 1024 == 0 else 512
    moe_chunk = 512
    comb_rows = 256

    row = lambda v: v.reshape(1, -1)

    h1, hn1 = _pool_mixer(x, row(attn_norm[0]), pool_w[0].astype(bf16), row(pool_scale[0]),
                          row(ffn_norm[0]), tok_rows)
    h2 = _dense_ffn(h1.reshape(t, d), hn1.reshape(t, d), ffn_w_gate[0].astype(bf16),
                    ffn_w_up[0].astype(bf16), ffn_w_down[0].astype(bf16), tok_rows, 512)

    cos_t, sin_t = _rope_tables(s)
    qt, k, vt, kmean = _qkv_proj(h2.reshape(b, s, d), row(attn_norm[1]), row(kv_norm), wq[0].astype(bf16),
                                 wk.astype(bf16), wv.astype(bf16), cos_t, sin_t, tok_rows)
    o = _moba_attention(qt, k, vt, kmean)

    h3, hn3, idx, gates = _out_router(o.reshape(t, d), h2, wo[0].astype(bf16), row(ffn_norm[1]),
                                      router_w[0], tok_rows)

    tile_expert, tile_valid, src, pos2d = _routing_plan(idx, n_exp, moe_rows, comb_rows)
    y_sorted = _moe_gemm(tile_expert, tile_valid, src, hn3, moe_w_gate[0].astype(bf16),
                         moe_w_up[0].astype(bf16), moe_w_down[0].astype(bf16), moe_rows, moe_chunk)
    out = _moe_combine(pos2d, h3, gates, row(final_norm), y_sorted, comb_rows)
    return out.reshape(b, s, d)
```

```python
import functools

import jax
import jax.numpy as jnp
from jax import lax
from jax.experimental import pallas as pl
from jax.experimental.pallas import tpu as pltpu

EPS = 1e-6
POOL_WINDOWS = (2, 4, 8, 16)
POOL_HALO = 16
LANES = 128
HEAD_DIM = 128
BLOCK = 256
TOP_BLOCKS = 3
ROT_DIM = HEAD_DIM // 4
ROPE_THETA = 500000.0
TOP_K = 2

VMEM_LIMIT = 56 * 1024 * 1024
DMA_UNROLL = 8
NEG_INF = float("-inf")

f32 = jnp.float32
bf16 = jnp.bfloat16


def _resident(shape):
    n = len(shape)
    return pl.BlockSpec(shape, lambda *_: (0,) * n, pipeline_mode=pl.Buffered(1))


def _rms_scale(xf):
    return lax.rsqrt(jnp.mean(xf * xf, axis=-1, keepdims=True) + EPS)


def _silu(g):
    return g * (1.0 / (1.0 + jnp.exp(-g)))


def _store_token_tiles(ref, x):
    rows, d = x.shape
    sub = d // LANES
    for c in range(sub):
        ref[pl.ds(c, rows, stride=sub), :] = x[:, c * LANES:(c + 1) * LANES]


def _load_token_tiles(ref, rows, sub, chunk):
    return ref[pl.ds(chunk, rows, stride=sub), :]


def _params(*sem, **kw):
    return pltpu.CompilerParams(dimension_semantics=sem, vmem_limit_bytes=VMEM_LIMIT, **kw)


def _pool_kernel(x_ref, halo_ref, g_attn_ref, w_ref, scale_ref, g_ffn_ref, h_ref, hn_ref, ext_ref):
    i = pl.program_id(1)
    rows = x_ref.shape[1]
    d = x_ref.shape[2]
    c = d // len(POOL_WINDOWS)
    g_attn = g_attn_ref[...]
    x = x_ref[0]
    xn = x * _rms_scale(x) * g_attn
    halo = halo_ref[0]
    halo = halo * _rms_scale(halo) * g_attn
    ext_ref[0:POOL_HALO, :] = jnp.where(i == 0, 0.0, halo)
    ext_ref[POOL_HALO:, :] = xn
    t = i * rows + lax.broadcasted_iota(jnp.int32, (rows, 1), 0)
    ys = []
    for g, w in enumerate(POOL_WINDOWS):
        c0 = g * c
        s = ext_ref[pl.ds(POOL_HALO, rows), c0:c0 + c]
        for k in range(1, w):
            s = s + ext_ref[pl.ds(POOL_HALO - k, rows), c0:c0 + c]
        cnt = jnp.minimum(t + 1, w).astype(f32)
        pooled = s / cnt - xn[:, c0:c0 + c]
        ys.append(jnp.dot(pooled.astype(bf16), w_ref[g], preferred_element_type=f32))
    y = jnp.concatenate(ys, axis=-1) * scale_ref[...]
    h = x + y
    h_ref[0] = h
    hn_ref[0] = (h * _rms_scale(h) * g_ffn_ref[...]).astype(bf16)


def _pool_mixer(x, g_attn, w_pool, scale, g_ffn, rows):
    b, s, d = x.shape
    c = d // len(POOL_WINDOWS)
    hb = rows // POOL_HALO
    return pl.pallas_call(
        _pool_kernel,
        grid=(b, s // rows),
        in_specs=[
            pl.BlockSpec((1, rows, d), lambda bi, i: (bi, i, 0)),
            pl.BlockSpec((1, POOL_HALO, d), lambda bi, i: (bi, jnp.maximum(i * hb - 1, 0), 0)),
            _resident((1, d)),
            _resident((len(POOL_WINDOWS), c, c)),
            _resident((1, d)),
            _resident((1, d)),
        ],
        out_specs=[
            pl.BlockSpec((1, rows, d), lambda bi, i: (bi, i, 0)),
            pl.BlockSpec((1, rows, d), lambda bi, i: (bi, i, 0)),
        ],
        out_shape=[jax.ShapeDtypeStruct((b, s, d), f32), jax.ShapeDtypeStruct((b, s, d), bf16)],
        scratch_shapes=[pltpu.VMEM((rows + POOL_HALO, d), f32)],
        compiler_params=_params("arbitrary", "arbitrary"),
        name="pool_mixer",
    )(x, x, g_attn, w_pool, scale, g_ffn)


def _ffn_chunks(f, size):
    out, c0 = [], 0
    while c0 < f:
        out.append((c0, min(size, f - c0)))
        c0 += size
    return out


def _dense_ffn_kernel(h_ref, hn_ref, wg_ref, wu_ref, wd_ref, o_ref, *, chunk):
    hn = hn_ref[...]
    acc = h_ref[...]
    for c0, cw in _ffn_chunks(wg_ref.shape[1], chunk):
        g = jnp.dot(hn, wg_ref[:, c0:c0 + cw], preferred_element_type=f32)
        u = jnp.dot(hn, wu_ref[:, c0:c0 + cw], preferred_element_type=f32)
        a = (_silu(g) * u).astype(bf16)
        acc = acc + jnp.dot(a, wd_ref[c0:c0 + cw, :], preferred_element_type=f32)
    o_ref[...] = acc


def _dense_ffn(h, hn, wg, wu, wd, rows, chunk):
    t, d = h.shape
    f = wg.shape[1]
    return pl.pallas_call(
        functools.partial(_dense_ffn_kernel, chunk=chunk),
        grid=(t // rows,),
        in_specs=[
            pl.BlockSpec((rows, d), lambda i: (i, 0)),
            pl.BlockSpec((rows, d), lambda i: (i, 0)),
            _resident((d, f)),
            _resident((d, f)),
            _resident((f, d)),
        ],
        out_specs=pl.BlockSpec((rows, d), lambda i: (i, 0)),
        out_shape=jax.ShapeDtypeStruct((t, d), f32),
        compiler_params=_params("arbitrary"),
        name="dense_ffn",
    )(h, hn, wg, wu, wd)


def _rope(x, cos, sin_signed, lane):
    d = x.shape[-1]
    half = ROT_DIM // 2
    partner = jnp.where(lane < half, pltpu.roll(x, d - half, 1), pltpu.roll(x, half, 1))
    return x * cos + partner * sin_signed


def _qkv_kernel(h_ref, g_q_ref, g_kv_ref, wq_ref, wk_ref, wv_ref, cos_ref, sin_ref,
                qt_ref, k_ref, vt_ref, kmean_ref):
    i = pl.program_id(1)
    rows, d = h_ref.shape[1], h_ref.shape[2]
    heads = d // HEAD_DIM
    h = h_ref[0]
    xhat = h * _rms_scale(h)
    hq = (xhat * g_q_ref[...]).astype(bf16)
    hkv = (xhat * g_kv_ref[...]).astype(bf16)
    q = jnp.dot(hq, wq_ref[...], preferred_element_type=f32)
    k = jnp.dot(hkv, wk_ref[...], preferred_element_type=f32)
    v = jnp.dot(hkv, wv_ref[...], preferred_element_type=f32)
    cos = jnp.concatenate([cos_ref[...]] * heads, axis=-1)
    sin = jnp.concatenate([sin_ref[...]] * heads, axis=-1)
    lane = lax.broadcasted_iota(jnp.int32, (rows, d), 1) % HEAD_DIM
    q = _rope(q, cos, sin, lane) * (HEAD_DIM ** -0.5)
    k = _rope(k, cos, sin, lane)
    qt_ref[0] = q.T.astype(bf16)
    k_ref[0] = k.astype(bf16)
    vt_ref[0] = v.T.astype(bf16)

    nb = kmean_ref.shape[1]
    blk_iota = lax.broadcasted_iota(jnp.int32, (nb, d), 0)

    @pl.when(i == 0)
    def _():
        kmean_ref[0] = jnp.zeros((nb, d), f32)

    km = kmean_ref[0]
    for j in range(rows // BLOCK):
        mean_j = jnp.mean(k[j * BLOCK:(j + 1) * BLOCK, :], axis=0, keepdims=True)
        km = jnp.where(blk_iota == i * (rows // BLOCK) + j, mean_j, km)
    kmean_ref[0] = km


def _qkv_proj(h, g_q, g_kv, wq, wk, wv, cos, sin, rows):
    b, s, d = h.shape
    nb = s // BLOCK
    return pl.pallas_call(
        _qkv_kernel,
        grid=(b, s // rows),
        in_specs=[
            pl.BlockSpec((1, rows, d), lambda bi, i: (bi, i, 0)),
            _resident((1, d)),
            _resident((1, d)),
            _resident((d, d)),
            _resident((d, d)),
            _resident((d, d)),
            pl.BlockSpec((rows, HEAD_DIM), lambda bi, i: (i, 0)),
            pl.BlockSpec((rows, HEAD_DIM), lambda bi, i: (i, 0)),
        ],
        out_specs=[
            pl.BlockSpec((1, d, rows), lambda bi, i: (bi, 0, i)),
            pl.BlockSpec((1, rows, d), lambda bi, i: (bi, i, 0)),
            pl.BlockSpec((1, d, rows), lambda bi, i: (bi, 0, i)),
            pl.BlockSpec((1, nb, d), lambda bi, i: (bi, 0, 0)),
        ],
        out_shape=[
            jax.ShapeDtypeStruct((b, d, s), bf16),
            jax.ShapeDtypeStruct((b, s, d), bf16),
            jax.ShapeDtypeStruct((b, d, s), bf16),
            jax.ShapeDtypeStruct((b, nb, d), f32),
        ],
        compiler_params=_params("arbitrary", "arbitrary"),
        name="qkv_proj",
    )(h, g_q, g_kv, wq, wk, wv, cos, sin)


def _moba_kernel(qt_ref, k_ref, vt_ref, kmean_ref, o_ref, *scratch, heads):
    sel_refs, m_refs, l_refs, acc_refs = (scratch[i * heads:(i + 1) * heads] for i in range(4))
    j = pl.program_id(1)
    nb = kmean_ref.shape[1]
    hd = HEAD_DIM
    blk = lax.broadcasted_iota(jnp.int32, (nb, BLOCK), 0)
    past = blk < j
    kpos = lax.broadcasted_iota(jnp.int32, (BLOCK, BLOCK), 0)
    qpos = lax.broadcasted_iota(jnp.int32, (BLOCK, BLOCK), 1)
    causal = kpos <= qpos
    start = pl.multiple_of(j * BLOCK, BLOCK)

    def q_head(h):
        return qt_ref[0, h * hd:(h + 1) * hd, :]

    def scores(off):
        return [jnp.dot(k_ref[0, pl.ds(off, BLOCK), h * hd:(h + 1) * hd], q_head(h),
                        preferred_element_type=f32) for h in range(heads)]

    def weighted_values(ps, off):
        return [jnp.dot(vt_ref[0, h * hd:(h + 1) * hd, pl.ds(off, BLOCK)], ps[h],
                        preferred_element_type=f32) for h in range(heads)]

    for h in range(heads):
        gate = jnp.dot(kmean_ref[0, :, h * hd:(h + 1) * hd], q_head(h).astype(f32),
                       precision=lax.Precision.HIGHEST, preferred_element_type=f32)
        gate = jnp.where(past, gate, NEG_INF)
        rank = jnp.zeros((nb, BLOCK), jnp.int32)
        for m in range(nb):
            gm = gate[m:m + 1, :]
            rank = rank + jnp.where(gm > gate, 1, jnp.where(gm == gate, jnp.where(blk > m, 1, 0), 0))
        sel_refs[h][...] = jnp.where(past, jnp.where(rank < TOP_BLOCKS, 1.0, 0.0), 0.0)

    ps = []
    for h, s in enumerate(scores(start)):
        s = jnp.where(causal, s, NEG_INF)
        m0 = jnp.max(s, axis=0, keepdims=True)
        p = jnp.exp(s - m0)
        m_refs[h][...] = m0
        l_refs[h][...] = jnp.sum(p, axis=0, keepdims=True)
        ps.append(p.astype(bf16))
    for h, pv in enumerate(weighted_values(ps, start)):
        acc_refs[h][...] = pv

    def body(n, carry):
        off = pl.multiple_of(n * BLOCK, BLOCK)
        ps, alphas = [], []
        for h, s in enumerate(scores(off)):
            s = jnp.where(sel_refs[h][pl.ds(n, 1), :] > 0.0, s, NEG_INF)
            m_old = m_refs[h][...]
            m_new = jnp.maximum(m_old, jnp.max(s, axis=0, keepdims=True))
            alpha = jnp.exp(m_old - m_new)
            p = jnp.exp(s - m_new)
            m_refs[h][...] = m_new
            l_refs[h][...] = alpha * l_refs[h][...] + jnp.sum(p, axis=0, keepdims=True)
            ps.append(p.astype(bf16))
            alphas.append(alpha)
        for h, pv in enumerate(weighted_values(ps, off)):
            acc_refs[h][...] = alphas[h] * acc_refs[h][...] + pv
        return carry

    lax.fori_loop(0, j, body, 0)
    for h in range(heads):
        o_ref[0, :, h * hd:(h + 1) * hd] = (acc_refs[h][...] / l_refs[h][...]).T.astype(o_ref.dtype)


def _moba_attention(qt, k, vt, kmean):
    b, d, s = qt.shape
    heads = d // HEAD_DIM
    nb = s // BLOCK
    return pl.pallas_call(
        functools.partial(_moba_kernel, heads=heads),
        grid=(b, nb),
        in_specs=[
            pl.BlockSpec((1, d, BLOCK), lambda bi, j: (bi, 0, j)),
            pl.BlockSpec((1, s, d), lambda bi, j: (bi, 0, 0), pipeline_mode=pl.Buffered(1)),
            pl.BlockSpec((1, d, s), lambda bi, j: (bi, 0, 0), pipeline_mode=pl.Buffered(1)),
            pl.BlockSpec((1, nb, d), lambda bi, j: (bi, 0, 0)),
        ],
        out_specs=pl.BlockSpec((1, BLOCK, d), lambda bi, j: (bi, j, 0)),
        out_shape=jax.ShapeDtypeStruct((b, s, d), bf16),
        scratch_shapes=(
            [pltpu.VMEM((nb, BLOCK), f32)] * heads
            + [pltpu.VMEM((1, BLOCK), f32)] * (2 * heads)
            + [pltpu.VMEM((HEAD_DIM, BLOCK), f32)] * heads
        ),
        compiler_params=_params("arbitrary", "arbitrary"),
        name="moba_attn",
    )(qt, k, vt, kmean)


def _out_router_kernel(o_ref, h_ref, wo_ref, g_ref, wr_ref, h_out_ref, hn_ref, idx_ref, gate_ref):
    rows = o_ref.shape[0]
    n_exp = wr_ref.shape[1]
    h = h_ref[...] + jnp.dot(o_ref[...], wo_ref[...], preferred_element_type=f32)
    h_out_ref[...] = h
    hn = h * _rms_scale(h) * g_ref[...]
    _store_token_tiles(hn_ref, hn)
    logits = jnp.dot(hn, wr_ref[...], precision=lax.Precision.HIGHEST, preferred_element_type=f32)
    e = lax.broadcasted_iota(jnp.int32, (rows, n_exp), 1)
    v1 = jnp.max(logits, axis=-1, keepdims=True)
    i1 = jnp.min(jnp.where(logits == v1, e, n_exp), axis=-1, keepdims=True)
    rest = jnp.where(e == i1, NEG_INF, logits)
    v2 = jnp.max(rest, axis=-1, keepdims=True)
    i2 = jnp.min(jnp.where(rest == v2, e, n_exp), axis=-1, keepdims=True)
    e2 = jnp.exp(v2 - v1)
    denom = 1.0 + e2
    col = lax.broadcasted_iota(jnp.int32, (rows, TOP_K), 1)
    idx_ref[...] = jnp.where(col == 0, i1, i2)
    gate_ref[...] = jnp.where(col == 0, 1.0 / denom, e2 / denom)


def _out_router(o, h, wo, g_ffn, w_router, rows):
    t, d = h.shape
    n_exp = w_router.shape[1]
    return pl.pallas_call(
        _out_router_kernel,
        grid=(t // rows,),
        in_specs=[
            pl.BlockSpec((rows, d), lambda i: (i, 0)),
            pl.BlockSpec((rows, d), lambda i: (i, 0)),
            _resident((d, d)),
            _resident((1, d)),
            _resident((d, n_exp)),
        ],
        out_specs=[
            pl.BlockSpec((rows, d), lambda i: (i, 0)),
            pl.BlockSpec((rows * (d // LANES), LANES), lambda i: (i, 0)),
            pl.BlockSpec((rows, TOP_K), lambda i: (i, 0)),
            pl.BlockSpec((rows, TOP_K), lambda i: (i, 0)),
        ],
        out_shape=[
            jax.ShapeDtypeStruct((t, d), f32),
            jax.ShapeDtypeStruct((t * (d // LANES), LANES), f32),
            jax.ShapeDtypeStruct((t, TOP_K), jnp.int32),
            jax.ShapeDtypeStruct((t, TOP_K), f32),
        ],
        compiler_params=_params("arbitrary"),
        name="out_router",
    )(o, h, wo, g_ffn, w_router)


def _moe_gemm_kernel(tile_expert_ref, tile_valid_ref, tok_ref, x_hbm, wg_ref, wu_ref, wd_ref,
                     y_ref, xf_ref, xb_ref, acc_ref, sem):
    i = pl.program_id(0)
    f = pl.program_id(1)
    rows, d = xb_ref.shape
    sub = d // LANES
    valid = tile_valid_ref[i] > 0
    slot = i % 2

    def gather_rows(tile, buf):
        base = tile * rows

        def trip(c, carry):
            for u in range(DMA_UNROLL):
                r = c * DMA_UNROLL + u
                src = pl.multiple_of(tok_ref[base + r] * sub, sub)
                pltpu.make_async_copy(x_hbm.at[pl.ds(src, sub), :],
                                      xf_ref.at[buf, pl.ds(pl.multiple_of(r * sub, sub), sub), :],
                                      sem.at[buf]).start()
            return carry
        lax.fori_loop(0, rows // DMA_UNROLL, trip, 0)

    @pl.when(jnp.logical_and(f == 0, jnp.logical_and(i == 0, valid)))
    def _():
        gather_rows(0, 0)

    @pl.when(jnp.logical_and(f == 0, valid))
    def _():
        pltpu.make_async_copy(x_hbm.at[pl.ds(0, rows * sub), :], xf_ref.at[slot], sem.at[slot]).wait()
        for c in range(sub):
            xb_ref[:, c * LANES:(c + 1) * LANES] = _load_token_tiles(xf_ref.at[slot], rows, sub, c).astype(bf16)
        acc_ref[...] = jnp.zeros(acc_ref.shape, f32)

        @pl.when(tile_valid_ref[i + 1] > 0)
        def _():
            gather_rows(i + 1, 1 - slot)

    @pl.when(valid)
    def _():
        xb = xb_ref[...]
        g = jnp.dot(xb, wg_ref[0], preferred_element_type=f32)
        u = jnp.dot(xb, wu_ref[0], preferred_element_type=f32)
        a = (_silu(g) * u).astype(bf16)
        acc_ref[...] += jnp.dot(a, wd_ref[0], preferred_element_type=f32)

    @pl.when(f == pl.num_programs(1) - 1)
    def _():
        _store_token_tiles(y_ref, jnp.where(valid, acc_ref[...], 0.0))


def _moe_gemm(tile_expert, tile_valid, tok, x_tiles, wg, wu, wd, rows, chunk):
    n_tiles = tile_expert.shape[0]
    d = wg.shape[1]
    sub = d // LANES
    fe = wg.shape[2]
    grid_spec = pltpu.PrefetchScalarGridSpec(
        num_scalar_prefetch=3,
        grid=(n_tiles, fe // chunk),
        in_specs=[
            pl.BlockSpec(memory_space=pl.ANY),
            pl.BlockSpec((1, d, chunk), lambda i, f, te, tv, tk: (te[i], 0, f)),
            pl.BlockSpec((1, d, chunk), lambda i, f, te, tv, tk: (te[i], 0, f)),
            pl.BlockSpec((1, chunk, d), lambda i, f, te, tv, tk: (te[i], f, 0)),
        ],
        out_specs=pl.BlockSpec((rows * sub, LANES), lambda i, f, te, tv, tk: (i, 0)),
        scratch_shapes=[
            pltpu.VMEM((2, rows * sub, LANES), f32),
            pltpu.VMEM((rows, d), bf16),
            pltpu.VMEM((rows, d), f32),
            pltpu.SemaphoreType.DMA((2,)),
        ],
    )
    return pl.pallas_call(
        _moe_gemm_kernel,
        grid_spec=grid_spec,
        out_shape=jax.ShapeDtypeStruct((n_tiles * rows * sub, LANES), f32),
        compiler_params=_params("arbitrary", "arbitrary", disable_bounds_checks=True),
        name="moe_gemm",
    )(tile_expert, tile_valid, tok, x_tiles, wg, wu, wd)


def _combine_kernel(pos_ref, h_ref, gate_ref, g_ref, y_hbm, o_ref, ybuf_ref, sem):
    i = pl.program_id(0)
    rows, d = h_ref.shape
    sub = d // LANES
    slot = i % 2

    def gather_rows(step, buf):
        base = step * (TOP_K * rows)

        def trip(c, carry):
            for u in range(DMA_UNROLL):
                r = c * DMA_UNROLL + u
                for k in range(TOP_K):
                    src = pl.multiple_of(pos_ref[base + k * rows + r] * sub, sub)
                    pltpu.make_async_copy(y_hbm.at[pl.ds(src, sub), :],
                                          ybuf_ref.at[buf, k, pl.ds(pl.multiple_of(r * sub, sub), sub), :],
                                          sem.at[buf]).start()
            return carry
        lax.fori_loop(0, rows // DMA_UNROLL, trip, 0)

    @pl.when(i == 0)
    def _():
        gather_rows(0, 0)

    @pl.when(i + 1 < pl.num_programs(0))
    def _():
        gather_rows(i + 1, 1 - slot)

    for k in range(TOP_K):
        pltpu.make_async_copy(y_hbm.at[pl.ds(0, rows * sub), :], ybuf_ref.at[slot, k], sem.at[slot]).wait()
    gate = gate_ref[...]
    chunks = []
    sumsq = jnp.zeros((rows, 1), f32)
    for c in range(sub):
        hc = h_ref[:, c * LANES:(c + 1) * LANES]
        for k in range(TOP_K):
            hc = hc + gate[:, k:k + 1] * _load_token_tiles(ybuf_ref.at[slot, k], rows, sub, c)
        sumsq = sumsq + jnp.sum(hc * hc, axis=-1, keepdims=True)
        chunks.append(hc)
    scale = lax.rsqrt(sumsq / d + EPS)
    for c in range(sub):
        o_ref[:, c * LANES:(c + 1) * LANES] = chunks[c] * scale * g_ref[:, c * LANES:(c + 1) * LANES]


def _moe_combine(pos, h, gates, g_final, y_tiles, rows):
    t, d = h.shape
    sub = d // LANES
    grid_spec = pltpu.PrefetchScalarGridSpec(
        num_scalar_prefetch=1,
        grid=(t // rows,),
        in_specs=[
            pl.BlockSpec((rows, d), lambda i, ps: (i, 0)),
            pl.BlockSpec((rows, TOP_K), lambda i, ps: (i, 0)),
            pl.BlockSpec((1, d), lambda i, ps: (0, 0)),
            pl.BlockSpec(memory_space=pl.ANY),
        ],
        out_specs=pl.BlockSpec((rows, d), lambda i, ps: (i, 0)),
        scratch_shapes=[pltpu.VMEM((2, TOP_K, rows * sub, LANES), f32), pltpu.SemaphoreType.DMA((2,))],
    )
    return pl.pallas_call(
        _combine_kernel,
        grid_spec=grid_spec,
        out_shape=jax.ShapeDtypeStruct((t, d), f32),
        compiler_params=_params("arbitrary", disable_bounds_checks=True),
        name="moe_combine",
    )(pos, h, gates, g_final, y_tiles)


def _routing_plan(idx, n_exp, rows, comb_rows):
    t = idx.shape[0]
    pairs = t * TOP_K
    n_tiles = -(-pairs // rows) + n_exp
    e_flat = idx.T.reshape(pairs)
    onehot = (e_flat[:, None] == jnp.arange(n_exp, dtype=jnp.int32)[None, :]).astype(jnp.int32)
    csum = jnp.cumsum(onehot, axis=0)
    counts = csum[-1]
    rank = jnp.sum(onehot * csum, axis=1) - 1
    padded = ((counts + rows - 1) // rows) * rows
    pend = jnp.cumsum(padded)
    poff = pend - padded
    pos = poff[e_flat] + rank
    tok = jnp.zeros((n_tiles * rows,), jnp.int32).at[pos].set(
        jnp.arange(pairs, dtype=jnp.int32) % t, unique_indices=True)
    tile_start = jnp.arange(n_tiles, dtype=jnp.int32) * rows
    tile_expert = jnp.minimum(jnp.sum((tile_start[:, None] >= pend[None, :]).astype(jnp.int32), axis=1),
                              n_exp - 1)
    tile_valid = (tile_start < pend[-1]).astype(jnp.int32)
    last_valid = jnp.maximum(jnp.sum(tile_valid) - 1, 0)
    tile_expert = jnp.where(tile_valid > 0, tile_expert, tile_expert[last_valid])
    n_steps = t // comb_rows
    pos_steps = pos.reshape(TOP_K, n_steps, comb_rows).transpose(1, 0, 2).reshape(pairs)
    return tile_expert, tile_valid, tok, pos_steps


def _rope_tables(s):
    half = ROT_DIM // 2
    inv = ROPE_THETA ** (-jnp.arange(half, dtype=f32) * 2.0 / ROT_DIM)
    ang = jnp.arange(s).astype(f32)[:, None] * inv[None, :]
    cos, sin = jnp.cos(ang), jnp.sin(ang)
    pad = HEAD_DIM - ROT_DIM
    cos_t = jnp.concatenate([cos, cos, jnp.ones((s, pad), f32)], axis=-1)
    sin_t = jnp.concatenate([-sin, sin, jnp.zeros((s, pad), f32)], axis=-1)
    return cos_t, sin_t


def kernel(x, attn_norm, ffn_norm, pool_w, pool_scale, wq, wo, kv_norm, wk, wv, ffn_w_gate, ffn_w_up,
           ffn_w_down, router_w, moe_w_gate, moe_w_up, moe_w_down, final_norm):
    b, s, d = x.shape
    t = b * s
    n_exp = router_w.shape[-1]
    assert attn_norm.shape[0] == 2 and pool_w.shape[0] == 1 and wq.shape[0] == 1 and router_w.shape[0] == 1
    assert s % 512 == 0 and d % (HEAD_DIM * 2) == 0

    tok_rows = 512
    moe_chunk = 512
    moe_rows = 1024
    comb_rows = 256
    assert moe_w_gate.shape[-1] % moe_chunk == 0 and d % (8 * LANES) == 0

    row = lambda v: v.reshape(1, -1)

    h1, hn1 = _pool_mixer(x, row(attn_norm[0]), pool_w[0].astype(bf16), row(pool_scale[0]),
                          row(ffn_norm[0]), tok_rows)
    h2 = _dense_ffn(h1.reshape(t, d), hn1.reshape(t, d), ffn_w_gate[0].astype(bf16),
                    ffn_w_up[0].astype(bf16), ffn_w_down[0].astype(bf16), tok_rows, 512)

    cos_t, sin_t = _rope_tables(s)
    qt, k, vt, kmean = _qkv_proj(h2.reshape(b, s, d), row(attn_norm[1]), row(kv_norm), wq[0].astype(bf16),
                                 wk.astype(bf16), wv.astype(bf16), cos_t, sin_t, tok_rows)
    o = _moba_attention(qt, k, vt, kmean)

    h3, hn3, idx, gates = _out_router(o.reshape(t, d), h2, wo[0].astype(bf16), row(ffn_norm[1]),
                                      router_w[0], tok_rows)

    tile_expert, tile_valid, tok, pos = _routing_plan(idx, n_exp, moe_rows, comb_rows)
    y_tiles = _moe_gemm(tile_expert, tile_valid, tok, hn3, moe_w_gate[0].astype(bf16),
                        moe_w_up[0].astype(bf16), moe_w_down[0].astype(bf16), moe_rows, moe_chunk)
    out = _moe_combine(pos, h3, gates, row(final_norm), y_tiles, comb_rows)
    return out.reshape(b, s, d)
```

```python
import functools

import jax
import jax.numpy as jnp
from jax import lax
from jax.experimental import pallas as pl
from jax.experimental.pallas import tpu as pltpu

EPS = 1e-6
POOL_WINDOWS = (2, 4, 8, 16)
POOL_HALO = 16
LANES = 128
HEAD_DIM = 128
BLOCK = 256
TOP_BLOCKS = 3
ROT_DIM = HEAD_DIM // 4
ROPE_THETA = 500000.0
TOP_K = 2

VMEM_LIMIT = 56 * 1024 * 1024
DMA_UNROLL = 8
NEG_INF = float("-inf")
LOG2_E = 1.4426950408889634

f32 = jnp.float32
bf16 = jnp.bfloat16


def _resident(shape):
    n = len(shape)
    return pl.BlockSpec(shape, lambda *_: (0,) * n, pipeline_mode=pl.Buffered(1))


def _rms_scale(xf):
    return lax.rsqrt(jnp.mean(xf * xf, axis=-1, keepdims=True) + EPS)


def _silu(g):
    return g * (1.0 / (1.0 + jnp.exp(-g)))


def _store_token_tiles(ref, x):
    rows, d = x.shape
    sub = d // LANES
    for c in range(sub):
        ref[pl.ds(c, rows, stride=sub), :] = x[:, c * LANES:(c + 1) * LANES]


def _load_token_tiles(ref, rows, sub, chunk):
    return ref[pl.ds(chunk, rows, stride=sub), :]


def _params(*sem, **kw):
    return pltpu.CompilerParams(dimension_semantics=sem, vmem_limit_bytes=VMEM_LIMIT, **kw)


def _pool_kernel(x_ref, halo_ref, g_attn_ref, w_ref, scale_ref, g_ffn_ref, h_ref, hn_ref, ext_ref):
    i = pl.program_id(1)
    rows = x_ref.shape[1]
    d = x_ref.shape[2]
    c = d // len(POOL_WINDOWS)
    g_attn = g_attn_ref[...]
    x = x_ref[0]
    xn = x * _rms_scale(x) * g_attn
    halo = halo_ref[0]
    halo = halo * _rms_scale(halo) * g_attn
    ext_ref[0:POOL_HALO, :] = jnp.where(i == 0, 0.0, halo)
    ext_ref[POOL_HALO:, :] = xn
    t = i * rows + lax.broadcasted_iota(jnp.int32, (rows, 1), 0)
    ys = []
    for g, w in enumerate(POOL_WINDOWS):
        c0 = g * c
        s = ext_ref[pl.ds(POOL_HALO, rows), c0:c0 + c]
        for k in range(1, w):
            s = s + ext_ref[pl.ds(POOL_HALO - k, rows), c0:c0 + c]
        cnt = jnp.minimum(t + 1, w).astype(f32)
        pooled = s / cnt - xn[:, c0:c0 + c]
        ys.append(jnp.dot(pooled.astype(bf16), w_ref[g], preferred_element_type=f32))
    y = jnp.concatenate(ys, axis=-1) * scale_ref[...]
    h = x + y
    h_ref[0] = h
    hn_ref[0] = (h * _rms_scale(h) * g_ffn_ref[...]).astype(bf16)


def _pool_mixer(x, g_attn, w_pool, scale, g_ffn, rows):
    b, s, d = x.shape
    c = d // len(POOL_WINDOWS)
    hb = rows // POOL_HALO
    return pl.pallas_call(
        _pool_kernel,
        grid=(b, s // rows),
        in_specs=[
            pl.BlockSpec((1, rows, d), lambda bi, i: (bi, i, 0)),
            pl.BlockSpec((1, POOL_HALO, d), lambda bi, i: (bi, jnp.maximum(i * hb - 1, 0), 0)),
            _resident((1, d)),
            _resident((len(POOL_WINDOWS), c, c)),
            _resident((1, d)),
            _resident((1, d)),
        ],
        out_specs=[
            pl.BlockSpec((1, rows, d), lambda bi, i: (bi, i, 0)),
            pl.BlockSpec((1, rows, d), lambda bi, i: (bi, i, 0)),
        ],
        out_shape=[jax.ShapeDtypeStruct((b, s, d), f32), jax.ShapeDtypeStruct((b, s, d), bf16)],
        scratch_shapes=[pltpu.VMEM((rows + POOL_HALO, d), f32)],
        compiler_params=_params("arbitrary", "arbitrary"),
        name="pool_mixer",
    )(x, x, g_attn, w_pool, scale, g_ffn)


def _ffn_chunks(f, size):
    out, c0 = [], 0
    while c0 < f:
        out.append((c0, min(size, f - c0)))
        c0 += size
    return out


def _dense_ffn_kernel(h_ref, hn_ref, wg_ref, wu_ref, wd_ref, o_ref, *, chunk):
    hn = hn_ref[...]
    acc = h_ref[...]
    for c0, cw in _ffn_chunks(wg_ref.shape[1], chunk):
        g = jnp.dot(hn, wg_ref[:, c0:c0 + cw], preferred_element_type=f32)
        u = jnp.dot(hn, wu_ref[:, c0:c0 + cw], preferred_element_type=f32)
        a = (_silu(g) * u).astype(bf16)
        acc = acc + jnp.dot(a, wd_ref[c0:c0 + cw, :], preferred_element_type=f32)
    o_ref[...] = acc


def _dense_ffn(h, hn, wg, wu, wd, rows, chunk):
    t, d = h.shape
    f = wg.shape[1]
    return pl.pallas_call(
        functools.partial(_dense_ffn_kernel, chunk=chunk),
        grid=(t // rows,),
        in_specs=[
            pl.BlockSpec((rows, d), lambda i: (i, 0)),
            pl.BlockSpec((rows, d), lambda i: (i, 0)),
            _resident((d, f)),
            _resident((d, f)),
            _resident((f, d)),
        ],
        out_specs=pl.BlockSpec((rows, d), lambda i: (i, 0)),
        out_shape=jax.ShapeDtypeStruct((t, d), f32),
        compiler_params=_params("arbitrary"),
        name="dense_ffn",
    )(h, hn, wg, wu, wd)


def _rope(x, cos, sin_signed, lane):
    d = x.shape[-1]
    half = ROT_DIM // 2
    partner = jnp.where(lane < half, pltpu.roll(x, d - half, 1), pltpu.roll(x, half, 1))
    return x * cos + partner * sin_signed


def _qkv_kernel(h_ref, g_q_ref, g_kv_ref, wq_ref, wk_ref, wv_ref, cos_ref, sin_ref,
                qt_ref, k_ref, vt_ref, kmean_ref):
    i = pl.program_id(1)
    rows, d = h_ref.shape[1], h_ref.shape[2]
    heads = d // HEAD_DIM
    h = h_ref[0]
    xhat = h * _rms_scale(h)
    hq = (xhat * g_q_ref[...]).astype(bf16)
    hkv = (xhat * g_kv_ref[...]).astype(bf16)
    q = jnp.dot(hq, wq_ref[...], preferred_element_type=f32)
    k = jnp.dot(hkv, wk_ref[...], preferred_element_type=f32)
    v = jnp.dot(hkv, wv_ref[...], preferred_element_type=f32)
    cos = jnp.concatenate([cos_ref[...]] * heads, axis=-1)
    sin = jnp.concatenate([sin_ref[...]] * heads, axis=-1)
    lane = lax.broadcasted_iota(jnp.int32, (rows, d), 1) % HEAD_DIM
    q = _rope(q, cos, sin, lane) * (HEAD_DIM ** -0.5 * LOG2_E)
    k = _rope(k, cos, sin, lane)
    qt_ref[0] = q.T.astype(bf16)
    k_ref[0] = k.astype(bf16)
    vt_ref[0] = v.T.astype(bf16)

    nb = kmean_ref.shape[1]
    blk_iota = lax.broadcasted_iota(jnp.int32, (nb, d), 0)

    @pl.when(i == 0)
    def _():
        kmean_ref[0] = jnp.zeros((nb, d), f32)

    km = kmean_ref[0]
    for j in range(rows // BLOCK):
        mean_j = jnp.mean(k[j * BLOCK:(j + 1) * BLOCK, :], axis=0, keepdims=True)
        km = jnp.where(blk_iota == i * (rows // BLOCK) + j, mean_j, km)
    kmean_ref[0] = km


def _qkv_proj(h, g_q, g_kv, wq, wk, wv, cos, sin, rows):
    b, s, d = h.shape
    nb = s // BLOCK
    return pl.pallas_call(
        _qkv_kernel,
        grid=(b, s // rows),
        in_specs=[
            pl.BlockSpec((1, rows, d), lambda bi, i: (bi, i, 0)),
            _resident((1, d)),
            _resident((1, d)),
            _resident((d, d)),
            _resident((d, d)),
            _resident((d, d)),
            pl.BlockSpec((rows, HEAD_DIM), lambda bi, i: (i, 0)),
            pl.BlockSpec((rows, HEAD_DIM), lambda bi, i: (i, 0)),
        ],
        out_specs=[
            pl.BlockSpec((1, d, rows), lambda bi, i: (bi, 0, i)),
            pl.BlockSpec((1, rows, d), lambda bi, i: (bi, i, 0)),
            pl.BlockSpec((1, d, rows), lambda bi, i: (bi, 0, i)),
            pl.BlockSpec((1, nb, d), lambda bi, i: (bi, 0, 0)),
        ],
        out_shape=[
            jax.ShapeDtypeStruct((b, d, s), bf16),
            jax.ShapeDtypeStruct((b, s, d), bf16),
            jax.ShapeDtypeStruct((b, d, s), bf16),
            jax.ShapeDtypeStruct((b, nb, d), f32),
        ],
        compiler_params=_params("arbitrary", "arbitrary"),
        name="qkv_proj",
    )(h, g_q, g_kv, wq, wk, wv, cos, sin)


def _moba_kernel(qt_ref, k_ref, vt_ref, kmean_ref, o_ref, *scratch, heads):
    sel_refs, m_refs, l_refs, acc_refs = (scratch[i * heads:(i + 1) * heads] for i in range(4))
    j = pl.program_id(1)
    nb = kmean_ref.shape[1]
    hd = HEAD_DIM
    blk = lax.broadcasted_iota(jnp.int32, (nb, BLOCK), 0)
    past = blk < j
    kpos = lax.broadcasted_iota(jnp.int32, (BLOCK, BLOCK), 0)
    qpos = lax.broadcasted_iota(jnp.int32, (BLOCK, BLOCK), 1)
    causal = kpos <= qpos
    start = pl.multiple_of(j * BLOCK, BLOCK)

    def q_head(h):
        return qt_ref[0, h * hd:(h + 1) * hd, :]

    def scores(off):
        return [jnp.dot(k_ref[0, pl.ds(off, BLOCK), h * hd:(h + 1) * hd], q_head(h),
                        preferred_element_type=f32) for h in range(heads)]

    def weighted_values(ps, off):
        return [jnp.dot(vt_ref[0, h * hd:(h + 1) * hd, pl.ds(off, BLOCK)], ps[h],
                        preferred_element_type=f32) for h in range(heads)]

    for h in range(heads):
        gate = jnp.dot(kmean_ref[0, :, h * hd:(h + 1) * hd], q_head(h).astype(f32),
                       precision=lax.Precision.HIGHEST, preferred_element_type=f32)
        gate = jnp.where(past, gate, NEG_INF)
        rank = jnp.zeros((nb, BLOCK), jnp.int32)
        for m in range(nb):
            gm = gate[m:m + 1, :]
            rank = rank + jnp.where(gm > gate, 1, jnp.where(gm == gate, jnp.where(blk > m, 1, 0), 0))
        sel_refs[h][...] = jnp.where(past, jnp.where(rank < TOP_BLOCKS, 1.0, 0.0), 0.0)

    ps = []
    for h, s in enumerate(scores(start)):
        s = jnp.where(causal, s, NEG_INF)
        m0 = jnp.max(s, axis=0, keepdims=True)
        p = jnp.exp2(s - m0)
        m_refs[h][...] = m0
        l_refs[h][...] = jnp.sum(p, axis=0, keepdims=True)
        ps.append(p.astype(bf16))
    for h, pv in enumerate(weighted_values(ps, start)):
        acc_refs[h][...] = pv

    def body(n, carry):
        off = pl.multiple_of(n * BLOCK, BLOCK)
        ps, alphas = [], []
        for h, s in enumerate(scores(off)):
            s = jnp.where(sel_refs[h][pl.ds(n, 1), :] > 0.0, s, NEG_INF)
            m_old = m_refs[h][...]
            m_new = jnp.maximum(m_old, jnp.max(s, axis=0, keepdims=True))
            alpha = jnp.exp2(m_old - m_new)
            p = jnp.exp2(s - m_new)
            m_refs[h][...] = m_new
            l_refs[h][...] = alpha * l_refs[h][...] + jnp.sum(p, axis=0, keepdims=True)
            ps.append(p.astype(bf16))
            alphas.append(alpha)
        for h, pv in enumerate(weighted_values(ps, off)):
            acc_refs[h][...] = alphas[h] * acc_refs[h][...] + pv
        return carry

    lax.fori_loop(0, j, body, 0)
    for h in range(heads):
        o_ref[0, :, h * hd:(h + 1) * hd] = (acc_refs[h][...] / l_refs[h][...]).T.astype(o_ref.dtype)


def _moba_attention(qt, k, vt, kmean):
    b, d, s = qt.shape
    heads = d // HEAD_DIM
    nb = s // BLOCK
    return pl.pallas_call(
        functools.partial(_moba_kernel, heads=heads),
        grid=(b, nb),
        in_specs=[
            pl.BlockSpec((1, d, BLOCK), lambda bi, j: (bi, 0, j)),
            pl.BlockSpec((1, s, d), lambda bi, j: (bi, 0, 0), pipeline_mode=pl.Buffered(1)),
            pl.BlockSpec((1, d, s), lambda bi, j: (bi, 0, 0), pipeline_mode=pl.Buffered(1)),
            pl.BlockSpec((1, nb, d), lambda bi, j: (bi, 0, 0)),
        ],
        out_specs=pl.BlockSpec((1, BLOCK, d), lambda bi, j: (bi, j, 0)),
        out_shape=jax.ShapeDtypeStruct((b, s, d), bf16),
        scratch_shapes=(
            [pltpu.VMEM((nb, BLOCK), f32)] * heads
            + [pltpu.VMEM((1, BLOCK), f32)] * (2 * heads)
            + [pltpu.VMEM((HEAD_DIM, BLOCK), f32)] * heads
        ),
        compiler_params=_params("arbitrary", "arbitrary"),
        name="moba_attn",
    )(qt, k, vt, kmean)


def _out_router_kernel(o_ref, h_ref, wo_ref, g_ref, wr_ref, h_out_ref, hn_ref, idx_ref, gate_ref):
    rows = o_ref.shape[0]
    n_exp = wr_ref.shape[1]
    h = h_ref[...] + jnp.dot(o_ref[...], wo_ref[...], preferred_element_type=f32)
    h_out_ref[...] = h
    hn = h * _rms_scale(h) * g_ref[...]
    _store_token_tiles(hn_ref, hn)
    wr = wr_ref[...]
    w_hi = wr.astype(bf16)
    w_lo = (wr - w_hi.astype(f32)).astype(bf16)
    x_hi = hn.astype(bf16)
    x_lo = (hn - x_hi.astype(f32)).astype(bf16)
    both = jnp.dot(x_hi, jnp.concatenate([w_hi, w_lo], axis=1), preferred_element_type=f32)
    logits = both[:, :n_exp] + both[:, n_exp:] + jnp.dot(x_lo, w_hi, preferred_element_type=f32)
    e = lax.broadcasted_iota(jnp.int32, (rows, n_exp), 1)
    v1 = jnp.max(logits, axis=-1, keepdims=True)
    i1 = jnp.min(jnp.where(logits == v1, e, n_exp), axis=-1, keepdims=True)
    rest = jnp.where(e == i1, NEG_INF, logits)
    v2 = jnp.max(rest, axis=-1, keepdims=True)
    i2 = jnp.min(jnp.where(rest == v2, e, n_exp), axis=-1, keepdims=True)
    e2 = jnp.exp(v2 - v1)
    denom = 1.0 + e2
    col = lax.broadcasted_iota(jnp.int32, (rows, TOP_K), 1)
    idx_ref[...] = jnp.where(col == 0, i1, i2)
    gate_ref[...] = jnp.where(col == 0, 1.0 / denom, e2 / denom)


def _out_router(o, h, wo, g_ffn, w_router, rows):
    t, d = h.shape
    n_exp = w_router.shape[1]
    return pl.pallas_call(
        _out_router_kernel,
        grid=(t // rows,),
        in_specs=[
            pl.BlockSpec((rows, d), lambda i: (i, 0)),
            pl.BlockSpec((rows, d), lambda i: (i, 0)),
            _resident((d, d)),
            _resident((1, d)),
            _resident((d, n_exp)),
        ],
        out_specs=[
            pl.BlockSpec((rows, d), lambda i: (i, 0)),
            pl.BlockSpec((rows * (d // LANES), LANES), lambda i: (i, 0)),
            pl.BlockSpec((rows, TOP_K), lambda i: (i, 0)),
            pl.BlockSpec((rows, TOP_K), lambda i: (i, 0)),
        ],
        out_shape=[
            jax.ShapeDtypeStruct((t, d), f32),
            jax.ShapeDtypeStruct((t * (d // LANES), LANES), f32),
            jax.ShapeDtypeStruct((t, TOP_K), jnp.int32),
            jax.ShapeDtypeStruct((t, TOP_K), f32),
        ],
        compiler_params=_params("arbitrary"),
        name="out_router",
    )(o, h, wo, g_ffn, w_router)


def _moe_gemm_kernel(tile_expert_ref, tile_valid_ref, tok_ref, x_hbm, wg_ref, wu_ref, wd_ref,
                     y_ref, xf_ref, xb_ref, acc_ref, sem):
    i = pl.program_id(0)
    f = pl.program_id(1)
    rows, d = xb_ref.shape
    sub = d // LANES
    valid = tile_valid_ref[i] > 0
    slot = i % 2

    def gather_rows(tile, buf):
        base = tile * rows

        def trip(c, carry):
            for u in range(DMA_UNROLL):
                r = c * DMA_UNROLL + u
                src = pl.multiple_of(tok_ref[base + r] * sub, sub)
                pltpu.make_async_copy(x_hbm.at[pl.ds(src, sub), :],
                                      xf_ref.at[buf, pl.ds(pl.multiple_of(r * sub, sub), sub), :],
                                      sem.at[buf]).start()
            return carry
        lax.fori_loop(0, rows // DMA_UNROLL, trip, 0)

    @pl.when(jnp.logical_and(f == 0, jnp.logical_and(i == 0, valid)))
    def _():
        gather_rows(0, 0)

    @pl.when(jnp.logical_and(f == 0, valid))
    def _():
        pltpu.make_async_copy(x_hbm.at[pl.ds(0, rows * sub), :], xf_ref.at[slot], sem.at[slot]).wait()
        for c in range(sub):
            xb_ref[:, c * LANES:(c + 1) * LANES] = _load_token_tiles(xf_ref.at[slot], rows, sub, c).astype(bf16)
        acc_ref[...] = jnp.zeros(acc_ref.shape, f32)

        @pl.when(tile_valid_ref[i + 1] > 0)
        def _():
            gather_rows(i + 1, 1 - slot)

    @pl.when(valid)
    def _():
        xb = xb_ref[...]
        g = jnp.dot(xb, wg_ref[0, 0], preferred_element_type=f32)
        u = jnp.dot(xb, wu_ref[0, 0], preferred_element_type=f32)
        a = (_silu(g) * u).astype(bf16)
        acc_ref[...] += jnp.dot(a, wd_ref[0], preferred_element_type=f32)

    @pl.when(f == pl.num_programs(1) - 1)
    def _():
        _store_token_tiles(y_ref, jnp.where(valid, acc_ref[...], 0.0))


def _moe_gemm(tile_expert, tile_valid, tok, x_tiles, wg, wu, wd, rows):
    n_tiles = tile_expert.shape[0]
    _, n_chunks, d, chunk = wg.shape
    sub = d // LANES
    grid_spec = pltpu.PrefetchScalarGridSpec(
        num_scalar_prefetch=3,
        grid=(n_tiles, n_chunks),
        in_specs=[
            pl.BlockSpec(memory_space=pl.ANY),
            pl.BlockSpec((1, 1, d, chunk), lambda i, f, te, tv, tk: (te[i], f, 0, 0)),
            pl.BlockSpec((1, 1, d, chunk), lambda i, f, te, tv, tk: (te[i], f, 0, 0)),
            pl.BlockSpec((1, chunk, d), lambda i, f, te, tv, tk: (te[i], f, 0)),
        ],
        out_specs=pl.BlockSpec((rows * sub, LANES), lambda i, f, te, tv, tk: (i, 0)),
        scratch_shapes=[
            pltpu.VMEM((2, rows * sub, LANES), f32),
            pltpu.VMEM((rows, d), bf16),
            pltpu.VMEM((rows, d), f32),
            pltpu.SemaphoreType.DMA((2,)),
        ],
    )
    return pl.pallas_call(
        _moe_gemm_kernel,
        grid_spec=grid_spec,
        out_shape=jax.ShapeDtypeStruct((n_tiles * rows * sub, LANES), f32),
        compiler_params=_params("arbitrary", "arbitrary", disable_bounds_checks=True),
        name="moe_gemm",
    )(tile_expert, tile_valid, tok, x_tiles, wg, wu, wd)


def _combine_kernel(pos_ref, h_ref, gate_ref, g_ref, y_hbm, o_ref, ybuf_ref, sem):
    i = pl.program_id(0)
    rows, d = h_ref.shape
    sub = d // LANES
    slot = i % 2

    def gather_rows(step, buf):
        base = step * (TOP_K * rows)

        def trip(c, carry):
            for u in range(DMA_UNROLL):
                r = c * DMA_UNROLL + u
                for k in range(TOP_K):
                    src = pl.multiple_of(pos_ref[base + k * rows + r] * sub, sub)
                    pltpu.make_async_copy(y_hbm.at[pl.ds(src, sub), :],
                                          ybuf_ref.at[buf, k, pl.ds(pl.multiple_of(r * sub, sub), sub), :],
                                          sem.at[buf]).start()
            return carry
        lax.fori_loop(0, rows // DMA_UNROLL, trip, 0)

    @pl.when(i == 0)
    def _():
        gather_rows(0, 0)

    @pl.when(i + 1 < pl.num_programs(0))
    def _():
        gather_rows(i + 1, 1 - slot)

    for k in range(TOP_K):
        pltpu.make_async_copy(y_hbm.at[pl.ds(0, rows * sub), :], ybuf_ref.at[slot, k], sem.at[slot]).wait()
    gate = gate_ref[...]
    chunks = []
    sumsq = jnp.zeros((rows, 1), f32)
    for c in range(sub):
        hc = h_ref[:, c * LANES:(c + 1) * LANES]
        for k in range(TOP_K):
            hc = hc + gate[:, k:k + 1] * _load_token_tiles(ybuf_ref.at[slot, k], rows, sub, c)
        sumsq = sumsq + jnp.sum(hc * hc, axis=-1, keepdims=True)
        chunks.append(hc)
    scale = lax.rsqrt(sumsq / d + EPS)
    for c in range(sub):
        o_ref[:, c * LANES:(c + 1) * LANES] = chunks[c] * scale * g_ref[:, c * LANES:(c + 1) * LANES]


def _moe_combine(pos, h, gates, g_final, y_tiles, rows):
    t, d = h.shape
    sub = d // LANES
    grid_spec = pltpu.PrefetchScalarGridSpec(
        num_scalar_prefetch=1,
        grid=(t // rows,),
        in_specs=[
            pl.BlockSpec((rows, d), lambda i, ps: (i, 0)),
            pl.BlockSpec((rows, TOP_K), lambda i, ps: (i, 0)),
            pl.BlockSpec((1, d), lambda i, ps: (0, 0)),
            pl.BlockSpec(memory_space=pl.ANY),
        ],
        out_specs=pl.BlockSpec((rows, d), lambda i, ps: (i, 0)),
        scratch_shapes=[pltpu.VMEM((2, TOP_K, rows * sub, LANES), f32), pltpu.SemaphoreType.DMA((2,))],
    )
    return pl.pallas_call(
        _combine_kernel,
        grid_spec=grid_spec,
        out_shape=jax.ShapeDtypeStruct((t, d), f32),
        compiler_params=_params("arbitrary", disable_bounds_checks=True),
        name="moe_combine",
    )(pos, h, gates, g_final, y_tiles)


def _routing_plan(idx, n_exp, rows, comb_rows):
    t = idx.shape[0]
    pairs = t * TOP_K
    n_tiles = -(-pairs // rows) + n_exp
    e_flat = idx.T.reshape(pairs)
    onehot = (e_flat[:, None] == jnp.arange(n_exp, dtype=jnp.int32)[None, :]).astype(jnp.int32)
    csum = jnp.cumsum(onehot, axis=0)
    counts = csum[-1]
    rank = jnp.sum(onehot * csum, axis=1) - 1
    padded = ((counts + rows - 1) // rows) * rows
    pend = jnp.cumsum(padded)
    poff = pend - padded
    pos = poff[e_flat] + rank
    tok = jnp.zeros((n_tiles * rows,), jnp.int32).at[pos].set(
        jnp.arange(pairs, dtype=jnp.int32) % t, unique_indices=True)
    tile_start = jnp.arange(n_tiles, dtype=jnp.int32) * rows
    tile_expert = jnp.minimum(jnp.sum((tile_start[:, None] >= pend[None, :]).astype(jnp.int32), axis=1),
                              n_exp - 1)
    tile_valid = (tile_start < pend[-1]).astype(jnp.int32)
    last_valid = jnp.maximum(jnp.sum(tile_valid) - 1, 0)
    tile_expert = jnp.where(tile_valid > 0, tile_expert, tile_expert[last_valid])
    n_steps = t // comb_rows
    pos_steps = pos.reshape(TOP_K, n_steps, comb_rows).transpose(1, 0, 2).reshape(pairs)
    return tile_expert, tile_valid, tok, pos_steps


def _rope_tables(s):
    half = ROT_DIM // 2
    inv = ROPE_THETA ** (-jnp.arange(half, dtype=f32) * 2.0 / ROT_DIM)
    ang = jnp.arange(s).astype(f32)[:, None] * inv[None, :]
    cos, sin = jnp.cos(ang), jnp.sin(ang)
    pad = HEAD_DIM - ROT_DIM
    cos_t = jnp.concatenate([cos, cos, jnp.ones((s, pad), f32)], axis=-1)
    sin_t = jnp.concatenate([-sin, sin, jnp.zeros((s, pad), f32)], axis=-1)
    return cos_t, sin_t


def kernel(x, attn_norm, ffn_norm, pool_w, pool_scale, wq, wo, kv_norm, wk, wv, ffn_w_gate, ffn_w_up,
           ffn_w_down, router_w, moe_w_gate, moe_w_up, moe_w_down, final_norm):
    b, s, d = x.shape
    t = b * s
    n_exp = router_w.shape[-1]
    assert attn_norm.shape[0] == 2 and pool_w.shape[0] == 1 and wq.shape[0] == 1 and router_w.shape[0] == 1
    assert s % 512 == 0 and d % (HEAD_DIM * 2) == 0

    tok_rows = 512
    moe_chunk = 512
    moe_rows = 1024
    comb_rows = 256
    assert moe_w_gate.shape[-1] % moe_chunk == 0 and d % (8 * LANES) == 0

    row = lambda v: v.reshape(1, -1)

    h1, hn1 = _pool_mixer(x, row(attn_norm[0]), pool_w[0].astype(bf16), row(pool_scale[0]),
                          row(ffn_norm[0]), tok_rows)
    h2 = _dense_ffn(h1.reshape(t, d), hn1.reshape(t, d), ffn_w_gate[0].astype(bf16),
                    ffn_w_up[0].astype(bf16), ffn_w_down[0].astype(bf16), tok_rows, 512)

    cos_t, sin_t = _rope_tables(s)
    qt, k, vt, kmean = _qkv_proj(h2.reshape(b, s, d), row(attn_norm[1]), row(kv_norm), wq[0].astype(bf16),
                                 wk.astype(bf16), wv.astype(bf16), cos_t, sin_t, tok_rows)
    o = _moba_attention(qt, k, vt, kmean)

    h3, hn3, idx, gates = _out_router(o.reshape(t, d), h2, wo[0].astype(bf16), row(ffn_norm[1]),
                                      router_w[0], tok_rows)

    tile_expert, tile_valid, tok, pos = _routing_plan(idx, n_exp, moe_rows, comb_rows)
    def chunk_major(w):
        e, _, fe = w.shape
        return w.astype(bf16).reshape(e, d, fe // moe_chunk, moe_chunk).transpose(0, 2, 1, 3)

    y_tiles = _moe_gemm(tile_expert, tile_valid, tok, hn3, chunk_major(moe_w_gate[0]),
                        chunk_major(moe_w_up[0]), moe_w_down[0].astype(bf16), moe_rows)
    out = _moe_combine(pos, h3, gates, row(final_norm), y_tiles, comb_rows)
    return out.reshape(b, s, d)
```

```python
import functools

import jax
import jax.numpy as jnp
from jax import lax
from jax.experimental import pallas as pl
from jax.experimental.pallas import tpu as pltpu

EPS = 1e-6
POOL_WINDOWS = (2, 4, 8, 16)
POOL_HALO = 16
LANES = 128
HEAD_DIM = 128
BLOCK = 256
TOP_BLOCKS = 3
ROT_DIM = HEAD_DIM // 4
ROPE_THETA = 500000.0
TOP_K = 2

VMEM_LIMIT = 56 * 1024 * 1024
DMA_UNROLL = 8
NEG_INF = float("-inf")
LOG2_E = 1.4426950408889634

f32 = jnp.float32
bf16 = jnp.bfloat16


def _resident(shape):
    n = len(shape)
    return pl.BlockSpec(shape, lambda *_: (0,) * n, pipeline_mode=pl.Buffered(1))


def _rms_scale(xf):
    return lax.rsqrt(jnp.mean(xf * xf, axis=-1, keepdims=True) + EPS)


def _silu(g):
    return g * (1.0 / (1.0 + jnp.exp(-g)))


def _store_token_tiles(ref, x):
    rows, d = x.shape
    sub = d // LANES
    for c in range(sub):
        ref[pl.ds(c, rows, stride=sub), :] = x[:, c * LANES:(c + 1) * LANES]


def _load_token_tiles(ref, rows, sub, chunk):
    return ref[pl.ds(chunk, rows, stride=sub), :]


def _params(*sem, **kw):
    return pltpu.CompilerParams(dimension_semantics=sem, vmem_limit_bytes=VMEM_LIMIT, **kw)


def _pool_kernel(x_ref, halo_ref, g_attn_ref, w_ref, scale_ref, g_ffn_ref, h_ref, hn_ref, ext_ref):
    i = pl.program_id(1)
    rows = x_ref.shape[1]
    d = x_ref.shape[2]
    c = d // len(POOL_WINDOWS)
    g_attn = g_attn_ref[...]
    x = x_ref[0]
    xn = x * _rms_scale(x) * g_attn
    halo = halo_ref[0]
    halo = halo * _rms_scale(halo) * g_attn
    ext_ref[0:POOL_HALO, :] = jnp.where(i == 0, 0.0, halo)
    ext_ref[POOL_HALO:, :] = xn
    t = i * rows + lax.broadcasted_iota(jnp.int32, (rows, 1), 0)
    ys = []
    for g, w in enumerate(POOL_WINDOWS):
        c0 = g * c
        s = ext_ref[pl.ds(POOL_HALO, rows), c0:c0 + c]
        for k in range(1, w):
            s = s + ext_ref[pl.ds(POOL_HALO - k, rows), c0:c0 + c]
        cnt = jnp.minimum(t + 1, w).astype(f32)
        pooled = s / cnt - xn[:, c0:c0 + c]
        ys.append(jnp.dot(pooled.astype(bf16), w_ref[g], preferred_element_type=f32))
    y = jnp.concatenate(ys, axis=-1) * scale_ref[...]
    h = x + y
    h_ref[0] = h
    hn_ref[0] = (h * _rms_scale(h) * g_ffn_ref[...]).astype(bf16)


def _pool_mixer(x, g_attn, w_pool, scale, g_ffn, rows):
    b, s, d = x.shape
    c = d // len(POOL_WINDOWS)
    hb = rows // POOL_HALO
    return pl.pallas_call(
        _pool_kernel,
        grid=(b, s // rows),
        in_specs=[
            pl.BlockSpec((1, rows, d), lambda bi, i: (bi, i, 0)),
            pl.BlockSpec((1, POOL_HALO, d), lambda bi, i: (bi, jnp.maximum(i * hb - 1, 0), 0)),
            _resident((1, d)),
            _resident((len(POOL_WINDOWS), c, c)),
            _resident((1, d)),
            _resident((1, d)),
        ],
        out_specs=[
            pl.BlockSpec((1, rows, d), lambda bi, i: (bi, i, 0)),
            pl.BlockSpec((1, rows, d), lambda bi, i: (bi, i, 0)),
        ],
        out_shape=[jax.ShapeDtypeStruct((b, s, d), f32), jax.ShapeDtypeStruct((b, s, d), bf16)],
        scratch_shapes=[pltpu.VMEM((rows + POOL_HALO, d), f32)],
        compiler_params=_params("arbitrary", "arbitrary"),
        name="pool_mixer",
    )(x, x, g_attn, w_pool, scale, g_ffn)


def _ffn_chunks(f, size):
    out, c0 = [], 0
    while c0 < f:
        out.append((c0, min(size, f - c0)))
        c0 += size
    return out


def _dense_ffn_kernel(h_ref, hn_ref, wg_ref, wu_ref, wd_ref, o_ref, *, chunk):
    hn = hn_ref[...]
    acc = h_ref[...]
    for c0, cw in _ffn_chunks(wg_ref.shape[1], chunk):
        g = jnp.dot(hn, wg_ref[:, c0:c0 + cw], preferred_element_type=f32)
        u = jnp.dot(hn, wu_ref[:, c0:c0 + cw], preferred_element_type=f32)
        a = (_silu(g) * u).astype(bf16)
        acc = acc + jnp.dot(a, wd_ref[c0:c0 + cw, :], preferred_element_type=f32)
    o_ref[...] = acc


def _dense_ffn(h, hn, wg, wu, wd, rows, chunk):
    t, d = h.shape
    f = wg.shape[1]
    return pl.pallas_call(
        functools.partial(_dense_ffn_kernel, chunk=chunk),
        grid=(t // rows,),
        in_specs=[
            pl.BlockSpec((rows, d), lambda i: (i, 0)),
            pl.BlockSpec((rows, d), lambda i: (i, 0)),
            _resident((d, f)),
            _resident((d, f)),
            _resident((f, d)),
        ],
        out_specs=pl.BlockSpec((rows, d), lambda i: (i, 0)),
        out_shape=jax.ShapeDtypeStruct((t, d), f32),
        compiler_params=_params("arbitrary"),
        name="dense_ffn",
    )(h, hn, wg, wu, wd)


def _rope(x, cos, sin_signed, lane):
    d = x.shape[-1]
    half = ROT_DIM // 2
    partner = jnp.where(lane < half, pltpu.roll(x, d - half, 1), pltpu.roll(x, half, 1))
    return x * cos + partner * sin_signed


def _qkv_kernel(h_ref, g_q_ref, g_kv_ref, wq_ref, wk_ref, wv_ref, cos_ref, sin_ref,
                qt_ref, k_ref, vt_ref, kmean_ref):
    i = pl.program_id(1)
    rows, d = h_ref.shape[1], h_ref.shape[2]
    heads = d // HEAD_DIM
    h = h_ref[0]
    xhat = h * _rms_scale(h)
    hq = (xhat * g_q_ref[...]).astype(bf16)
    hkv = (xhat * g_kv_ref[...]).astype(bf16)
    q = jnp.dot(hq, wq_ref[...], preferred_element_type=f32)
    k = jnp.dot(hkv, wk_ref[...], preferred_element_type=f32)
    v = jnp.dot(hkv, wv_ref[...], preferred_element_type=f32)
    cos = jnp.concatenate([cos_ref[...]] * heads, axis=-1)
    sin = jnp.concatenate([sin_ref[...]] * heads, axis=-1)
    lane = lax.broadcasted_iota(jnp.int32, (rows, d), 1) % HEAD_DIM
    q = _rope(q, cos, sin, lane) * (HEAD_DIM ** -0.5 * LOG2_E)
    k = _rope(k, cos, sin, lane)
    qt_ref[0] = q.T.astype(bf16)
    k_ref[0] = k.astype(bf16)
    vt_ref[0] = v.T.astype(bf16)

    nb = kmean_ref.shape[1]
    blk_iota = lax.broadcasted_iota(jnp.int32, (nb, d), 0)

    @pl.when(i == 0)
    def _():
        kmean_ref[0] = jnp.zeros((nb, d), f32)

    km = kmean_ref[0]
    for j in range(rows // BLOCK):
        mean_j = jnp.mean(k[j * BLOCK:(j + 1) * BLOCK, :], axis=0, keepdims=True)
        km = jnp.where(blk_iota == i * (rows // BLOCK) + j, mean_j, km)
    kmean_ref[0] = km


def _qkv_proj(h, g_q, g_kv, wq, wk, wv, cos, sin, rows):
    b, s, d = h.shape
    nb = s // BLOCK
    return pl.pallas_call(
        _qkv_kernel,
        grid=(b, s // rows),
        in_specs=[
            pl.BlockSpec((1, rows, d), lambda bi, i: (bi, i, 0)),
            _resident((1, d)),
            _resident((1, d)),
            _resident((d, d)),
            _resident((d, d)),
            _resident((d, d)),
            pl.BlockSpec((rows, HEAD_DIM), lambda bi, i: (i, 0)),
            pl.BlockSpec((rows, HEAD_DIM), lambda bi, i: (i, 0)),
        ],
        out_specs=[
            pl.BlockSpec((1, d, rows), lambda bi, i: (bi, 0, i)),
            pl.BlockSpec((1, rows, d), lambda bi, i: (bi, i, 0)),
            pl.BlockSpec((1, d, rows), lambda bi, i: (bi, 0, i)),
            pl.BlockSpec((1, nb, d), lambda bi, i: (bi, 0, 0)),
        ],
        out_shape=[
            jax.ShapeDtypeStruct((b, d, s), bf16),
            jax.ShapeDtypeStruct((b, s, d), bf16),
            jax.ShapeDtypeStruct((b, d, s), bf16),
            jax.ShapeDtypeStruct((b, nb, d), f32),
        ],
        compiler_params=_params("arbitrary", "arbitrary"),
        name="qkv_proj",
    )(h, g_q, g_kv, wq, wk, wv, cos, sin)


def _moba_kernel(qt_ref, k_ref, vt_ref, kmean_ref, o_ref, *scratch, heads):
    sel_refs, m_refs, l_refs, acc_refs = (scratch[i * heads:(i + 1) * heads] for i in range(4))
    j = pl.program_id(1)
    nb = kmean_ref.shape[1]
    hd = HEAD_DIM
    blk = lax.broadcasted_iota(jnp.int32, (nb, BLOCK), 0)
    past = blk < j
    kpos = lax.broadcasted_iota(jnp.int32, (BLOCK, BLOCK), 0)
    qpos = lax.broadcasted_iota(jnp.int32, (BLOCK, BLOCK), 1)
    causal = kpos <= qpos
    start = pl.multiple_of(j * BLOCK, BLOCK)

    def q_head(h):
        return qt_ref[0, h * hd:(h + 1) * hd, :]

    def scores(off):
        return [jnp.dot(k_ref[0, pl.ds(off, BLOCK), h * hd:(h + 1) * hd], q_head(h),
                        preferred_element_type=f32) for h in range(heads)]

    def weighted_values(ps, off):
        return [jnp.dot(vt_ref[0, h * hd:(h + 1) * hd, pl.ds(off, BLOCK)], ps[h],
                        preferred_element_type=f32) for h in range(heads)]

    for h in range(heads):
        gate = jnp.dot(kmean_ref[0, :, h * hd:(h + 1) * hd], q_head(h).astype(f32),
                       precision=lax.Precision.HIGHEST, preferred_element_type=f32)
        gate = jnp.where(past, gate, NEG_INF)
        rank = jnp.zeros((nb, BLOCK), jnp.int32)
        for m in range(nb):
            gm = gate[m:m + 1, :]
            rank = rank + jnp.where(gm > gate, 1, jnp.where(gm == gate, jnp.where(blk > m, 1, 0), 0))
        sel_refs[h][...] = jnp.where(past, jnp.where(rank < TOP_BLOCKS, 1.0, 0.0), 0.0)

    ps = []
    for h, s in enumerate(scores(start)):
        s = jnp.where(causal, s, NEG_INF)
        m0 = jnp.max(s, axis=0, keepdims=True)
        p = jnp.exp2(s - m0)
        m_refs[h][...] = m0
        l_refs[h][...] = jnp.sum(p, axis=0, keepdims=True)
        ps.append(p.astype(bf16))
    for h, pv in enumerate(weighted_values(ps, start)):
        acc_refs[h][...] = pv

    def body(n, carry):
        off = pl.multiple_of(n * BLOCK, BLOCK)
        ps, alphas = [], []
        for h, s in enumerate(scores(off)):
            s = jnp.where(sel_refs[h][pl.ds(n, 1), :] > 0.0, s, NEG_INF)
            m_old = m_refs[h][...]
            m_new = jnp.maximum(m_old, jnp.max(s, axis=0, keepdims=True))
            alpha = jnp.exp2(m_old - m_new)
            p = jnp.exp2(s - m_new)
            m_refs[h][...] = m_new
            l_refs[h][...] = alpha * l_refs[h][...] + jnp.sum(p, axis=0, keepdims=True)
            ps.append(p.astype(bf16))
            alphas.append(alpha)
        for h, pv in enumerate(weighted_values(ps, off)):
            acc_refs[h][...] = alphas[h] * acc_refs[h][...] + pv
        return carry

    lax.fori_loop(0, j, body, 0)
    for h in range(heads):
        o_ref[0, :, h * hd:(h + 1) * hd] = (acc_refs[h][...] / l_refs[h][...]).T.astype(o_ref.dtype)


def _moba_attention(qt, k, vt, kmean):
    b, d, s = qt.shape
    heads = d // HEAD_DIM
    nb = s // BLOCK
    return pl.pallas_call(
        functools.partial(_moba_kernel, heads=heads),
        grid=(b, nb),
        in_specs=[
            pl.BlockSpec((1, d, BLOCK), lambda bi, j: (bi, 0, j)),
            pl.BlockSpec((1, s, d), lambda bi, j: (bi, 0, 0), pipeline_mode=pl.Buffered(1)),
            pl.BlockSpec((1, d, s), lambda bi, j: (bi, 0, 0), pipeline_mode=pl.Buffered(1)),
            pl.BlockSpec((1, nb, d), lambda bi, j: (bi, 0, 0)),
        ],
        out_specs=pl.BlockSpec((1, BLOCK, d), lambda bi, j: (bi, j, 0)),
        out_shape=jax.ShapeDtypeStruct((b, s, d), bf16),
        scratch_shapes=(
            [pltpu.VMEM((nb, BLOCK), f32)] * heads
            + [pltpu.VMEM((1, BLOCK), f32)] * (2 * heads)
            + [pltpu.VMEM((HEAD_DIM, BLOCK), f32)] * heads
        ),
        compiler_params=_params("arbitrary", "arbitrary"),
        name="moba_attn",
    )(qt, k, vt, kmean)


def _out_router_kernel(o_ref, h_ref, wo_ref, g_ref, wr_ref, h_out_ref, idx_ref, gate_ref):
    rows = o_ref.shape[0]
    n_exp = wr_ref.shape[1]
    h = h_ref[...] + jnp.dot(o_ref[...], wo_ref[...], preferred_element_type=f32)
    h_out_ref[...] = h
    hn = h * _rms_scale(h) * g_ref[...]
    wr = wr_ref[...]
    w_hi = wr.astype(bf16)
    w_lo = (wr - w_hi.astype(f32)).astype(bf16)
    x_hi = hn.astype(bf16)
    x_lo = (hn - x_hi.astype(f32)).astype(bf16)
    both = jnp.dot(x_hi, jnp.concatenate([w_hi, w_lo], axis=1), preferred_element_type=f32)
    logits = both[:, :n_exp] + both[:, n_exp:] + jnp.dot(x_lo, w_hi, preferred_element_type=f32)
    e = lax.broadcasted_iota(jnp.int32, (rows, n_exp), 1)
    v1 = jnp.max(logits, axis=-1, keepdims=True)
    i1 = jnp.min(jnp.where(logits == v1, e, n_exp), axis=-1, keepdims=True)
    rest = jnp.where(e == i1, NEG_INF, logits)
    v2 = jnp.max(rest, axis=-1, keepdims=True)
    i2 = jnp.min(jnp.where(rest == v2, e, n_exp), axis=-1, keepdims=True)
    e2 = jnp.exp(v2 - v1)
    denom = 1.0 + e2
    col = lax.broadcasted_iota(jnp.int32, (rows, TOP_K), 1)
    idx_ref[...] = jnp.where(col == 0, i1, i2)
    gate_ref[...] = jnp.where(col == 0, 1.0 / denom, e2 / denom)


def _out_router(o, h, wo, g_ffn, w_router, rows):
    t, d = h.shape
    n_exp = w_router.shape[1]
    return pl.pallas_call(
        _out_router_kernel,
        grid=(t // rows,),
        in_specs=[
            pl.BlockSpec((rows, d), lambda i: (i, 0)),
            pl.BlockSpec((rows, d), lambda i: (i, 0)),
            _resident((d, d)),
            _resident((1, d)),
            _resident((d, n_exp)),
        ],
        out_specs=[
            pl.BlockSpec((rows, d), lambda i: (i, 0)),
            pl.BlockSpec((rows, TOP_K), lambda i: (i, 0)),
            pl.BlockSpec((rows, TOP_K), lambda i: (i, 0)),
        ],
        out_shape=[
            jax.ShapeDtypeStruct((t, d), f32),
            jax.ShapeDtypeStruct((t, TOP_K), jnp.int32),
            jax.ShapeDtypeStruct((t, TOP_K), f32),
        ],
        compiler_params=_params("arbitrary"),
        name="out_router",
    )(o, h, wo, g_ffn, w_router)


def _dispatch_kernel(pos_ref, pad_lo_ref, pad_hi_ref, h_ref, g_ref, x_hbm, buf_ref, zero_ref, sem, zsem):
    i = pl.program_id(0)
    n_steps = pl.num_programs(0)
    rows, d = h_ref.shape
    sub = d // LANES
    slot = i % 2

    def staged_copies_wait(buf):
        for _ in range(TOP_K):
            pltpu.make_async_copy(buf_ref.at[buf], x_hbm.at[pl.ds(0, rows * sub), :], sem.at[buf]).wait()

    @pl.when(i == 0)
    def _():
        zero_ref[...] = jnp.zeros(zero_ref.shape, f32)
        for e in range(pad_lo_ref.shape[0]):
            lo, hi = pad_lo_ref[e], pad_hi_ref[e]

            def fill(r, carry):
                pltpu.make_async_copy(zero_ref, x_hbm.at[pl.ds(pl.multiple_of(r * sub, sub), sub), :],
                                      zsem).start()
                return carry

            def drain(r, carry):
                pltpu.make_async_copy(zero_ref, x_hbm.at[pl.ds(0, sub), :], zsem).wait()
                return carry
            lax.fori_loop(lo, hi, fill, 0)
            lax.fori_loop(lo, hi, drain, 0)

        buf_ref[1] = jnp.zeros(buf_ref.shape[1:], f32)
        tail_lo = pad_hi_ref[pad_hi_ref.shape[0] - 1]
        n_tail = (x_hbm.shape[0] // sub - tail_lo) // rows

        def tail_copy(c):
            dst = pl.multiple_of((tail_lo + c * rows) * sub, sub)
            return pltpu.make_async_copy(buf_ref.at[1], x_hbm.at[pl.ds(dst, rows * sub), :], zsem)

        def tail_fill(c, carry):
            tail_copy(c).start()
            return carry

        def tail_drain(c, carry):
            tail_copy(c).wait()
            return carry
        lax.fori_loop(0, n_tail, tail_fill, 0)
        lax.fori_loop(0, n_tail, tail_drain, 0)

    @pl.when(i >= 2)
    def _():
        staged_copies_wait(slot)

    h = h_ref[...]
    _store_token_tiles(buf_ref.at[slot], h * _rms_scale(h) * g_ref[...])
    base = i * (TOP_K * rows)

    def trip(c, carry):
        for u in range(DMA_UNROLL):
            r = c * DMA_UNROLL + u
            src = buf_ref.at[slot, pl.ds(pl.multiple_of(r * sub, sub), sub), :]
            for k in range(TOP_K):
                dst = pl.multiple_of(pos_ref[base + k * rows + r] * sub, sub)
                pltpu.make_async_copy(src, x_hbm.at[pl.ds(dst, sub), :], sem.at[slot]).start()
        return carry
    lax.fori_loop(0, rows // DMA_UNROLL, trip, 0)

    @pl.when(i == n_steps - 1)
    def _():
        staged_copies_wait(slot)

        @pl.when(n_steps >= 2)
        def _():
            staged_copies_wait(1 - slot)


def _moe_dispatch(pos_steps, pad_lo, pad_hi, h, g_ffn, n_rows, rows):
    t, d = h.shape
    sub = d // LANES
    grid_spec = pltpu.PrefetchScalarGridSpec(
        num_scalar_prefetch=3,
        grid=(t // rows,),
        in_specs=[
            pl.BlockSpec((rows, d), lambda i, ps, lo, hi: (i, 0)),
            pl.BlockSpec((1, d), lambda i, ps, lo, hi: (0, 0)),
        ],
        out_specs=pl.BlockSpec(memory_space=pl.ANY),
        scratch_shapes=[
            pltpu.VMEM((2, rows * sub, LANES), f32),
            pltpu.VMEM((sub, LANES), f32),
            pltpu.SemaphoreType.DMA((2,)),
            pltpu.SemaphoreType.DMA,
        ],
    )
    return pl.pallas_call(
        _dispatch_kernel,
        grid_spec=grid_spec,
        out_shape=jax.ShapeDtypeStruct((n_rows * sub, LANES), f32),
        compiler_params=_params("arbitrary", disable_bounds_checks=True),
        name="moe_dispatch",
    )(pos_steps, pad_lo, pad_hi, h, g_ffn)


def _moe_gemm_kernel(tile_expert_ref, tile_valid_ref, tile_src_ref, x_ref, wg_ref, wu_ref, wd_ref,
                     y_ref, xb_ref, acc_ref):
    i = pl.program_id(0)
    f = pl.program_id(1)
    rows, d = xb_ref.shape
    sub = d // LANES
    valid = tile_valid_ref[i] > 0

    @pl.when(jnp.logical_and(f == 0, valid))
    def _():
        for c in range(sub):
            xb_ref[:, c * LANES:(c + 1) * LANES] = _load_token_tiles(x_ref, rows, sub, c).astype(bf16)
        acc_ref[...] = jnp.zeros(acc_ref.shape, f32)

    @pl.when(valid)
    def _():
        xb = xb_ref[...]
        g = jnp.dot(xb, wg_ref[0].astype(bf16), preferred_element_type=f32)
        u = jnp.dot(xb, wu_ref[0].astype(bf16), preferred_element_type=f32)
        a = (_silu(g) * u).astype(bf16)
        acc_ref[...] += jnp.dot(a, wd_ref[0].astype(bf16), preferred_element_type=f32)

    @pl.when(f == pl.num_programs(1) - 1)
    def _():
        _store_token_tiles(y_ref, jnp.where(valid, acc_ref[...], 0.0))


def _moe_gemm(tile_expert, tile_valid, tile_src, x_tiles, wg, wu, wd, rows, chunk):
    n_tiles = tile_expert.shape[0]
    _, d, fe = wg.shape
    sub = d // LANES
    grid_spec = pltpu.PrefetchScalarGridSpec(
        num_scalar_prefetch=3,
        grid=(n_tiles, fe // chunk),
        in_specs=[
            pl.BlockSpec((rows * sub, LANES), lambda i, f, te, tv, ts: (ts[i], 0)),
            pl.BlockSpec((1, d, chunk), lambda i, f, te, tv, ts: (te[i], 0, f)),
            pl.BlockSpec((1, d, chunk), lambda i, f, te, tv, ts: (te[i], 0, f)),
            pl.BlockSpec((1, chunk, d), lambda i, f, te, tv, ts: (te[i], f, 0)),
        ],
        out_specs=pl.BlockSpec((rows * sub, LANES), lambda i, f, te, tv, ts: (i, 0)),
        scratch_shapes=[
            pltpu.VMEM((rows, d), bf16),
            pltpu.VMEM((rows, d), f32),
        ],
    )
    return pl.pallas_call(
        _moe_gemm_kernel,
        grid_spec=grid_spec,
        out_shape=jax.ShapeDtypeStruct((n_tiles * rows * sub, LANES), f32),
        compiler_params=_params("arbitrary", "arbitrary"),
        name="moe_gemm",
    )(tile_expert, tile_valid, tile_src, x_tiles, wg, wu, wd)


def _combine_kernel(pos_ref, h_ref, gate_ref, g_ref, y_hbm, o_ref, ybuf_ref, sem):
    i = pl.program_id(0)
    rows, d = h_ref.shape
    sub = d // LANES
    slot = i % 2

    def gather_rows(step, buf):
        base = step * (TOP_K * rows)

        def trip(c, carry):
            for u in range(DMA_UNROLL):
                r = c * DMA_UNROLL + u
                for k in range(TOP_K):
                    src = pl.multiple_of(pos_ref[base + k * rows + r] * sub, sub)
                    pltpu.make_async_copy(y_hbm.at[pl.ds(src, sub), :],
                                          ybuf_ref.at[buf, k, pl.ds(pl.multiple_of(r * sub, sub), sub), :],
                                          sem.at[buf]).start()
            return carry
        lax.fori_loop(0, rows // DMA_UNROLL, trip, 0)

    @pl.when(i == 0)
    def _():
        gather_rows(0, 0)

    @pl.when(i + 1 < pl.num_programs(0))
    def _():
        gather_rows(i + 1, 1 - slot)

    for k in range(TOP_K):
        pltpu.make_async_copy(y_hbm.at[pl.ds(0, rows * sub), :], ybuf_ref.at[slot, k], sem.at[slot]).wait()
    gate = gate_ref[...]
    chunks = []
    sumsq = jnp.zeros((rows, 1), f32)
    for c in range(sub):
        hc = h_ref[:, c * LANES:(c + 1) * LANES]
        for k in range(TOP_K):
            hc = hc + gate[:, k:k + 1] * _load_token_tiles(ybuf_ref.at[slot, k], rows, sub, c)
        sumsq = sumsq + jnp.sum(hc * hc, axis=-1, keepdims=True)
        chunks.append(hc)
    scale = lax.rsqrt(sumsq / d + EPS)
    for c in range(sub):
        o_ref[:, c * LANES:(c + 1) * LANES] = chunks[c] * scale * g_ref[:, c * LANES:(c + 1) * LANES]


def _moe_combine(pos, h, gates, g_final, y_tiles, rows):
    t, d = h.shape
    sub = d // LANES
    grid_spec = pltpu.PrefetchScalarGridSpec(
        num_scalar_prefetch=1,
        grid=(t // rows,),
        in_specs=[
            pl.BlockSpec((rows, d), lambda i, ps: (i, 0)),
            pl.BlockSpec((rows, TOP_K), lambda i, ps: (i, 0)),
            pl.BlockSpec((1, d), lambda i, ps: (0, 0)),
            pl.BlockSpec(memory_space=pl.ANY),
        ],
        out_specs=pl.BlockSpec((rows, d), lambda i, ps: (i, 0)),
        scratch_shapes=[pltpu.VMEM((2, TOP_K, rows * sub, LANES), f32), pltpu.SemaphoreType.DMA((2,))],
    )
    return pl.pallas_call(
        _combine_kernel,
        grid_spec=grid_spec,
        out_shape=jax.ShapeDtypeStruct((t, d), f32),
        compiler_params=_params("arbitrary", disable_bounds_checks=True),
        name="moe_combine",
    )(pos, h, gates, g_final, y_tiles)


def _routing_plan(idx, n_exp, rows, disp_rows, comb_rows):
    t = idx.shape[0]
    pairs = t * TOP_K
    n_tiles = -(-pairs // rows) + n_exp
    e_flat = idx.T.reshape(pairs)
    onehot = (e_flat[:, None] == jnp.arange(n_exp, dtype=jnp.int32)[None, :]).astype(jnp.int32)
    csum = jnp.cumsum(onehot, axis=0)
    counts = csum[-1]
    rank = jnp.sum(onehot * csum, axis=1) - 1
    padded = ((counts + rows - 1) // rows) * rows
    pend = jnp.cumsum(padded)
    poff = pend - padded
    pos = poff[e_flat] + rank
    tile_start = jnp.arange(n_tiles, dtype=jnp.int32) * rows
    tile_expert = jnp.minimum(jnp.sum((tile_start[:, None] >= pend[None, :]).astype(jnp.int32), axis=1),
                              n_exp - 1)
    tile_valid = (tile_start < pend[-1]).astype(jnp.int32)
    last_valid = jnp.maximum(jnp.sum(tile_valid) - 1, 0)
    tile_expert = jnp.where(tile_valid > 0, tile_expert, tile_expert[last_valid])
    tile_src = jnp.minimum(jnp.arange(n_tiles, dtype=jnp.int32), last_valid)

    def by_step(step_rows):
        return pos.reshape(TOP_K, t // step_rows, step_rows).transpose(1, 0, 2).reshape(pairs)

    return (tile_expert, tile_valid, tile_src, poff + counts, pend, n_tiles * rows,
            by_step(disp_rows), by_step(comb_rows))


def _rope_tables(s):
    half = ROT_DIM // 2
    inv = ROPE_THETA ** (-jnp.arange(half, dtype=f32) * 2.0 / ROT_DIM)
    ang = jnp.arange(s).astype(f32)[:, None] * inv[None, :]
    cos, sin = jnp.cos(ang), jnp.sin(ang)
    pad = HEAD_DIM - ROT_DIM
    cos_t = jnp.concatenate([cos, cos, jnp.ones((s, pad), f32)], axis=-1)
    sin_t = jnp.concatenate([-sin, sin, jnp.zeros((s, pad), f32)], axis=-1)
    return cos_t, sin_t


def kernel(x, attn_norm, ffn_norm, pool_w, pool_scale, wq, wo, kv_norm, wk, wv, ffn_w_gate, ffn_w_up,
           ffn_w_down, router_w, moe_w_gate, moe_w_up, moe_w_down, final_norm):
    b, s, d = x.shape
    t = b * s
    n_exp = router_w.shape[-1]
    assert attn_norm.shape[0] == 2 and pool_w.shape[0] == 1 and wq.shape[0] == 1 and router_w.shape[0] == 1
    assert s % 512 == 0 and d % (HEAD_DIM * 2) == 0

    tok_rows = 512
    moe_chunk = 512
    moe_rows = 1024
    comb_rows = 256
    assert moe_w_gate.shape[-1] % moe_chunk == 0 and d % (8 * LANES) == 0 and moe_rows % tok_rows == 0

    row = lambda v: v.reshape(1, -1)

    h1, hn1 = _pool_mixer(x, row(attn_norm[0]), pool_w[0].astype(bf16), row(pool_scale[0]),
                          row(ffn_norm[0]), tok_rows)
    h2 = _dense_ffn(h1.reshape(t, d), hn1.reshape(t, d), ffn_w_gate[0].astype(bf16),
                    ffn_w_up[0].astype(bf16), ffn_w_down[0].astype(bf16), tok_rows, 512)

    cos_t, sin_t = _rope_tables(s)
    qt, k, vt, kmean = _qkv_proj(h2.reshape(b, s, d), row(attn_norm[1]), row(kv_norm), wq[0].astype(bf16),
                                 wk.astype(bf16), wv.astype(bf16), cos_t, sin_t, tok_rows)
    o = _moba_attention(qt, k, vt, kmean)

    h3, idx, gates = _out_router(o.reshape(t, d), h2, wo[0].astype(bf16), row(ffn_norm[1]), router_w[0],
                                 tok_rows)

    (tile_expert, tile_valid, tile_src, pad_lo, pad_hi, n_rows, pos_disp,
     pos_comb) = _routing_plan(idx, n_exp, moe_rows, tok_rows, comb_rows)
    x_tiles = _moe_dispatch(pos_disp, pad_lo, pad_hi, h3, row(ffn_norm[1]), n_rows, tok_rows)
    y_tiles = _moe_gemm(tile_expert, tile_valid, tile_src, x_tiles, moe_w_gate[0], moe_w_up[0],
                        moe_w_down[0], moe_rows, moe_chunk)
    out = _moe_combine(pos_comb, h3, gates, row(final_norm), y_tiles, comb_rows)
    return out.reshape(b, s, d)
```

```python
import functools

import jax
import jax.numpy as jnp
from jax import lax
from jax.experimental import pallas as pl
from jax.experimental.pallas import tpu as pltpu

EPS = 1e-6
POOL_WINDOWS = (2, 4, 8, 16)
POOL_HALO = 16
LANES = 128
HEAD_DIM = 128
BLOCK = 256
TOP_BLOCKS = 3
ROT_DIM = HEAD_DIM // 4
ROPE_THETA = 500000.0
TOP_K = 2

VMEM_LIMIT = 56 * 1024 * 1024
DMA_UNROLL = 8
NEG_INF = float("-inf")
POS_INF = float("inf")
MXU_LAG = 8
ONES_ROWS = 16
LOG2_E = 1.4426950408889634

f32 = jnp.float32
bf16 = jnp.bfloat16


def _resident(shape):
    n = len(shape)
    return pl.BlockSpec(shape, lambda *_: (0,) * n, pipeline_mode=pl.Buffered(1))


def _rms_scale(xf):
    return lax.rsqrt(jnp.mean(xf * xf, axis=-1, keepdims=True) + EPS)


def _silu(g):
    return g * (1.0 / (1.0 + jnp.exp(-g)))


def _store_token_tiles(ref, x):
    rows, d = x.shape
    sub = d // LANES
    for c in range(sub):
        ref[pl.ds(c, rows, stride=sub), :] = x[:, c * LANES:(c + 1) * LANES]


def _load_token_tiles(ref, rows, sub, chunk):
    return ref[pl.ds(chunk, rows, stride=sub), :]


def _params(*sem, **kw):
    return pltpu.CompilerParams(dimension_semantics=sem, vmem_limit_bytes=VMEM_LIMIT, **kw)


def _pool_kernel(x_ref, halo_ref, g_attn_ref, w_ref, scale_ref, g_ffn_ref, h_ref, hn_ref, ext_ref):
    i = pl.program_id(1)
    rows = x_ref.shape[1]
    d = x_ref.shape[2]
    c = d // len(POOL_WINDOWS)
    g_attn = g_attn_ref[...]
    x = x_ref[0]
    xn = x * _rms_scale(x) * g_attn
    halo = halo_ref[0]
    halo = halo * _rms_scale(halo) * g_attn
    ext_ref[0:POOL_HALO, :] = jnp.where(i == 0, 0.0, halo)
    ext_ref[POOL_HALO:, :] = xn
    t = i * rows + lax.broadcasted_iota(jnp.int32, (rows, 1), 0)
    ys = []
    for g, w in enumerate(POOL_WINDOWS):
        c0 = g * c
        s = ext_ref[pl.ds(POOL_HALO, rows), c0:c0 + c]
        for k in range(1, w):
            s = s + ext_ref[pl.ds(POOL_HALO - k, rows), c0:c0 + c]
        cnt = jnp.minimum(t + 1, w).astype(f32)
        pooled = s / cnt - xn[:, c0:c0 + c]
        ys.append(jnp.dot(pooled.astype(bf16), w_ref[g], preferred_element_type=f32))
    y = jnp.concatenate(ys, axis=-1) * scale_ref[...]
    h = x + y
    h_ref[0] = h
    hn_ref[0] = (h * _rms_scale(h) * g_ffn_ref[...]).astype(bf16)


def _pool_mixer(x, g_attn, w_pool, scale, g_ffn, rows):
    b, s, d = x.shape
    c = d // len(POOL_WINDOWS)
    hb = rows // POOL_HALO
    return pl.pallas_call(
        _pool_kernel,
        grid=(b, s // rows),
        in_specs=[
            pl.BlockSpec((1, rows, d), lambda bi, i: (bi, i, 0)),
            pl.BlockSpec((1, POOL_HALO, d), lambda bi, i: (bi, jnp.maximum(i * hb - 1, 0), 0)),
            _resident((1, d)),
            _resident((len(POOL_WINDOWS), c, c)),
            _resident((1, d)),
            _resident((1, d)),
        ],
        out_specs=[
            pl.BlockSpec((1, rows, d), lambda bi, i: (bi, i, 0)),
            pl.BlockSpec((1, rows, d), lambda bi, i: (bi, i, 0)),
        ],
        out_shape=[jax.ShapeDtypeStruct((b, s, d), f32), jax.ShapeDtypeStruct((b, s, d), bf16)],
        scratch_shapes=[pltpu.VMEM((rows + POOL_HALO, d), f32)],
        compiler_params=_params("arbitrary", "arbitrary"),
        name="pool_mixer",
    )(x, x, g_attn, w_pool, scale, g_ffn)


def _ffn_chunks(f, size):
    out, c0 = [], 0
    while c0 < f:
        out.append((c0, min(size, f - c0)))
        c0 += size
    return out


def _dense_ffn_kernel(h_ref, hn_ref, wg_ref, wu_ref, wd_ref, o_ref, *, chunk):
    hn = hn_ref[...]
    acc = h_ref[...]
    for c0, cw in _ffn_chunks(wg_ref.shape[1], chunk):
        g = jnp.dot(hn, wg_ref[:, c0:c0 + cw], preferred_element_type=f32)
        u = jnp.dot(hn, wu_ref[:, c0:c0 + cw], preferred_element_type=f32)
        a = (_silu(g) * u).astype(bf16)
        acc = acc + jnp.dot(a, wd_ref[c0:c0 + cw, :], preferred_element_type=f32)
    o_ref[...] = acc


def _dense_ffn(h, hn, wg, wu, wd, rows, chunk):
    t, d = h.shape
    f = wg.shape[1]
    return pl.pallas_call(
        functools.partial(_dense_ffn_kernel, chunk=chunk),
        grid=(t // rows,),
        in_specs=[
            pl.BlockSpec((rows, d), lambda i: (i, 0)),
            pl.BlockSpec((rows, d), lambda i: (i, 0)),
            _resident((d, f)),
            _resident((d, f)),
            _resident((f, d)),
        ],
        out_specs=pl.BlockSpec((rows, d), lambda i: (i, 0)),
        out_shape=jax.ShapeDtypeStruct((t, d), f32),
        compiler_params=_params("arbitrary"),
        name="dense_ffn",
    )(h, hn, wg, wu, wd)


def _rope(x, cos, sin_signed, lane):
    d = x.shape[-1]
    half = ROT_DIM // 2
    partner = jnp.where(lane < half, pltpu.roll(x, d - half, 1), pltpu.roll(x, half, 1))
    return x * cos + partner * sin_signed


def _qkv_kernel(h_ref, g_q_ref, g_kv_ref, wq_ref, wk_ref, wv_ref, cos_ref, sin_ref,
                qt_ref, k_ref, vt_ref, kmean_ref):
    i = pl.program_id(1)
    rows, d = h_ref.shape[1], h_ref.shape[2]
    heads = d // HEAD_DIM
    h = h_ref[0]
    xhat = h * _rms_scale(h)
    hq = (xhat * g_q_ref[...]).astype(bf16)
    hkv = (xhat * g_kv_ref[...]).astype(bf16)
    q = jnp.dot(hq, wq_ref[...], preferred_element_type=f32)
    k = jnp.dot(hkv, wk_ref[...], preferred_element_type=f32)
    v = jnp.dot(hkv, wv_ref[...], preferred_element_type=f32)
    cos = jnp.concatenate([cos_ref[...]] * heads, axis=-1)
    sin = jnp.concatenate([sin_ref[...]] * heads, axis=-1)
    lane = lax.broadcasted_iota(jnp.int32, (rows, d), 1) % HEAD_DIM
    q = _rope(q, cos, sin, lane) * (HEAD_DIM ** -0.5 * LOG2_E)
    k = _rope(k, cos, sin, lane)
    qt_ref[0] = q.T.astype(bf16)
    k_ref[0] = k.astype(bf16)
    vt_ref[0] = v.T.astype(bf16)

    nb = kmean_ref.shape[1]
    blk_iota = lax.broadcasted_iota(jnp.int32, (nb, d), 0)

    @pl.when(i == 0)
    def _():
        kmean_ref[0] = jnp.zeros((nb, d), f32)

    km = kmean_ref[0]
    for j in range(rows // BLOCK):
        mean_j = jnp.mean(k[j * BLOCK:(j + 1) * BLOCK, :], axis=0, keepdims=True)
        km = jnp.where(blk_iota == i * (rows // BLOCK) + j, mean_j, km)
    kmean_ref[0] = km


def _qkv_proj(h, g_q, g_kv, wq, wk, wv, cos, sin, rows):
    b, s, d = h.shape
    nb = s // BLOCK
    return pl.pallas_call(
        _qkv_kernel,
        grid=(b, s // rows),
        in_specs=[
            pl.BlockSpec((1, rows, d), lambda bi, i: (bi, i, 0)),
            _resident((1, d)),
            _resident((1, d)),
            _resident((d, d)),
            _resident((d, d)),
            _resident((d, d)),
            pl.BlockSpec((rows, HEAD_DIM), lambda bi, i: (i, 0)),
            pl.BlockSpec((rows, HEAD_DIM), lambda bi, i: (i, 0)),
        ],
        out_specs=[
            pl.BlockSpec((1, d, rows), lambda bi, i: (bi, 0, i)),
            pl.BlockSpec((1, rows, d), lambda bi, i: (bi, i, 0)),
            pl.BlockSpec((1, d, rows), lambda bi, i: (bi, 0, i)),
            pl.BlockSpec((1, nb, d), lambda bi, i: (bi, 0, 0)),
        ],
        out_shape=[
            jax.ShapeDtypeStruct((b, d, s), bf16),
            jax.ShapeDtypeStruct((b, s, d), bf16),
            jax.ShapeDtypeStruct((b, d, s), bf16),
            jax.ShapeDtypeStruct((b, nb, d), f32),
        ],
        compiler_params=_params("arbitrary", "arbitrary"),
        name="qkv_proj",
    )(h, g_q, g_kv, wq, wk, wv, cos, sin)


def _moba_kernel(qt_ref, k_ref, vt_ref, kmean_ref, o_ref, *scratch, heads):
    sel_refs, m_refs, l_refs, acc_refs = (scratch[i * heads:(i + 1) * heads] for i in range(4))
    j = pl.program_id(1)
    nb = kmean_ref.shape[1]
    hd = HEAD_DIM
    blk = lax.broadcasted_iota(jnp.int32, (nb, BLOCK), 0)
    past = blk < j
    kpos = lax.broadcasted_iota(jnp.int32, (BLOCK, BLOCK), 0)
    qpos = lax.broadcasted_iota(jnp.int32, (BLOCK, BLOCK), 1)
    causal = kpos <= qpos
    start = pl.multiple_of(j * BLOCK, BLOCK)

    def q_head(h):
        return qt_ref[0, h * hd:(h + 1) * hd, :]

    ones_rows = jnp.ones((ONES_ROWS, BLOCK), bf16)

    def score(h, off):
        return jnp.dot(k_ref[0, pl.ds(off, BLOCK), h * hd:(h + 1) * hd], q_head(h), preferred_element_type=f32)

    def weighted_values(h, p, off):
        vt_ones = jnp.concatenate([vt_ref[0, h * hd:(h + 1) * hd, pl.ds(off, BLOCK)], ones_rows], axis=0)
        out = jnp.dot(vt_ones, p, preferred_element_type=f32)
        return out[:hd], out[hd:hd + 1]

    def key_block(off, probs, accumulate):
        pending = {}
        for step in range(heads + MXU_LAG):
            if step < heads:
                pending[step] = probs(step, score(step, off))
            if step >= MXU_LAG:
                h = step - MXU_LAG
                p, extra = pending.pop(h)
                accumulate(h, extra, *weighted_values(h, p, off))

    for h in range(heads):
        gate = jnp.dot(kmean_ref[0, :, h * hd:(h + 1) * hd], q_head(h).astype(f32),
                       precision=lax.Precision.HIGHEST, preferred_element_type=f32)
        gate = jnp.where(past, gate, NEG_INF)
        rank = jnp.zeros((nb, BLOCK), jnp.int32)
        for m in range(nb):
            gm = gate[m:m + 1, :]
            rank = rank + jnp.where(gm > gate, 1, jnp.where(gm == gate, jnp.where(blk > m, 1, 0), 0))
        sel_refs[h][...] = jnp.where(past, jnp.where(rank < TOP_BLOCKS, 1.0, 0.0), 0.0)

    def own_probs(h, s):
        s = jnp.where(causal, s, NEG_INF)
        m0 = jnp.max(s, axis=0, keepdims=True)
        m_refs[h][...] = m0
        return jnp.exp2((s - m0).astype(bf16)), None

    def own_accumulate(h, _, pv, psum):
        acc_refs[h][...] = pv
        l_refs[h][...] = psum

    key_block(start, own_probs, own_accumulate)

    def body(n, carry):
        def probs(h, s):
            chosen = sel_refs[h][pl.ds(n, 1), :] > 0.0
            m_old = m_refs[h][...]
            m_new = jnp.maximum(m_old, jnp.where(chosen, jnp.max(s, axis=0, keepdims=True), NEG_INF))
            m_refs[h][...] = m_new
            p = jnp.exp2((s - jnp.where(chosen, m_new, POS_INF)).astype(bf16))
            return p, jnp.exp2(m_old - m_new)

        def accumulate(h, alpha, pv, psum):
            acc_refs[h][...] = alpha * acc_refs[h][...] + pv
            l_refs[h][...] = alpha * l_refs[h][...] + psum

        key_block(pl.multiple_of(n * BLOCK, BLOCK), probs, accumulate)
        return carry

    lax.fori_loop(0, j, body, 0)
    for h in range(heads):
        o_ref[0, :, h * hd:(h + 1) * hd] = (acc_refs[h][...] / l_refs[h][...]).T.astype(o_ref.dtype)


def _moba_attention(qt, k, vt, kmean):
    b, d, s = qt.shape
    heads = d // HEAD_DIM
    nb = s // BLOCK
    return pl.pallas_call(
        functools.partial(_moba_kernel, heads=heads),
        grid=(b, nb),
        in_specs=[
            pl.BlockSpec((1, d, BLOCK), lambda bi, j: (bi, 0, j)),
            pl.BlockSpec((1, s, d), lambda bi, j: (bi, 0, 0), pipeline_mode=pl.Buffered(1)),
            pl.BlockSpec((1, d, s), lambda bi, j: (bi, 0, 0), pipeline_mode=pl.Buffered(1)),
            pl.BlockSpec((1, nb, d), lambda bi, j: (bi, 0, 0)),
        ],
        out_specs=pl.BlockSpec((1, BLOCK, d), lambda bi, j: (bi, j, 0)),
        out_shape=jax.ShapeDtypeStruct((b, s, d), bf16),
        scratch_shapes=(
            [pltpu.VMEM((nb, BLOCK), f32)] * heads
            + [pltpu.VMEM((1, BLOCK), f32)] * (2 * heads)
            + [pltpu.VMEM((HEAD_DIM, BLOCK), f32)] * heads
        ),
        compiler_params=_params("arbitrary", "arbitrary"),
        name="moba_attn",
    )(qt, k, vt, kmean)


def _out_router_kernel(o_ref, h_ref, wo_ref, g_ref, wr_ref, h_out_ref, idx_ref, gate_ref):
    rows = o_ref.shape[0]
    n_exp = wr_ref.shape[1]
    h = h_ref[...] + jnp.dot(o_ref[...], wo_ref[...], preferred_element_type=f32)
    h_out_ref[...] = h
    hn = h * _rms_scale(h) * g_ref[...]
    wr = wr_ref[...]
    w_hi = wr.astype(bf16)
    w_lo = (wr - w_hi.astype(f32)).astype(bf16)
    x_hi = hn.astype(bf16)
    x_lo = (hn - x_hi.astype(f32)).astype(bf16)
    both = jnp.dot(x_hi, jnp.concatenate([w_hi, w_lo], axis=1), preferred_element_type=f32)
    logits = both[:, :n_exp] + both[:, n_exp:] + jnp.dot(x_lo, w_hi, preferred_element_type=f32)
    e = lax.broadcasted_iota(jnp.int32, (rows, n_exp), 1)
    v1 = jnp.max(logits, axis=-1, keepdims=True)
    i1 = jnp.min(jnp.where(logits == v1, e, n_exp), axis=-1, keepdims=True)
    rest = jnp.where(e == i1, NEG_INF, logits)
    v2 = jnp.max(rest, axis=-1, keepdims=True)
    i2 = jnp.min(jnp.where(rest == v2, e, n_exp), axis=-1, keepdims=True)
    e2 = jnp.exp(v2 - v1)
    denom = 1.0 + e2
    col = lax.broadcasted_iota(jnp.int32, (rows, TOP_K), 1)
    idx_ref[...] = jnp.where(col == 0, i1, i2)
    gate_ref[...] = jnp.where(col == 0, 1.0 / denom, e2 / denom)


def _out_router(o, h, wo, g_ffn, w_router, rows):
    t, d = h.shape
    n_exp = w_router.shape[1]
    return pl.pallas_call(
        _out_router_kernel,
        grid=(t // rows,),
        in_specs=[
            pl.BlockSpec((rows, d), lambda i: (i, 0)),
            pl.BlockSpec((rows, d), lambda i: (i, 0)),
            _resident((d, d)),
            _resident((1, d)),
            _resident((d, n_exp)),
        ],
        out_specs=[
            pl.BlockSpec((rows, d), lambda i: (i, 0)),
            pl.BlockSpec((rows, TOP_K), lambda i: (i, 0)),
            pl.BlockSpec((rows, TOP_K), lambda i: (i, 0)),
        ],
        out_shape=[
            jax.ShapeDtypeStruct((t, d), f32),
            jax.ShapeDtypeStruct((t, TOP_K), jnp.int32),
            jax.ShapeDtypeStruct((t, TOP_K), f32),
        ],
        compiler_params=_params("arbitrary"),
        name="out_router",
    )(o, h, wo, g_ffn, w_router)


def _dispatch_kernel(pos_ref, pad_lo_ref, pad_hi_ref, h_ref, g_ref, x_hbm, buf_ref, zero_ref, sem, zsem):
    i = pl.program_id(0)
    n_steps = pl.num_programs(0)
    rows, d = h_ref.shape
    sub = d // LANES
    slot = i % 2

    def staged_copies_wait(buf):
        for _ in range(TOP_K):
            pltpu.make_async_copy(buf_ref.at[buf], x_hbm.at[pl.ds(0, rows * sub), :], sem.at[buf]).wait()

    @pl.when(i == 0)
    def _():
        zero_ref[...] = jnp.zeros(zero_ref.shape, f32)
        for e in range(pad_lo_ref.shape[0]):
            lo, hi = pad_lo_ref[e], pad_hi_ref[e]

            def fill(r, carry):
                pltpu.make_async_copy(zero_ref, x_hbm.at[pl.ds(pl.multiple_of(r * sub, sub), sub), :],
                                      zsem).start()
                return carry

            def drain(r, carry):
                pltpu.make_async_copy(zero_ref, x_hbm.at[pl.ds(0, sub), :], zsem).wait()
                return carry
            lax.fori_loop(lo, hi, fill, 0)
            lax.fori_loop(lo, hi, drain, 0)

        buf_ref[1] = jnp.zeros(buf_ref.shape[1:], f32)
        tail_lo = pad_hi_ref[pad_hi_ref.shape[0] - 1]
        n_tail = (x_hbm.shape[0] // sub - tail_lo) // rows

        def tail_copy(c):
            dst = pl.multiple_of((tail_lo + c * rows) * sub, sub)
            return pltpu.make_async_copy(buf_ref.at[1], x_hbm.at[pl.ds(dst, rows * sub), :], zsem)

        def tail_fill(c, carry):
            tail_copy(c).start()
            return carry

        def tail_drain(c, carry):
            tail_copy(c).wait()
            return carry
        lax.fori_loop(0, n_tail, tail_fill, 0)
        lax.fori_loop(0, n_tail, tail_drain, 0)

    @pl.when(i >= 2)
    def _():
        staged_copies_wait(slot)

    h = h_ref[...]
    _store_token_tiles(buf_ref.at[slot], h * _rms_scale(h) * g_ref[...])
    base = i * (TOP_K * rows)

    def trip(c, carry):
        for u in range(DMA_UNROLL):
            r = c * DMA_UNROLL + u
            src = buf_ref.at[slot, pl.ds(pl.multiple_of(r * sub, sub), sub), :]
            for k in range(TOP_K):
                dst = pl.multiple_of(pos_ref[base + k * rows + r] * sub, sub)
                pltpu.make_async_copy(src, x_hbm.at[pl.ds(dst, sub), :], sem.at[slot]).start(priority=k % 2)
        return carry
    lax.fori_loop(0, rows // DMA_UNROLL, trip, 0)

    @pl.when(i == n_steps - 1)
    def _():
        staged_copies_wait(slot)

        @pl.when(n_steps >= 2)
        def _():
            staged_copies_wait(1 - slot)


def _moe_dispatch(pos_steps, pad_lo, pad_hi, h, g_ffn, n_rows, rows):
    t, d = h.shape
    sub = d // LANES
    grid_spec = pltpu.PrefetchScalarGridSpec(
        num_scalar_prefetch=3,
        grid=(t // rows,),
        in_specs=[
            pl.BlockSpec((rows, d), lambda i, ps, lo, hi: (i, 0)),
            pl.BlockSpec((1, d), lambda i, ps, lo, hi: (0, 0)),
        ],
        out_specs=pl.BlockSpec(memory_space=pl.ANY),
        scratch_shapes=[
            pltpu.VMEM((2, rows * sub, LANES), f32),
            pltpu.VMEM((sub, LANES), f32),
            pltpu.SemaphoreType.DMA((2,)),
            pltpu.SemaphoreType.DMA,
        ],
    )
    return pl.pallas_call(
        _dispatch_kernel,
        grid_spec=grid_spec,
        out_shape=jax.ShapeDtypeStruct((n_rows * sub, LANES), f32),
        compiler_params=_params("arbitrary", disable_bounds_checks=True),
        name="moe_dispatch",
    )(pos_steps, pad_lo, pad_hi, h, g_ffn)


def _moe_gemm_kernel(tile_expert_ref, tile_valid_ref, tile_src_ref, x_ref, wg_ref, wu_ref, wd_ref,
                     y_ref, xb_ref, acc_ref):
    i = pl.program_id(0)
    f = pl.program_id(1)
    rows, d = xb_ref.shape
    sub = d // LANES
    valid = tile_valid_ref[i] > 0

    @pl.when(jnp.logical_and(f == 0, valid))
    def _():
        for c in range(sub):
            xb_ref[:, c * LANES:(c + 1) * LANES] = _load_token_tiles(x_ref, rows, sub, c).astype(bf16)
        acc_ref[...] = jnp.zeros(acc_ref.shape, f32)

    @pl.when(valid)
    def _():
        xb = xb_ref[...]
        g = jnp.dot(xb, wg_ref[0].astype(bf16), preferred_element_type=f32)
        u = jnp.dot(xb, wu_ref[0].astype(bf16), preferred_element_type=f32)
        a = (_silu(g) * u).astype(bf16)
        acc_ref[...] += jnp.dot(a, wd_ref[0].astype(bf16), preferred_element_type=f32)

    @pl.when(f == pl.num_programs(1) - 1)
    def _():
        _store_token_tiles(y_ref, jnp.where(valid, acc_ref[...], 0.0))


def _moe_gemm(tile_expert, tile_valid, tile_src, x_tiles, wg, wu, wd, rows, chunk):
    n_tiles = tile_expert.shape[0]
    _, d, fe = wg.shape
    sub = d // LANES
    grid_spec = pltpu.PrefetchScalarGridSpec(
        num_scalar_prefetch=3,
        grid=(n_tiles, fe // chunk),
        in_specs=[
            pl.BlockSpec((rows * sub, LANES), lambda i, f, te, tv, ts: (ts[i], 0)),
            pl.BlockSpec((1, d, chunk), lambda i, f, te, tv, ts: (te[i], 0, f)),
            pl.BlockSpec((1, d, chunk), lambda i, f, te, tv, ts: (te[i], 0, f)),
            pl.BlockSpec((1, chunk, d), lambda i, f, te, tv, ts: (te[i], f, 0)),
        ],
        out_specs=pl.BlockSpec((rows * sub, LANES), lambda i, f, te, tv, ts: (i, 0)),
        scratch_shapes=[
            pltpu.VMEM((rows, d), bf16),
            pltpu.VMEM((rows, d), f32),
        ],
    )
    return pl.pallas_call(
        _moe_gemm_kernel,
        grid_spec=grid_spec,
        out_shape=jax.ShapeDtypeStruct((n_tiles * rows * sub, LANES), f32),
        compiler_params=_params("arbitrary", "arbitrary"),
        name="moe_gemm",
    )(tile_expert, tile_valid, tile_src, x_tiles, wg, wu, wd)


def _combine_kernel(pos_ref, h_ref, gate_ref, g_ref, y_hbm, o_ref, ybuf_ref, sem):
    i = pl.program_id(0)
    rows, d = h_ref.shape
    sub = d // LANES
    slot = i % 2

    def gather_rows(step, buf):
        base = step * (TOP_K * rows)

        def trip(c, carry):
            for u in range(DMA_UNROLL):
                r = c * DMA_UNROLL + u
                for k in range(TOP_K):
                    src = pl.multiple_of(pos_ref[base + k * rows + r] * sub, sub)
                    pltpu.make_async_copy(y_hbm.at[pl.ds(src, sub), :],
                                          ybuf_ref.at[buf, k, pl.ds(pl.multiple_of(r * sub, sub), sub), :],
                                          sem.at[buf]).start(priority=k % 2)
            return carry
        lax.fori_loop(0, rows // DMA_UNROLL, trip, 0)

    @pl.when(i == 0)
    def _():
        gather_rows(0, 0)

    @pl.when(i + 1 < pl.num_programs(0))
    def _():
        gather_rows(i + 1, 1 - slot)

    for k in range(TOP_K):
        pltpu.make_async_copy(y_hbm.at[pl.ds(0, rows * sub), :], ybuf_ref.at[slot, k], sem.at[slot]).wait()
    gate = gate_ref[...]
    chunks = []
    sumsq = jnp.zeros((rows, 1), f32)
    for c in range(sub):
        hc = h_ref[:, c * LANES:(c + 1) * LANES]
        for k in range(TOP_K):
            hc = hc + gate[:, k:k + 1] * _load_token_tiles(ybuf_ref.at[slot, k], rows, sub, c)
        sumsq = sumsq + jnp.sum(hc * hc, axis=-1, keepdims=True)
        chunks.append(hc)
    scale = lax.rsqrt(sumsq / d + EPS)
    for c in range(sub):
        o_ref[:, c * LANES:(c + 1) * LANES] = chunks[c] * scale * g_ref[:, c * LANES:(c + 1) * LANES]


def _moe_combine(pos, h, gates, g_final, y_tiles, rows):
    t, d = h.shape
    sub = d // LANES
    grid_spec = pltpu.PrefetchScalarGridSpec(
        num_scalar_prefetch=1,
        grid=(t // rows,),
        in_specs=[
            pl.BlockSpec((rows, d), lambda i, ps: (i, 0)),
            pl.BlockSpec((rows, TOP_K), lambda i, ps: (i, 0)),
            pl.BlockSpec((1, d), lambda i, ps: (0, 0)),
            pl.BlockSpec(memory_space=pl.ANY),
        ],
        out_specs=pl.BlockSpec((rows, d), lambda i, ps: (i, 0)),
        scratch_shapes=[pltpu.VMEM((2, TOP_K, rows * sub, LANES), f32), pltpu.SemaphoreType.DMA((2,))],
    )
    return pl.pallas_call(
        _combine_kernel,
        grid_spec=grid_spec,
        out_shape=jax.ShapeDtypeStruct((t, d), f32),
        compiler_params=_params("arbitrary", disable_bounds_checks=True),
        name="moe_combine",
    )(pos, h, gates, g_final, y_tiles)


def _routing_plan(idx, n_exp, rows, disp_rows, comb_rows):
    t = idx.shape[0]
    pairs = t * TOP_K
    n_tiles = -(-pairs // rows) + n_exp
    e_flat = idx.T.reshape(pairs)
    onehot = (e_flat[:, None] == jnp.arange(n_exp, dtype=jnp.int32)[None, :]).astype(jnp.int32)
    csum = jnp.cumsum(onehot, axis=0)
    counts = csum[-1]
    rank = jnp.sum(onehot * csum, axis=1) - 1
    padded = ((counts + rows - 1) // rows) * rows
    pend = jnp.cumsum(padded)
    poff = pend - padded
    pos = poff[e_flat] + rank
    tile_start = jnp.arange(n_tiles, dtype=jnp.int32) * rows
    tile_expert = jnp.minimum(jnp.sum((tile_start[:, None] >= pend[None, :]).astype(jnp.int32), axis=1),
                              n_exp - 1)
    tile_valid = (tile_start < pend[-1]).astype(jnp.int32)
    last_valid = jnp.maximum(jnp.sum(tile_valid) - 1, 0)
    tile_expert = jnp.where(tile_valid > 0, tile_expert, tile_expert[last_valid])
    tile_src = jnp.minimum(jnp.arange(n_tiles, dtype=jnp.int32), last_valid)

    def by_step(step_rows):
        return pos.reshape(TOP_K, t // step_rows, step_rows).transpose(1, 0, 2).reshape(pairs)

    return (tile_expert, tile_valid, tile_src, poff + counts, pend, n_tiles * rows,
            by_step(disp_rows), by_step(comb_rows))


def _rope_tables(s):
    half = ROT_DIM // 2
    inv = ROPE_THETA ** (-jnp.arange(half, dtype=f32) * 2.0 / ROT_DIM)
    ang = jnp.arange(s).astype(f32)[:, None] * inv[None, :]
    cos, sin = jnp.cos(ang), jnp.sin(ang)
    pad = HEAD_DIM - ROT_DIM
    cos_t = jnp.concatenate([cos, cos, jnp.ones((s, pad), f32)], axis=-1)
    sin_t = jnp.concatenate([-sin, sin, jnp.zeros((s, pad), f32)], axis=-1)
    return cos_t, sin_t


def kernel(x, attn_norm, ffn_norm, pool_w, pool_scale, wq, wo, kv_norm, wk, wv, ffn_w_gate, ffn_w_up,
           ffn_w_down, router_w, moe_w_gate, moe_w_up, moe_w_down, final_norm):
    b, s, d = x.shape
    t = b * s
    n_exp = router_w.shape[-1]
    assert attn_norm.shape[0] == 2 and pool_w.shape[0] == 1 and wq.shape[0] == 1 and router_w.shape[0] == 1
    assert s % 512 == 0 and d % (HEAD_DIM * 2) == 0

    tok_rows = 512
    moe_chunk = 512
    moe_rows = 1024
    comb_rows = 256
    assert moe_w_gate.shape[-1] % moe_chunk == 0 and d % (8 * LANES) == 0 and moe_rows % tok_rows == 0

    row = lambda v: v.reshape(1, -1)

    h1, hn1 = _pool_mixer(x, row(attn_norm[0]), pool_w[0].astype(bf16), row(pool_scale[0]),
                          row(ffn_norm[0]), tok_rows)
    h2 = _dense_ffn(h1.reshape(t, d), hn1.reshape(t, d), ffn_w_gate[0].astype(bf16),
                    ffn_w_up[0].astype(bf16), ffn_w_down[0].astype(bf16), tok_rows, 512)

    cos_t, sin_t = _rope_tables(s)
    qt, k, vt, kmean = _qkv_proj(h2.reshape(b, s, d), row(attn_norm[1]), row(kv_norm), wq[0].astype(bf16),
                                 wk.astype(bf16), wv.astype(bf16), cos_t, sin_t, tok_rows)
    o = _moba_attention(qt, k, vt, kmean)

    h3, idx, gates = _out_router(o.reshape(t, d), h2, wo[0].astype(bf16), row(ffn_norm[1]), router_w[0],
                                 tok_rows)

    (tile_expert, tile_valid, tile_src, pad_lo, pad_hi, n_rows, pos_disp,
     pos_comb) = _routing_plan(idx, n_exp, moe_rows, tok_rows, comb_rows)
    x_tiles = _moe_dispatch(pos_disp, pad_lo, pad_hi, h3, row(ffn_norm[1]), n_rows, tok_rows)
    y_tiles = _moe_gemm(tile_expert, tile_valid, tile_src, x_tiles, moe_w_gate[0], moe_w_up[0],
                        moe_w_down[0], moe_rows, moe_chunk)
    out = _moe_combine(pos_comb, h3, gates, row(final_norm), y_tiles, comb_rows)
    return out.reshape(b, s, d)
```

```python
import functools

import jax
import jax.numpy as jnp
from jax import lax
from jax.experimental import pallas as pl
from jax.experimental.pallas import tpu as pltpu

EPS = 1e-6
POOL_WINDOWS = (2, 4, 8, 16)
POOL_HALO = 16
LANES = 128
HEAD_DIM = 128
BLOCK = 256
TOP_BLOCKS = 3
ROT_DIM = HEAD_DIM // 4
ROPE_THETA = 500000.0
TOP_K = 2

VMEM_LIMIT = 56 * 1024 * 1024
DMA_UNROLL = 8
NEG_INF = float("-inf")
POS_INF = float("inf")
MXU_LAG = 8
ONES_ROWS = 16
LOG2_E = 1.4426950408889634

f32 = jnp.float32
bf16 = jnp.bfloat16


def _resident(shape):
    n = len(shape)
    return pl.BlockSpec(shape, lambda *_: (0,) * n, pipeline_mode=pl.Buffered(1))


def _rms_scale(xf):
    return lax.rsqrt(jnp.mean(xf * xf, axis=-1, keepdims=True) + EPS)


def _silu(g):
    return g * (1.0 / (1.0 + jnp.exp(-g)))


def _store_token_tiles(ref, x):
    rows, d = x.shape
    sub = d // LANES
    for c in range(sub):
        ref[pl.ds(c, rows, stride=sub), :] = x[:, c * LANES:(c + 1) * LANES]


def _load_token_tiles(ref, rows, sub, chunk):
    return ref[pl.ds(chunk, rows, stride=sub), :]


def _params(*sem, **kw):
    return pltpu.CompilerParams(dimension_semantics=sem, vmem_limit_bytes=VMEM_LIMIT, **kw)


def _pool_kernel(x_ref, halo_ref, g_attn_ref, w_ref, scale_ref, g_ffn_ref, h_ref, hn_ref, ext_ref):
    i = pl.program_id(1)
    rows = x_ref.shape[1]
    d = x_ref.shape[2]
    c = d // len(POOL_WINDOWS)
    g_attn = g_attn_ref[...]
    x = x_ref[0]
    xn = x * _rms_scale(x) * g_attn
    halo = halo_ref[0]
    halo = halo * _rms_scale(halo) * g_attn
    ext_ref[0:POOL_HALO, :] = jnp.where(i == 0, 0.0, halo)
    ext_ref[POOL_HALO:, :] = xn
    t = i * rows + lax.broadcasted_iota(jnp.int32, (rows, 1), 0)
    ys = []
    for g, w in enumerate(POOL_WINDOWS):
        c0 = g * c
        s = ext_ref[:, c0:c0 + c]
        shift = 1
        while shift < w:
            s = s + pltpu.roll(s, shift, 0)
            shift *= 2
        cnt = jnp.minimum(t + 1, w).astype(f32)
        pooled = s[POOL_HALO:, :] / cnt - xn[:, c0:c0 + c]
        ys.append(jnp.dot(pooled.astype(bf16), w_ref[g], preferred_element_type=f32))
    y = jnp.concatenate(ys, axis=-1) * scale_ref[...]
    h = x + y
    h_ref[0] = h
    hn_ref[0] = (h * _rms_scale(h) * g_ffn_ref[...]).astype(bf16)


def _pool_mixer(x, g_attn, w_pool, scale, g_ffn, rows):
    b, s, d = x.shape
    c = d // len(POOL_WINDOWS)
    hb = rows // POOL_HALO
    return pl.pallas_call(
        _pool_kernel,
        grid=(b, s // rows),
        in_specs=[
            pl.BlockSpec((1, rows, d), lambda bi, i: (bi, i, 0)),
            pl.BlockSpec((1, POOL_HALO, d), lambda bi, i: (bi, jnp.maximum(i * hb - 1, 0), 0)),
            _resident((1, d)),
            _resident((len(POOL_WINDOWS), c, c)),
            _resident((1, d)),
            _resident((1, d)),
        ],
        out_specs=[
            pl.BlockSpec((1, rows, d), lambda bi, i: (bi, i, 0)),
            pl.BlockSpec((1, rows, d), lambda bi, i: (bi, i, 0)),
        ],
        out_shape=[jax.ShapeDtypeStruct((b, s, d), f32), jax.ShapeDtypeStruct((b, s, d), bf16)],
        scratch_shapes=[pltpu.VMEM((rows + POOL_HALO, d), f32)],
        compiler_params=_params("arbitrary", "arbitrary"),
        name="pool_mixer",
    )(x, x, g_attn, w_pool, scale, g_ffn)


def _ffn_chunks(f, size):
    out, c0 = [], 0
    while c0 < f:
        out.append((c0, min(size, f - c0)))
        c0 += size
    return out


def _dense_ffn_kernel(h_ref, hn_ref, wg_ref, wu_ref, wd_ref, o_ref, *, chunk):
    hn = hn_ref[...]
    acc = h_ref[...]
    for c0, cw in _ffn_chunks(wg_ref.shape[1], chunk):
        g = jnp.dot(hn, wg_ref[:, c0:c0 + cw], preferred_element_type=f32)
        u = jnp.dot(hn, wu_ref[:, c0:c0 + cw], preferred_element_type=f32)
        a = (_silu(g) * u).astype(bf16)
        acc = acc + jnp.dot(a, wd_ref[c0:c0 + cw, :], preferred_element_type=f32)
    o_ref[...] = acc


def _dense_ffn(h, hn, wg, wu, wd, rows, chunk):
    t, d = h.shape
    f = wg.shape[1]
    return pl.pallas_call(
        functools.partial(_dense_ffn_kernel, chunk=chunk),
        grid=(t // rows,),
        in_specs=[
            pl.BlockSpec((rows, d), lambda i: (i, 0)),
            pl.BlockSpec((rows, d), lambda i: (i, 0)),
            _resident((d, f)),
            _resident((d, f)),
            _resident((f, d)),
        ],
        out_specs=pl.BlockSpec((rows, d), lambda i: (i, 0)),
        out_shape=jax.ShapeDtypeStruct((t, d), f32),
        compiler_params=_params("arbitrary"),
        name="dense_ffn",
    )(h, hn, wg, wu, wd)


def _rope(x, cos, sin_signed, lane):
    d = x.shape[-1]
    half = ROT_DIM // 2
    partner = jnp.where(lane < half, pltpu.roll(x, d - half, 1), pltpu.roll(x, half, 1))
    return x * cos + partner * sin_signed


def _qkv_kernel(h_ref, g_q_ref, g_kv_ref, wq_ref, wk_ref, wv_ref, cos_ref, sin_ref,
                qt_ref, k_ref, vt_ref, kmean_ref):
    i = pl.program_id(1)
    rows, d = h_ref.shape[1], h_ref.shape[2]
    heads = d // HEAD_DIM
    h = h_ref[0]
    xhat = h * _rms_scale(h)
    hq = (xhat * g_q_ref[...]).astype(bf16)
    hkv = (xhat * g_kv_ref[...]).astype(bf16)
    q = jnp.dot(hq, wq_ref[...], preferred_element_type=f32)
    k = jnp.dot(hkv, wk_ref[...], preferred_element_type=f32)
    v = jnp.dot(hkv, wv_ref[...], preferred_element_type=f32)
    cos = jnp.concatenate([cos_ref[...]] * heads, axis=-1)
    sin = jnp.concatenate([sin_ref[...]] * heads, axis=-1)
    lane = lax.broadcasted_iota(jnp.int32, (rows, d), 1) % HEAD_DIM
    q = _rope(q, cos, sin, lane) * (HEAD_DIM ** -0.5 * LOG2_E)
    k = _rope(k, cos, sin, lane)
    qt_ref[0] = q.T.astype(bf16)
    k_ref[0] = k.astype(bf16)
    vt_ref[0] = v.T.astype(bf16)

    nb = kmean_ref.shape[1]
    blk_iota = lax.broadcasted_iota(jnp.int32, (nb, d), 0)

    @pl.when(i == 0)
    def _():
        kmean_ref[0] = jnp.zeros((nb, d), f32)

    km = kmean_ref[0]
    for j in range(rows // BLOCK):
        mean_j = jnp.mean(k[j * BLOCK:(j + 1) * BLOCK, :], axis=0, keepdims=True)
        km = jnp.where(blk_iota == i * (rows // BLOCK) + j, mean_j, km)
    kmean_ref[0] = km


def _qkv_proj(h, g_q, g_kv, wq, wk, wv, cos, sin, rows):
    b, s, d = h.shape
    nb = s // BLOCK
    return pl.pallas_call(
        _qkv_kernel,
        grid=(b, s // rows),
        in_specs=[
            pl.BlockSpec((1, rows, d), lambda bi, i: (bi, i, 0)),
            _resident((1, d)),
            _resident((1, d)),
            _resident((d, d)),
            _resident((d, d)),
            _resident((d, d)),
            pl.BlockSpec((rows, HEAD_DIM), lambda bi, i: (i, 0)),
            pl.BlockSpec((rows, HEAD_DIM), lambda bi, i: (i, 0)),
        ],
        out_specs=[
            pl.BlockSpec((1, d, rows), lambda bi, i: (bi, 0, i)),
            pl.BlockSpec((1, rows, d), lambda bi, i: (bi, i, 0)),
            pl.BlockSpec((1, d, rows), lambda bi, i: (bi, 0, i)),
            pl.BlockSpec((1, nb, d), lambda bi, i: (bi, 0, 0)),
        ],
        out_shape=[
            jax.ShapeDtypeStruct((b, d, s), bf16),
            jax.ShapeDtypeStruct((b, s, d), bf16),
            jax.ShapeDtypeStruct((b, d, s), bf16),
            jax.ShapeDtypeStruct((b, nb, d), f32),
        ],
        compiler_params=_params("arbitrary", "arbitrary"),
        name="qkv_proj",
    )(h, g_q, g_kv, wq, wk, wv, cos, sin)


def _moba_kernel(qt_ref, k_ref, vt_ref, kmean_ref, o_ref, *scratch, heads):
    sel_refs, m_refs, l_refs, acc_refs = (scratch[i * heads:(i + 1) * heads] for i in range(4))
    j = pl.program_id(1)
    nb = kmean_ref.shape[1]
    hd = HEAD_DIM
    blk = lax.broadcasted_iota(jnp.int32, (nb, BLOCK), 0)
    past = blk < j
    kpos = lax.broadcasted_iota(jnp.int32, (BLOCK, BLOCK), 0)
    qpos = lax.broadcasted_iota(jnp.int32, (BLOCK, BLOCK), 1)
    causal = kpos <= qpos
    start = pl.multiple_of(j * BLOCK, BLOCK)

    def q_head(h):
        return qt_ref[0, h * hd:(h + 1) * hd, :]

    ones_rows = jnp.ones((ONES_ROWS, BLOCK), bf16)

    def score(h, off):
        return jnp.dot(k_ref[0, pl.ds(off, BLOCK), h * hd:(h + 1) * hd], q_head(h), preferred_element_type=f32)

    def weighted_values(h, p, off):
        vt_ones = jnp.concatenate([vt_ref[0, h * hd:(h + 1) * hd, pl.ds(off, BLOCK)], ones_rows], axis=0)
        out = jnp.dot(vt_ones, p, preferred_element_type=f32)
        return out[:hd], out[hd:hd + 1]

    def key_block(off, probs, accumulate):
        pending = {}
        for step in range(heads + MXU_LAG):
            if step < heads:
                pending[step] = probs(step, score(step, off))
            if step >= MXU_LAG:
                h = step - MXU_LAG
                p, extra = pending.pop(h)
                accumulate(h, extra, *weighted_values(h, p, off))

    for h in range(heads):
        km = kmean_ref[0, :, h * hd:(h + 1) * hd]
        km_hi = km.astype(bf16)
        km_lo = (km - km_hi.astype(f32)).astype(bf16)
        gate = (jnp.dot(km_hi, q_head(h), preferred_element_type=f32)
                + jnp.dot(km_lo, q_head(h), preferred_element_type=f32))
        gate = jnp.where(past, gate, NEG_INF)
        rank = jnp.zeros((nb, BLOCK), jnp.int32)
        for m in range(nb):
            gm = gate[m:m + 1, :]
            rank = rank + jnp.where(gm > gate, 1, jnp.where(gm == gate, jnp.where(blk > m, 1, 0), 0))
        sel_refs[h][...] = jnp.where(past, jnp.where(rank < TOP_BLOCKS, 1.0, 0.0), 0.0)

    def own_probs(h, s):
        s = jnp.where(causal, s, NEG_INF)
        m0 = jnp.max(s, axis=0, keepdims=True)
        m_refs[h][...] = m0
        return jnp.exp2((s - m0).astype(bf16)), None

    def own_accumulate(h, _, pv, psum):
        acc_refs[h][...] = pv
        l_refs[h][...] = psum

    key_block(start, own_probs, own_accumulate)

    def body(n, carry):
        def probs(h, s):
            chosen = sel_refs[h][pl.ds(n, 1), :] > 0.0
            m_old = m_refs[h][...]
            m_new = jnp.maximum(m_old, jnp.where(chosen, jnp.max(s, axis=0, keepdims=True), NEG_INF))
            m_refs[h][...] = m_new
            p = jnp.exp2((s - jnp.where(chosen, m_new, POS_INF)).astype(bf16))
            return p, jnp.exp2(m_old - m_new)

        def accumulate(h, alpha, pv, psum):
            acc_refs[h][...] = alpha * acc_refs[h][...] + pv
            l_refs[h][...] = alpha * l_refs[h][...] + psum

        key_block(pl.multiple_of(n * BLOCK, BLOCK), probs, accumulate)
        return carry

    lax.fori_loop(0, j, body, 0)
    for h in range(heads):
        o_ref[0, :, h * hd:(h + 1) * hd] = (acc_refs[h][...] / l_refs[h][...]).T.astype(o_ref.dtype)


def _moba_attention(qt, k, vt, kmean):
    b, d, s = qt.shape
    heads = d // HEAD_DIM
    nb = s // BLOCK
    return pl.pallas_call(
        functools.partial(_moba_kernel, heads=heads),
        grid=(b, nb),
        in_specs=[
            pl.BlockSpec((1, d, BLOCK), lambda bi, j: (bi, 0, j)),
            pl.BlockSpec((1, s, d), lambda bi, j: (bi, 0, 0), pipeline_mode=pl.Buffered(1)),
            pl.BlockSpec((1, d, s), lambda bi, j: (bi, 0, 0), pipeline_mode=pl.Buffered(1)),
            pl.BlockSpec((1, nb, d), lambda bi, j: (bi, 0, 0)),
        ],
        out_specs=pl.BlockSpec((1, BLOCK, d), lambda bi, j: (bi, j, 0)),
        out_shape=jax.ShapeDtypeStruct((b, s, d), bf16),
        scratch_shapes=(
            [pltpu.VMEM((nb, BLOCK), f32)] * heads
            + [pltpu.VMEM((1, BLOCK), f32)] * (2 * heads)
            + [pltpu.VMEM((HEAD_DIM, BLOCK), f32)] * heads
        ),
        compiler_params=_params("arbitrary", "arbitrary"),
        name="moba_attn",
    )(qt, k, vt, kmean)


def _out_router_kernel(o_ref, h_ref, wo_ref, g_ref, wr_ref, h_out_ref, idx_ref, gate_ref):
    rows = o_ref.shape[0]
    n_exp = wr_ref.shape[1]
    h = h_ref[...] + jnp.dot(o_ref[...], wo_ref[...], preferred_element_type=f32)
    h_out_ref[...] = h
    hn = h * _rms_scale(h) * g_ref[...]
    wr = wr_ref[...]
    w_hi = wr.astype(bf16)
    w_lo = (wr - w_hi.astype(f32)).astype(bf16)
    x_hi = hn.astype(bf16)
    x_lo = (hn - x_hi.astype(f32)).astype(bf16)
    both = jnp.dot(x_hi, jnp.concatenate([w_hi, w_lo], axis=1), preferred_element_type=f32)
    logits = both[:, :n_exp] + both[:, n_exp:] + jnp.dot(x_lo, w_hi, preferred_element_type=f32)
    e = lax.broadcasted_iota(jnp.int32, (rows, n_exp), 1)
    v1 = jnp.max(logits, axis=-1, keepdims=True)
    i1 = jnp.min(jnp.where(logits == v1, e, n_exp), axis=-1, keepdims=True)
    rest = jnp.where(e == i1, NEG_INF, logits)
    v2 = jnp.max(rest, axis=-1, keepdims=True)
    i2 = jnp.min(jnp.where(rest == v2, e, n_exp), axis=-1, keepdims=True)
    e2 = jnp.exp(v2 - v1)
    denom = 1.0 + e2
    col = lax.broadcasted_iota(jnp.int32, (rows, TOP_K), 1)
    idx_ref[...] = jnp.where(col == 0, i1, i2)
    gate_ref[...] = jnp.where(col == 0, 1.0 / denom, e2 / denom)


def _out_router(o, h, wo, g_ffn, w_router, rows):
    t, d = h.shape
    n_exp = w_router.shape[1]
    return pl.pallas_call(
        _out_router_kernel,
        grid=(t // rows,),
        in_specs=[
            pl.BlockSpec((rows, d), lambda i: (i, 0)),
            pl.BlockSpec((rows, d), lambda i: (i, 0)),
            _resident((d, d)),
            _resident((1, d)),
            _resident((d, n_exp)),
        ],
        out_specs=[
            pl.BlockSpec((rows, d), lambda i: (i, 0)),
            pl.BlockSpec((rows, TOP_K), lambda i: (i, 0)),
            pl.BlockSpec((rows, TOP_K), lambda i: (i, 0)),
        ],
        out_shape=[
            jax.ShapeDtypeStruct((t, d), f32),
            jax.ShapeDtypeStruct((t, TOP_K), jnp.int32),
            jax.ShapeDtypeStruct((t, TOP_K), f32),
        ],
        compiler_params=_params("arbitrary"),
        name="out_router",
    )(o, h, wo, g_ffn, w_router)


def _dispatch_kernel(pos_ref, pad_lo_ref, pad_hi_ref, h_ref, g_ref, x_hbm, buf_ref, zero_ref, sem, zsem):
    i = pl.program_id(0)
    n_steps = pl.num_programs(0)
    rows, d = h_ref.shape
    sub = d // LANES
    slot = i % 2

    def staged_copies_wait(buf):
        for _ in range(TOP_K):
            pltpu.make_async_copy(buf_ref.at[buf], x_hbm.at[pl.ds(0, rows * sub), :], sem.at[buf]).wait()

    @pl.when(i == 0)
    def _():
        zero_ref[...] = jnp.zeros(zero_ref.shape, f32)
        for e in range(pad_lo_ref.shape[0]):
            lo, hi = pad_lo_ref[e], pad_hi_ref[e]

            def fill(r, carry):
                pltpu.make_async_copy(zero_ref, x_hbm.at[pl.ds(pl.multiple_of(r * sub, sub), sub), :],
                                      zsem).start()
                return carry

            def drain(r, carry):
                pltpu.make_async_copy(zero_ref, x_hbm.at[pl.ds(0, sub), :], zsem).wait()
                return carry
            lax.fori_loop(lo, hi, fill, 0)
            lax.fori_loop(lo, hi, drain, 0)

        buf_ref[1] = jnp.zeros(buf_ref.shape[1:], f32)
        tail_lo = pad_hi_ref[pad_hi_ref.shape[0] - 1]
        n_tail = (x_hbm.shape[0] // sub - tail_lo) // rows

        def tail_copy(c):
            dst = pl.multiple_of((tail_lo + c * rows) * sub, sub)
            return pltpu.make_async_copy(buf_ref.at[1], x_hbm.at[pl.ds(dst, rows * sub), :], zsem)

        def tail_fill(c, carry):
            tail_copy(c).start()
            return carry

        def tail_drain(c, carry):
            tail_copy(c).wait()
            return carry
        lax.fori_loop(0, n_tail, tail_fill, 0)
        lax.fori_loop(0, n_tail, tail_drain, 0)

    @pl.when(i >= 2)
    def _():
        staged_copies_wait(slot)

    h = h_ref[...]
    _store_token_tiles(buf_ref.at[slot], h * _rms_scale(h) * g_ref[...])
    base = i * (TOP_K * rows)

    def trip(c, carry):
        for u in range(DMA_UNROLL):
            r = c * DMA_UNROLL + u
            src = buf_ref.at[slot, pl.ds(pl.multiple_of(r * sub, sub), sub), :]
            for k in range(TOP_K):
                dst = pl.multiple_of(pos_ref[base + k * rows + r] * sub, sub)
                pltpu.make_async_copy(src, x_hbm.at[pl.ds(dst, sub), :], sem.at[slot]).start(priority=k % 2)
        return carry
    lax.fori_loop(0, rows // DMA_UNROLL, trip, 0)

    @pl.when(i == n_steps - 1)
    def _():
        staged_copies_wait(slot)

        @pl.when(n_steps >= 2)
        def _():
            staged_copies_wait(1 - slot)


def _moe_dispatch(pos_steps, pad_lo, pad_hi, h, g_ffn, n_rows, rows):
    t, d = h.shape
    sub = d // LANES
    grid_spec = pltpu.PrefetchScalarGridSpec(
        num_scalar_prefetch=3,
        grid=(t // rows,),
        in_specs=[
            pl.BlockSpec((rows, d), lambda i, ps, lo, hi: (i, 0)),
            pl.BlockSpec((1, d), lambda i, ps, lo, hi: (0, 0)),
        ],
        out_specs=pl.BlockSpec(memory_space=pl.ANY),
        scratch_shapes=[
            pltpu.VMEM((2, rows * sub, LANES), f32),
            pltpu.VMEM((sub, LANES), f32),
            pltpu.SemaphoreType.DMA((2,)),
            pltpu.SemaphoreType.DMA,
        ],
    )
    return pl.pallas_call(
        _dispatch_kernel,
        grid_spec=grid_spec,
        out_shape=jax.ShapeDtypeStruct((n_rows * sub, LANES), f32),
        compiler_params=_params("arbitrary", disable_bounds_checks=True),
        name="moe_dispatch",
    )(pos_steps, pad_lo, pad_hi, h, g_ffn)


def _moe_gemm_kernel(tile_expert_ref, tile_valid_ref, tile_src_ref, x_ref, wg_ref, wu_ref, wd_ref,
                     y_ref, xb_ref, acc_ref):
    i = pl.program_id(0)
    f = pl.program_id(1)
    rows, d = xb_ref.shape
    sub = d // LANES
    valid = tile_valid_ref[i] > 0

    @pl.when(jnp.logical_and(f == 0, valid))
    def _():
        for c in range(sub):
            xb_ref[:, c * LANES:(c + 1) * LANES] = _load_token_tiles(x_ref, rows, sub, c).astype(bf16)
        acc_ref[...] = jnp.zeros(acc_ref.shape, f32)

    @pl.when(valid)
    def _():
        xb = xb_ref[...]
        g = jnp.dot(xb, wg_ref[0].astype(bf16), preferred_element_type=f32)
        u = jnp.dot(xb, wu_ref[0].astype(bf16), preferred_element_type=f32)
        a = (_silu(g) * u).astype(bf16)
        acc_ref[...] += jnp.dot(a, wd_ref[0].astype(bf16), preferred_element_type=f32)

    @pl.when(f == pl.num_programs(1) - 1)
    def _():
        _store_token_tiles(y_ref, jnp.where(valid, acc_ref[...], 0.0))


def _moe_gemm(tile_expert, tile_valid, tile_src, x_tiles, wg, wu, wd, rows, chunk):
    n_tiles = tile_expert.shape[0]
    _, d, fe = wg.shape
    sub = d // LANES
    grid_spec = pltpu.PrefetchScalarGridSpec(
        num_scalar_prefetch=3,
        grid=(n_tiles, fe // chunk),
        in_specs=[
            pl.BlockSpec((rows * sub, LANES), lambda i, f, te, tv, ts: (ts[i], 0)),
            pl.BlockSpec((1, d, chunk), lambda i, f, te, tv, ts: (te[i], 0, f)),
            pl.BlockSpec((1, d, chunk), lambda i, f, te, tv, ts: (te[i], 0, f)),
            pl.BlockSpec((1, chunk, d), lambda i, f, te, tv, ts: (te[i], f, 0)),
        ],
        out_specs=pl.BlockSpec((rows * sub, LANES), lambda i, f, te, tv, ts: (i, 0)),
        scratch_shapes=[
            pltpu.VMEM((rows, d), bf16),
            pltpu.VMEM((rows, d), f32),
        ],
    )
    return pl.pallas_call(
        _moe_gemm_kernel,
        grid_spec=grid_spec,
        out_shape=jax.ShapeDtypeStruct((n_tiles * rows * sub, LANES), f32),
        compiler_params=_params("arbitrary", "arbitrary"),
        name="moe_gemm",
    )(tile_expert, tile_valid, tile_src, x_tiles, wg, wu, wd)


def _combine_kernel(pos_ref, h_ref, gate_ref, g_ref, y_hbm, o_ref, ybuf_ref, sem):
    i = pl.program_id(0)
    rows, d = h_ref.shape
    sub = d // LANES
    slot = i % 2

    def gather_rows(step, buf):
        base = step * (TOP_K * rows)

        def trip(c, carry):
            for u in range(DMA_UNROLL):
                r = c * DMA_UNROLL + u
                for k in range(TOP_K):
                    src = pl.multiple_of(pos_ref[base + k * rows + r] * sub, sub)
                    pltpu.make_async_copy(y_hbm.at[pl.ds(src, sub), :],
                                          ybuf_ref.at[buf, k, pl.ds(pl.multiple_of(r * sub, sub), sub), :],
                                          sem.at[buf]).start(priority=k % 2)
            return carry
        lax.fori_loop(0, rows // DMA_UNROLL, trip, 0)

    @pl.when(i == 0)
    def _():
        gather_rows(0, 0)

    @pl.when(i + 1 < pl.num_programs(0))
    def _():
        gather_rows(i + 1, 1 - slot)

    for k in range(TOP_K):
        pltpu.make_async_copy(y_hbm.at[pl.ds(0, rows * sub), :], ybuf_ref.at[slot, k], sem.at[slot]).wait()
    gate = gate_ref[...]
    chunks = []
    sumsq = jnp.zeros((rows, 1), f32)
    for c in range(sub):
        hc = h_ref[:, c * LANES:(c + 1) * LANES]
        for k in range(TOP_K):
            hc = hc + gate[:, k:k + 1] * _load_token_tiles(ybuf_ref.at[slot, k], rows, sub, c)
        sumsq = sumsq + jnp.sum(hc * hc, axis=-1, keepdims=True)
        chunks.append(hc)
    scale = lax.rsqrt(sumsq / d + EPS)
    for c in range(sub):
        o_ref[:, c * LANES:(c + 1) * LANES] = chunks[c] * scale * g_ref[:, c * LANES:(c + 1) * LANES]


def _moe_combine(pos, h, gates, g_final, y_tiles, rows):
    t, d = h.shape
    sub = d // LANES
    grid_spec = pltpu.PrefetchScalarGridSpec(
        num_scalar_prefetch=1,
        grid=(t // rows,),
        in_specs=[
            pl.BlockSpec((rows, d), lambda i, ps: (i, 0)),
            pl.BlockSpec((rows, TOP_K), lambda i, ps: (i, 0)),
            pl.BlockSpec((1, d), lambda i, ps: (0, 0)),
            pl.BlockSpec(memory_space=pl.ANY),
        ],
        out_specs=pl.BlockSpec((rows, d), lambda i, ps: (i, 0)),
        scratch_shapes=[pltpu.VMEM((2, TOP_K, rows * sub, LANES), f32), pltpu.SemaphoreType.DMA((2,))],
    )
    return pl.pallas_call(
        _combine_kernel,
        grid_spec=grid_spec,
        out_shape=jax.ShapeDtypeStruct((t, d), f32),
        compiler_params=_params("arbitrary", disable_bounds_checks=True),
        name="moe_combine",
    )(pos, h, gates, g_final, y_tiles)


def _routing_plan(idx, n_exp, rows, disp_rows, comb_rows):
    t = idx.shape[0]
    pairs = t * TOP_K
    n_tiles = -(-pairs // rows) + n_exp
    e_flat = idx.T.reshape(pairs)
    onehot = (e_flat[:, None] == jnp.arange(n_exp, dtype=jnp.int32)[None, :]).astype(jnp.int32)
    csum = jnp.cumsum(onehot, axis=0)
    counts = csum[-1]
    rank = jnp.sum(onehot * csum, axis=1) - 1
    padded = ((counts + rows - 1) // rows) * rows
    pend = jnp.cumsum(padded)
    poff = pend - padded
    pos = poff[e_flat] + rank
    tile_start = jnp.arange(n_tiles, dtype=jnp.int32) * rows
    tile_expert = jnp.minimum(jnp.sum((tile_start[:, None] >= pend[None, :]).astype(jnp.int32), axis=1),
                              n_exp - 1)
    tile_valid = (tile_start < pend[-1]).astype(jnp.int32)
    last_valid = jnp.maximum(jnp.sum(tile_valid) - 1, 0)
    tile_expert = jnp.where(tile_valid > 0, tile_expert, tile_expert[last_valid])
    tile_src = jnp.minimum(jnp.arange(n_tiles, dtype=jnp.int32), last_valid)

    def by_step(step_rows):
        return pos.reshape(TOP_K, t // step_rows, step_rows).transpose(1, 0, 2).reshape(pairs)

    return (tile_expert, tile_valid, tile_src, poff + counts, pend, n_tiles * rows,
            by_step(disp_rows), by_step(comb_rows))


def _rope_tables(s):
    half = ROT_DIM // 2
    inv = ROPE_THETA ** (-jnp.arange(half, dtype=f32) * 2.0 / ROT_DIM)
    ang = jnp.arange(s).astype(f32)[:, None] * inv[None, :]
    cos, sin = jnp.cos(ang), jnp.sin(ang)
    pad = HEAD_DIM - ROT_DIM
    cos_t = jnp.concatenate([cos, cos, jnp.ones((s, pad), f32)], axis=-1)
    sin_t = jnp.concatenate([-sin, sin, jnp.zeros((s, pad), f32)], axis=-1)
    return cos_t, sin_t


def kernel(x, attn_norm, ffn_norm, pool_w, pool_scale, wq, wo, kv_norm, wk, wv, ffn_w_gate, ffn_w_up,
           ffn_w_down, router_w, moe_w_gate, moe_w_up, moe_w_down, final_norm):
    b, s, d = x.shape
    t = b * s
    n_exp = router_w.shape[-1]
    assert attn_norm.shape[0] == 2 and pool_w.shape[0] == 1 and wq.shape[0] == 1 and router_w.shape[0] == 1
    assert s % 512 == 0 and d % (HEAD_DIM * 2) == 0

    tok_rows = 512
    moe_chunk = 512
    moe_rows = 1024
    comb_rows = 512
    assert moe_w_gate.shape[-1] % moe_chunk == 0 and d % (8 * LANES) == 0 and moe_rows % tok_rows == 0

    row = lambda v: v.reshape(1, -1)

    h1, hn1 = _pool_mixer(x, row(attn_norm[0]), pool_w[0].astype(bf16), row(pool_scale[0]),
                          row(ffn_norm[0]), tok_rows)
    h2 = _dense_ffn(h1.reshape(t, d), hn1.reshape(t, d), ffn_w_gate[0].astype(bf16),
                    ffn_w_up[0].astype(bf16), ffn_w_down[0].astype(bf16), tok_rows, 512)

    cos_t, sin_t = _rope_tables(s)
    qt, k, vt, kmean = _qkv_proj(h2.reshape(b, s, d), row(attn_norm[1]), row(kv_norm), wq[0].astype(bf16),
                                 wk.astype(bf16), wv.astype(bf16), cos_t, sin_t, tok_rows)
    o = _moba_attention(qt, k, vt, kmean)

    h3, idx, gates = _out_router(o.reshape(t, d), h2, wo[0].astype(bf16), row(ffn_norm[1]), router_w[0],
                                 tok_rows)

    (tile_expert, tile_valid, tile_src, pad_lo, pad_hi, n_rows, pos_disp,
     pos_comb) = _routing_plan(idx, n_exp, moe_rows, tok_rows, comb_rows)
    x_tiles = _moe_dispatch(pos_disp, pad_lo, pad_hi, h3, row(ffn_norm[1]), n_rows, tok_rows)
    y_tiles = _moe_gemm(tile_expert, tile_valid, tile_src, x_tiles, moe_w_gate[0], moe_w_up[0],
                        moe_w_down[0], moe_rows, moe_chunk)
    out = _moe_combine(pos_comb, h3, gates, row(final_norm), y_tiles, comb_rows)
    return out.reshape(b, s, d)
```

```python
import functools

import jax
import jax.numpy as jnp
from jax import lax
from jax.experimental import pallas as pl
from jax.experimental.pallas import tpu as pltpu

EPS = 1e-6
POOL_WINDOWS = (2, 4, 8, 16)
POOL_HALO = 16
LANES = 128
HEAD_DIM = 128
BLOCK = 256
TOP_BLOCKS = 3
ROT_DIM = HEAD_DIM // 4
ROPE_THETA = 500000.0
TOP_K = 2

VMEM_LIMIT = 56 * 1024 * 1024
DMA_UNROLL = 8
NEG_INF = float("-inf")
POS_INF = float("inf")
MXU_LAG = 8
ONES_ROWS = 16
LOG2_E = 1.4426950408889634

f32 = jnp.float32
bf16 = jnp.bfloat16


def _resident(shape):
    n = len(shape)
    return pl.BlockSpec(shape, lambda *_: (0,) * n, pipeline_mode=pl.Buffered(1))


def _rms_scale(xf):
    return lax.rsqrt(jnp.mean(xf * xf, axis=-1, keepdims=True) + EPS)


def _silu(g):
    return g * (1.0 / (1.0 + jnp.exp(-g)))


def _store_token_tiles(ref, x):
    rows, d = x.shape
    sub = d // LANES
    for c in range(sub):
        ref[pl.ds(c, rows, stride=sub), :] = x[:, c * LANES:(c + 1) * LANES]


def _load_token_tiles(ref, rows, sub, chunk):
    return ref[pl.ds(chunk, rows, stride=sub), :]


def _params(*sem, **kw):
    return pltpu.CompilerParams(dimension_semantics=sem, vmem_limit_bytes=VMEM_LIMIT, **kw)


def _pool_kernel(x_ref, halo_ref, g_attn_ref, w_ref, scale_ref, g_ffn_ref, h_ref, hn_ref, ext_ref):
    i = pl.program_id(1)
    rows = x_ref.shape[1]
    d = x_ref.shape[2]
    c = d // len(POOL_WINDOWS)
    g_attn = g_attn_ref[...]
    x = x_ref[0]
    xn = x * _rms_scale(x) * g_attn
    halo = halo_ref[0]
    halo = halo * _rms_scale(halo) * g_attn
    ext_ref[0:POOL_HALO, :] = jnp.where(i == 0, 0.0, halo)
    ext_ref[POOL_HALO:, :] = xn
    t = i * rows + lax.broadcasted_iota(jnp.int32, (rows, 1), 0)
    ys = []
    for g, w in enumerate(POOL_WINDOWS):
        c0 = g * c
        s = ext_ref[:, c0:c0 + c]
        shift = 1
        while shift < w:
            s = s + pltpu.roll(s, shift, 0)
            shift *= 2
        cnt = jnp.minimum(t + 1, w).astype(f32)
        pooled = s[POOL_HALO:, :] / cnt - xn[:, c0:c0 + c]
        ys.append(jnp.dot(pooled.astype(bf16), w_ref[g], preferred_element_type=f32))
    y = jnp.concatenate(ys, axis=-1) * scale_ref[...]
    h = x + y
    h_ref[0] = h
    hn_ref[0] = (h * _rms_scale(h) * g_ffn_ref[...]).astype(bf16)


def _pool_mixer(x, g_attn, w_pool, scale, g_ffn, rows):
    b, s, d = x.shape
    c = d // len(POOL_WINDOWS)
    hb = rows // POOL_HALO
    return pl.pallas_call(
        _pool_kernel,
        grid=(b, s // rows),
        in_specs=[
            pl.BlockSpec((1, rows, d), lambda bi, i: (bi, i, 0)),
            pl.BlockSpec((1, POOL_HALO, d), lambda bi, i: (bi, jnp.maximum(i * hb - 1, 0), 0)),
            _resident((1, d)),
            _resident((len(POOL_WINDOWS), c, c)),
            _resident((1, d)),
            _resident((1, d)),
        ],
        out_specs=[
            pl.BlockSpec((1, rows, d), lambda bi, i: (bi, i, 0)),
            pl.BlockSpec((1, rows, d), lambda bi, i: (bi, i, 0)),
        ],
        out_shape=[jax.ShapeDtypeStruct((b, s, d), f32), jax.ShapeDtypeStruct((b, s, d), bf16)],
        scratch_shapes=[pltpu.VMEM((rows + POOL_HALO, d), f32)],
        compiler_params=_params("arbitrary", "arbitrary"),
        name="pool_mixer",
    )(x, x, g_attn, w_pool, scale, g_ffn)


def _ffn_chunks(f, size):
    out, c0 = [], 0
    while c0 < f:
        out.append((c0, min(size, f - c0)))
        c0 += size
    return out


def _dense_ffn_kernel(h_ref, hn_ref, wg_ref, wu_ref, wd_ref, o_ref, *, chunk):
    hn = hn_ref[...]
    acc = h_ref[...]
    for c0, cw in _ffn_chunks(wg_ref.shape[1], chunk):
        g = jnp.dot(hn, wg_ref[:, c0:c0 + cw], preferred_element_type=f32)
        u = jnp.dot(hn, wu_ref[:, c0:c0 + cw], preferred_element_type=f32)
        a = (_silu(g) * u).astype(bf16)
        acc = acc + jnp.dot(a, wd_ref[c0:c0 + cw, :], preferred_element_type=f32)
    o_ref[...] = acc


def _dense_ffn(h, hn, wg, wu, wd, rows, chunk):
    t, d = h.shape
    f = wg.shape[1]
    return pl.pallas_call(
        functools.partial(_dense_ffn_kernel, chunk=chunk),
        grid=(t // rows,),
        in_specs=[
            pl.BlockSpec((rows, d), lambda i: (i, 0)),
            pl.BlockSpec((rows, d), lambda i: (i, 0)),
            _resident((d, f)),
            _resident((d, f)),
            _resident((f, d)),
        ],
        out_specs=pl.BlockSpec((rows, d), lambda i: (i, 0)),
        out_shape=jax.ShapeDtypeStruct((t, d), f32),
        compiler_params=_params("arbitrary"),
        name="dense_ffn",
    )(h, hn, wg, wu, wd)


def _rope(x, cos, sin_signed, lane):
    d = x.shape[-1]
    half = ROT_DIM // 2
    partner = jnp.where(lane < half, pltpu.roll(x, d - half, 1), pltpu.roll(x, half, 1))
    return x * cos + partner * sin_signed


def _qkv_kernel(h_ref, g_q_ref, g_kv_ref, wq_ref, wk_ref, wv_ref, cos_ref, sin_ref,
                qt_ref, k_ref, vt_ref, kmean_ref):
    i = pl.program_id(1)
    rows, d = h_ref.shape[1], h_ref.shape[2]
    heads = d // HEAD_DIM
    h = h_ref[0]
    xhat = h * _rms_scale(h)
    hq = (xhat * g_q_ref[...]).astype(bf16)
    hkv = (xhat * g_kv_ref[...]).astype(bf16)
    q = jnp.dot(hq, wq_ref[...], preferred_element_type=f32)
    k = jnp.dot(hkv, wk_ref[...], preferred_element_type=f32)
    v = jnp.dot(hkv, wv_ref[...], preferred_element_type=f32)
    cos = jnp.concatenate([cos_ref[...]] * heads, axis=-1)
    sin = jnp.concatenate([sin_ref[...]] * heads, axis=-1)
    lane = lax.broadcasted_iota(jnp.int32, (rows, d), 1) % HEAD_DIM
    q = _rope(q, cos, sin, lane) * (HEAD_DIM ** -0.5 * LOG2_E)
    k = _rope(k, cos, sin, lane)
    qt_ref[0] = q.T.astype(bf16)
    k_ref[0] = k.astype(bf16)
    vt_ref[0] = v.T.astype(bf16)

    nb = kmean_ref.shape[1]
    blk_iota = lax.broadcasted_iota(jnp.int32, (nb, d), 0)

    @pl.when(i == 0)
    def _():
        kmean_ref[0] = jnp.zeros((nb, d), f32)

    km = kmean_ref[0]
    for j in range(rows // BLOCK):
        mean_j = jnp.mean(k[j * BLOCK:(j + 1) * BLOCK, :], axis=0, keepdims=True)
        km = jnp.where(blk_iota == i * (rows // BLOCK) + j, mean_j, km)
    kmean_ref[0] = km


def _qkv_proj(h, g_q, g_kv, wq, wk, wv, cos, sin, rows):
    b, s, d = h.shape
    nb = s // BLOCK
    return pl.pallas_call(
        _qkv_kernel,
        grid=(b, s // rows),
        in_specs=[
            pl.BlockSpec((1, rows, d), lambda bi, i: (bi, i, 0)),
            _resident((1, d)),
            _resident((1, d)),
            _resident((d, d)),
            _resident((d, d)),
            _resident((d, d)),
            pl.BlockSpec((rows, HEAD_DIM), lambda bi, i: (i, 0)),
            pl.BlockSpec((rows, HEAD_DIM), lambda bi, i: (i, 0)),
        ],
        out_specs=[
            pl.BlockSpec((1, d, rows), lambda bi, i: (bi, 0, i)),
            pl.BlockSpec((1, rows, d), lambda bi, i: (bi, i, 0)),
            pl.BlockSpec((1, d, rows), lambda bi, i: (bi, 0, i)),
            pl.BlockSpec((1, nb, d), lambda bi, i: (bi, 0, 0)),
        ],
        out_shape=[
            jax.ShapeDtypeStruct((b, d, s), bf16),
            jax.ShapeDtypeStruct((b, s, d), bf16),
            jax.ShapeDtypeStruct((b, d, s), bf16),
            jax.ShapeDtypeStruct((b, nb, d), f32),
        ],
        compiler_params=_params("arbitrary", "arbitrary"),
        name="qkv_proj",
    )(h, g_q, g_kv, wq, wk, wv, cos, sin)


def _moba_kernel(qt_ref, k_ref, vt_ref, kmean_ref, o_ref, *scratch, heads):
    sel_refs, m_refs, l_refs, acc_refs = (scratch[i * heads:(i + 1) * heads] for i in range(4))
    j = pl.program_id(1)
    nb = kmean_ref.shape[1]
    hd = HEAD_DIM
    blk = lax.broadcasted_iota(jnp.int32, (nb, BLOCK), 0)
    past = blk < j
    kpos = lax.broadcasted_iota(jnp.int32, (BLOCK, BLOCK), 0)
    qpos = lax.broadcasted_iota(jnp.int32, (BLOCK, BLOCK), 1)
    causal = kpos <= qpos
    start = pl.multiple_of(j * BLOCK, BLOCK)

    def q_head(h):
        return qt_ref[0, h * hd:(h + 1) * hd, :]

    ones_rows = jnp.ones((ONES_ROWS, BLOCK), bf16)

    def score(h, off):
        return jnp.dot(k_ref[0, pl.ds(off, BLOCK), h * hd:(h + 1) * hd], q_head(h), preferred_element_type=f32)

    def weighted_values(h, p, off):
        vt_ones = jnp.concatenate([vt_ref[0, h * hd:(h + 1) * hd, pl.ds(off, BLOCK)], ones_rows], axis=0)
        out = jnp.dot(vt_ones, p, preferred_element_type=f32)
        return out[:hd], out[hd:hd + 1]

    def key_block(off, probs, accumulate):
        pending = {}
        for step in range(heads + MXU_LAG):
            if step < heads:
                pending[step] = probs(step, score(step, off))
            if step >= MXU_LAG:
                h = step - MXU_LAG
                p, extra = pending.pop(h)
                accumulate(h, extra, *weighted_values(h, p, off))

    for h in range(heads):
        km = kmean_ref[0, :, h * hd:(h + 1) * hd]
        km_hi = km.astype(bf16)
        km_lo = (km - km_hi.astype(f32)).astype(bf16)
        gate = (jnp.dot(km_hi, q_head(h), preferred_element_type=f32)
                + jnp.dot(km_lo, q_head(h), preferred_element_type=f32))
        gate = jnp.where(past, gate, NEG_INF)
        rank = jnp.zeros((nb, BLOCK), jnp.int32)
        for m in range(nb):
            gm = gate[m:m + 1, :]
            rank = rank + jnp.where(gm > gate, 1, jnp.where(gm == gate, jnp.where(blk > m, 1, 0), 0))
        sel_refs[h][...] = jnp.where(past, jnp.where(rank < TOP_BLOCKS, 1.0, 0.0), 0.0)

    def own_probs(h, s):
        s = jnp.where(causal, s, NEG_INF)
        m0 = jnp.max(s, axis=0, keepdims=True)
        m_refs[h][...] = m0
        return jnp.exp2((s - m0).astype(bf16)), None

    def own_accumulate(h, _, pv, psum):
        acc_refs[h][...] = pv
        l_refs[h][...] = psum

    key_block(start, own_probs, own_accumulate)

    def body(n, carry):
        def probs(h, s):
            chosen = sel_refs[h][pl.ds(n, 1), :] > 0.0
            m_old = m_refs[h][...]
            m_new = jnp.maximum(m_old, jnp.where(chosen, jnp.max(s, axis=0, keepdims=True), NEG_INF))
            m_refs[h][...] = m_new
            p = jnp.exp2((s - jnp.where(chosen, m_new, POS_INF)).astype(bf16))
            return p, jnp.exp2(m_old - m_new)

        def accumulate(h, alpha, pv, psum):
            acc_refs[h][...] = alpha * acc_refs[h][...] + pv
            l_refs[h][...] = alpha * l_refs[h][...] + psum

        key_block(pl.multiple_of(n * BLOCK, BLOCK), probs, accumulate)
        return carry

    lax.fori_loop(0, j, body, 0)
    for h in range(heads):
        o_ref[0, :, h * hd:(h + 1) * hd] = (acc_refs[h][...] / l_refs[h][...]).T.astype(o_ref.dtype)


def _moba_attention(qt, k, vt, kmean):
    b, d, s = qt.shape
    heads = d // HEAD_DIM
    nb = s // BLOCK
    return pl.pallas_call(
        functools.partial(_moba_kernel, heads=heads),
        grid=(b, nb),
        in_specs=[
            pl.BlockSpec((1, d, BLOCK), lambda bi, j: (bi, 0, j)),
            pl.BlockSpec((1, s, d), lambda bi, j: (bi, 0, 0), pipeline_mode=pl.Buffered(1)),
            pl.BlockSpec((1, d, s), lambda bi, j: (bi, 0, 0), pipeline_mode=pl.Buffered(1)),
            pl.BlockSpec((1, nb, d), lambda bi, j: (bi, 0, 0)),
        ],
        out_specs=pl.BlockSpec((1, BLOCK, d), lambda bi, j: (bi, j, 0)),
        out_shape=jax.ShapeDtypeStruct((b, s, d), bf16),
        scratch_shapes=(
            [pltpu.VMEM((nb, BLOCK), f32)] * heads
            + [pltpu.VMEM((1, BLOCK), f32)] * (2 * heads)
            + [pltpu.VMEM((HEAD_DIM, BLOCK), f32)] * heads
        ),
        compiler_params=_params("arbitrary", "arbitrary"),
        name="moba_attn",
    )(qt, k, vt, kmean)


def _out_router_kernel(o_ref, h_ref, wo_ref, g_ref, wr_ref, h_out_ref, idx_ref, gate_ref):
    rows = o_ref.shape[0]
    n_exp = wr_ref.shape[1]
    h = h_ref[...] + jnp.dot(o_ref[...], wo_ref[...], preferred_element_type=f32)
    h_out_ref[...] = h
    hn = h * _rms_scale(h) * g_ref[...]
    wr = wr_ref[...]
    w_hi = wr.astype(bf16)
    w_lo = (wr - w_hi.astype(f32)).astype(bf16)
    x_hi = hn.astype(bf16)
    x_lo = (hn - x_hi.astype(f32)).astype(bf16)
    both = jnp.dot(x_hi, jnp.concatenate([w_hi, w_lo], axis=1), preferred_element_type=f32)
    logits = both[:, :n_exp] + both[:, n_exp:] + jnp.dot(x_lo, w_hi, preferred_element_type=f32)
    e = lax.broadcasted_iota(jnp.int32, (rows, n_exp), 1)
    v1 = jnp.max(logits, axis=-1, keepdims=True)
    i1 = jnp.min(jnp.where(logits == v1, e, n_exp), axis=-1, keepdims=True)
    rest = jnp.where(e == i1, NEG_INF, logits)
    v2 = jnp.max(rest, axis=-1, keepdims=True)
    i2 = jnp.min(jnp.where(rest == v2, e, n_exp), axis=-1, keepdims=True)
    e2 = jnp.exp(v2 - v1)
    denom = 1.0 + e2
    col = lax.broadcasted_iota(jnp.int32, (rows, TOP_K), 1)
    idx_ref[...] = jnp.where(col == 0, i1, i2)
    gate_ref[...] = jnp.where(col == 0, 1.0 / denom, e2 / denom)


def _out_router(o, h, wo, g_ffn, w_router, rows):
    t, d = h.shape
    n_exp = w_router.shape[1]
    return pl.pallas_call(
        _out_router_kernel,
        grid=(t // rows,),
        in_specs=[
            pl.BlockSpec((rows, d), lambda i: (i, 0)),
            pl.BlockSpec((rows, d), lambda i: (i, 0)),
            _resident((d, d)),
            _resident((1, d)),
            _resident((d, n_exp)),
        ],
        out_specs=[
            pl.BlockSpec((rows, d), lambda i: (i, 0)),
            pl.BlockSpec((rows, TOP_K), lambda i: (i, 0)),
            pl.BlockSpec((rows, TOP_K), lambda i: (i, 0)),
        ],
        out_shape=[
            jax.ShapeDtypeStruct((t, d), f32),
            jax.ShapeDtypeStruct((t, TOP_K), jnp.int32),
            jax.ShapeDtypeStruct((t, TOP_K), f32),
        ],
        compiler_params=_params("arbitrary"),
        name="out_router",
    )(o, h, wo, g_ffn, w_router)


def _dispatch_kernel(pos_ref, pad_lo_ref, pad_hi_ref, h_ref, g_ref, x_hbm, buf_ref, zero_ref, sem, zsem):
    i = pl.program_id(0)
    n_steps = pl.num_programs(0)
    rows, d = h_ref.shape
    sub = d // LANES
    slot = i % 2

    def staged_copies_wait(buf):
        for _ in range(TOP_K):
            pltpu.make_async_copy(buf_ref.at[buf], x_hbm.at[pl.ds(0, rows * sub), :], sem.at[buf]).wait()

    @pl.when(i == 0)
    def _():
        zero_ref[...] = jnp.zeros(zero_ref.shape, f32)
        for e in range(pad_lo_ref.shape[0]):
            lo, hi = pad_lo_ref[e], pad_hi_ref[e]

            def fill(r, carry):
                pltpu.make_async_copy(zero_ref, x_hbm.at[pl.ds(pl.multiple_of(r * sub, sub), sub), :],
                                      zsem).start()
                return carry

            def drain(r, carry):
                pltpu.make_async_copy(zero_ref, x_hbm.at[pl.ds(0, sub), :], zsem).wait()
                return carry
            lax.fori_loop(lo, hi, fill, 0)
            lax.fori_loop(lo, hi, drain, 0)

        buf_ref[1] = jnp.zeros(buf_ref.shape[1:], f32)
        tail_lo = pad_hi_ref[pad_hi_ref.shape[0] - 1]
        n_tail = (x_hbm.shape[0] // sub - tail_lo) // rows

        def tail_copy(c):
            dst = pl.multiple_of((tail_lo + c * rows) * sub, sub)
            return pltpu.make_async_copy(buf_ref.at[1], x_hbm.at[pl.ds(dst, rows * sub), :], zsem)

        def tail_fill(c, carry):
            tail_copy(c).start()
            return carry

        def tail_drain(c, carry):
            tail_copy(c).wait()
            return carry
        lax.fori_loop(0, n_tail, tail_fill, 0)
        lax.fori_loop(0, n_tail, tail_drain, 0)

    @pl.when(i >= 2)
    def _():
        staged_copies_wait(slot)

    h = h_ref[...]
    _store_token_tiles(buf_ref.at[slot], h * _rms_scale(h) * g_ref[...])
    base = i * (TOP_K * rows)

    def trip(c, carry):
        for u in range(DMA_UNROLL):
            r = c * DMA_UNROLL + u
            src = buf_ref.at[slot, pl.ds(pl.multiple_of(r * sub, sub), sub), :]
            for k in range(TOP_K):
                dst = pl.multiple_of(pos_ref[base + k * rows + r] * sub, sub)
                pltpu.make_async_copy(src, x_hbm.at[pl.ds(dst, sub), :], sem.at[slot]).start(priority=k % 2)
        return carry
    lax.fori_loop(0, rows // DMA_UNROLL, trip, 0)

    @pl.when(i == n_steps - 1)
    def _():
        staged_copies_wait(slot)

        @pl.when(n_steps >= 2)
        def _():
            staged_copies_wait(1 - slot)


def _moe_dispatch(pos_steps, pad_lo, pad_hi, h, g_ffn, n_rows, rows):
    t, d = h.shape
    sub = d // LANES
    grid_spec = pltpu.PrefetchScalarGridSpec(
        num_scalar_prefetch=3,
        grid=(t // rows,),
        in_specs=[
            pl.BlockSpec((rows, d), lambda i, ps, lo, hi: (i, 0)),
            pl.BlockSpec((1, d), lambda i, ps, lo, hi: (0, 0)),
        ],
        out_specs=pl.BlockSpec(memory_space=pl.ANY),
        scratch_shapes=[
            pltpu.VMEM((2, rows * sub, LANES), f32),
            pltpu.VMEM((sub, LANES), f32),
            pltpu.SemaphoreType.DMA((2,)),
            pltpu.SemaphoreType.DMA,
        ],
    )
    return pl.pallas_call(
        _dispatch_kernel,
        grid_spec=grid_spec,
        out_shape=jax.ShapeDtypeStruct((n_rows * sub, LANES), f32),
        compiler_params=_params("arbitrary", disable_bounds_checks=True),
        name="moe_dispatch",
    )(pos_steps, pad_lo, pad_hi, h, g_ffn)


def _moe_gemm_kernel(tile_expert_ref, tile_valid_ref, tile_src_ref, x_ref, wg_ref, wu_ref, wd_ref,
                     y_ref, xb_ref, acc_ref):
    i = pl.program_id(0)
    f = pl.program_id(1)
    rows, d = xb_ref.shape
    sub = d // LANES
    valid = tile_valid_ref[i] > 0

    @pl.when(jnp.logical_and(f == 0, valid))
    def _():
        for c in range(sub):
            xb_ref[:, c * LANES:(c + 1) * LANES] = _load_token_tiles(x_ref, rows, sub, c).astype(bf16)
        acc_ref[...] = jnp.zeros(acc_ref.shape, f32)

    @pl.when(valid)
    def _():
        xb = xb_ref[...]
        g = jnp.dot(xb, wg_ref[0].astype(bf16), preferred_element_type=f32)
        u = jnp.dot(xb, wu_ref[0].astype(bf16), preferred_element_type=f32)
        a = (_silu(g) * u).astype(bf16)
        acc_ref[...] += jnp.dot(a, wd_ref[0].astype(bf16), preferred_element_type=f32)

    @pl.when(f == pl.num_programs(1) - 1)
    def _():
        _store_token_tiles(y_ref, jnp.where(valid, acc_ref[...], 0.0))


def _moe_gemm(tile_expert, tile_valid, tile_src, x_tiles, wg, wu, wd, rows, chunk):
    n_tiles = tile_expert.shape[0]
    _, d, fe = wg.shape
    sub = d // LANES
    grid_spec = pltpu.PrefetchScalarGridSpec(
        num_scalar_prefetch=3,
        grid=(n_tiles, fe // chunk),
        in_specs=[
            pl.BlockSpec((rows * sub, LANES), lambda i, f, te, tv, ts: (ts[i], 0)),
            pl.BlockSpec((1, d, chunk), lambda i, f, te, tv, ts: (te[i], 0, f)),
            pl.BlockSpec((1, d, chunk), lambda i, f, te, tv, ts: (te[i], 0, f)),
            pl.BlockSpec((1, chunk, d), lambda i, f, te, tv, ts: (te[i], f, 0)),
        ],
        out_specs=pl.BlockSpec((rows * sub, LANES), lambda i, f, te, tv, ts: (i, 0)),
        scratch_shapes=[
            pltpu.VMEM((rows, d), bf16),
            pltpu.VMEM((rows, d), f32),
        ],
    )
    return pl.pallas_call(
        _moe_gemm_kernel,
        grid_spec=grid_spec,
        out_shape=jax.ShapeDtypeStruct((n_tiles * rows * sub, LANES), f32),
        compiler_params=_params("arbitrary", "arbitrary"),
        name="moe_gemm",
    )(tile_expert, tile_valid, tile_src, x_tiles, wg, wu, wd)


def _combine_kernel(pos_ref, h_ref, gate_ref, g_ref, y_hbm, o_ref, ybuf_ref, sem):
    i = pl.program_id(0)
    rows, d = h_ref.shape
    sub = d // LANES
    slot = i % 2

    def gather_rows(step, buf):
        base = step * (TOP_K * rows)

        def trip(c, carry):
            for u in range(DMA_UNROLL):
                r = c * DMA_UNROLL + u
                for k in range(TOP_K):
                    src = pl.multiple_of(pos_ref[base + k * rows + r] * sub, sub)
                    pltpu.make_async_copy(y_hbm.at[pl.ds(src, sub), :],
                                          ybuf_ref.at[buf, k, pl.ds(pl.multiple_of(r * sub, sub), sub), :],
                                          sem.at[buf]).start(priority=k % 2)
            return carry
        lax.fori_loop(0, rows // DMA_UNROLL, trip, 0)

    @pl.when(i == 0)
    def _():
        gather_rows(0, 0)

    @pl.when(i + 1 < pl.num_programs(0))
    def _():
        gather_rows(i + 1, 1 - slot)

    for k in range(TOP_K):
        pltpu.make_async_copy(y_hbm.at[pl.ds(0, rows * sub), :], ybuf_ref.at[slot, k], sem.at[slot]).wait()
    gate = gate_ref[...]
    chunks = []
    sumsq = jnp.zeros((rows, 1), f32)
    for c in range(sub):
        hc = h_ref[:, c * LANES:(c + 1) * LANES]
        for k in range(TOP_K):
            hc = hc + gate[:, k:k + 1] * _load_token_tiles(ybuf_ref.at[slot, k], rows, sub, c)
        sumsq = sumsq + jnp.sum(hc * hc, axis=-1, keepdims=True)
        chunks.append(hc)
    scale = lax.rsqrt(sumsq / d + EPS)
    for c in range(sub):
        o_ref[:, c * LANES:(c + 1) * LANES] = chunks[c] * scale * g_ref[:, c * LANES:(c + 1) * LANES]


def _moe_combine(pos, h, gates, g_final, y_tiles, rows):
    t, d = h.shape
    sub = d // LANES
    grid_spec = pltpu.PrefetchScalarGridSpec(
        num_scalar_prefetch=1,
        grid=(t // rows,),
        in_specs=[
            pl.BlockSpec((rows, d), lambda i, ps: (i, 0)),
            pl.BlockSpec((rows, TOP_K), lambda i, ps: (i, 0)),
            pl.BlockSpec((1, d), lambda i, ps: (0, 0)),
            pl.BlockSpec(memory_space=pl.ANY),
        ],
        out_specs=pl.BlockSpec((rows, d), lambda i, ps: (i, 0)),
        scratch_shapes=[pltpu.VMEM((2, TOP_K, rows * sub, LANES), f32), pltpu.SemaphoreType.DMA((2,))],
    )
    return pl.pallas_call(
        _combine_kernel,
        grid_spec=grid_spec,
        out_shape=jax.ShapeDtypeStruct((t, d), f32),
        compiler_params=_params("arbitrary", disable_bounds_checks=True),
        name="moe_combine",
    )(pos, h, gates, g_final, y_tiles)


def _routing_plan(idx, n_exp, rows, disp_rows, comb_rows):
    t = idx.shape[0]
    pairs = t * TOP_K
    n_tiles = -(-pairs // rows) + n_exp
    e_flat = idx.T.reshape(pairs)
    onehot = (e_flat[:, None] == jnp.arange(n_exp, dtype=jnp.int32)[None, :]).astype(jnp.int32)
    csum = jnp.cumsum(onehot, axis=0)
    counts = csum[-1]
    rank = jnp.sum(onehot * csum, axis=1) - 1
    padded = ((counts + rows - 1) // rows) * rows
    pend = jnp.cumsum(padded)
    poff = pend - padded
    pos = poff[e_flat] + rank
    tile_start = jnp.arange(n_tiles, dtype=jnp.int32) * rows
    tile_expert = jnp.minimum(jnp.sum((tile_start[:, None] >= pend[None, :]).astype(jnp.int32), axis=1),
                              n_exp - 1)
    tile_valid = (tile_start < pend[-1]).astype(jnp.int32)
    last_valid = jnp.maximum(jnp.sum(tile_valid) - 1, 0)
    tile_expert = jnp.where(tile_valid > 0, tile_expert, tile_expert[last_valid])
    tile_src = jnp.minimum(jnp.arange(n_tiles, dtype=jnp.int32), last_valid)

    def by_step(step_rows):
        return pos.reshape(TOP_K, t // step_rows, step_rows).transpose(1, 0, 2).reshape(pairs)

    return (tile_expert, tile_valid, tile_src, poff + counts, pend, n_tiles * rows,
            by_step(disp_rows), by_step(comb_rows))


def _rope_tables(s):
    half = ROT_DIM // 2
    inv = ROPE_THETA ** (-jnp.arange(half, dtype=f32) * 2.0 / ROT_DIM)
    ang = jnp.arange(s).astype(f32)[:, None] * inv[None, :]
    cos, sin = jnp.cos(ang), jnp.sin(ang)
    pad = HEAD_DIM - ROT_DIM
    cos_t = jnp.concatenate([cos, cos, jnp.ones((s, pad), f32)], axis=-1)
    sin_t = jnp.concatenate([-sin, sin, jnp.zeros((s, pad), f32)], axis=-1)
    return cos_t, sin_t


def kernel(x, attn_norm, ffn_norm, pool_w, pool_scale, wq, wo, kv_norm, wk, wv, ffn_w_gate, ffn_w_up,
           ffn_w_down, router_w, moe_w_gate, moe_w_up, moe_w_down, final_norm):
    b, s, d = x.shape
    t = b * s
    n_exp = router_w.shape[-1]
    assert attn_norm.shape[0] == 2 and pool_w.shape[0] == 1 and wq.shape[0] == 1 and router_w.shape[0] == 1
    assert s % 512 == 0 and d % (HEAD_DIM * 2) == 0

    tok_rows = 1024
    moe_chunk = 512
    moe_rows = 1024
    comb_rows = 512
    assert moe_w_gate.shape[-1] % moe_chunk == 0 and d % (8 * LANES) == 0 and moe_rows % tok_rows == 0

    row = lambda v: v.reshape(1, -1)

    h1, hn1 = _pool_mixer(x, row(attn_norm[0]), pool_w[0].astype(bf16), row(pool_scale[0]),
                          row(ffn_norm[0]), tok_rows)
    h2 = _dense_ffn(h1.reshape(t, d), hn1.reshape(t, d), ffn_w_gate[0].astype(bf16),
                    ffn_w_up[0].astype(bf16), ffn_w_down[0].astype(bf16), tok_rows, 512)

    cos_t, sin_t = _rope_tables(s)
    qt, k, vt, kmean = _qkv_proj(h2.reshape(b, s, d), row(attn_norm[1]), row(kv_norm), wq[0].astype(bf16),
                                 wk.astype(bf16), wv.astype(bf16), cos_t, sin_t, tok_rows)
    o = _moba_attention(qt, k, vt, kmean)

    h3, idx, gates = _out_router(o.reshape(t, d), h2, wo[0].astype(bf16), row(ffn_norm[1]), router_w[0],
                                 tok_rows)

    (tile_expert, tile_valid, tile_src, pad_lo, pad_hi, n_rows, pos_disp,
     pos_comb) = _routing_plan(idx, n_exp, moe_rows, tok_rows, comb_rows)
    x_tiles = _moe_dispatch(pos_disp, pad_lo, pad_hi, h3, row(ffn_norm[1]), n_rows, tok_rows)
    y_tiles = _moe_gemm(tile_expert, tile_valid, tile_src, x_tiles, moe_w_gate[0], moe_w_up[0],
                        moe_w_down[0], moe_rows, moe_chunk)
    out = _moe_combine(pos_comb, h3, gates, row(final_norm), y_tiles, comb_rows)
    return out.reshape(b, s, d)
```

```python
import functools

import jax
import jax.numpy as jnp
from jax import lax
from jax.experimental import pallas as pl
from jax.experimental.pallas import tpu as pltpu

EPS = 1e-6
POOL_WINDOWS = (2, 4, 8, 16)
POOL_HALO = 16
LANES = 128
HEAD_DIM = 128
BLOCK = 256
TOP_BLOCKS = 3
ROT_DIM = HEAD_DIM // 4
ROPE_THETA = 500000.0
TOP_K = 2

VMEM_LIMIT = 56 * 1024 * 1024
DMA_UNROLL = 8
NEG_INF = float("-inf")
POS_INF = float("inf")
MXU_LAG = 8
ONES_ROWS = 16
LOG2_E = 1.4426950408889634

f32 = jnp.float32
bf16 = jnp.bfloat16


def _resident(shape):
    n = len(shape)
    return pl.BlockSpec(shape, lambda *_: (0,) * n, pipeline_mode=pl.Buffered(1))


def _rms_scale(xf):
    return lax.rsqrt(jnp.mean(xf * xf, axis=-1, keepdims=True) + EPS)


def _silu(g):
    return g * (1.0 / (1.0 + jnp.exp(-g)))


def _store_token_tiles(ref, x):
    rows, d = x.shape
    sub = d // LANES
    for c in range(sub):
        ref[pl.ds(c, rows, stride=sub), :] = x[:, c * LANES:(c + 1) * LANES]


def _load_token_tiles(ref, rows, sub, chunk):
    return ref[pl.ds(chunk, rows, stride=sub), :]


def _params(*sem, **kw):
    return pltpu.CompilerParams(dimension_semantics=sem, vmem_limit_bytes=VMEM_LIMIT, **kw)


def _pool_kernel(x_ref, halo_ref, g_attn_ref, w_ref, scale_ref, g_ffn_ref, h_ref, hn_ref, ext_ref):
    i = pl.program_id(1)
    rows = x_ref.shape[1]
    d = x_ref.shape[2]
    c = d // len(POOL_WINDOWS)
    g_attn = g_attn_ref[...]
    x = x_ref[0]
    xn = x * _rms_scale(x) * g_attn
    halo = halo_ref[0]
    halo = halo * _rms_scale(halo) * g_attn
    ext_ref[0:POOL_HALO, :] = jnp.where(i == 0, 0.0, halo)
    ext_ref[POOL_HALO:, :] = xn
    t = i * rows + lax.broadcasted_iota(jnp.int32, (rows, 1), 0)
    ys = []
    for g, w in enumerate(POOL_WINDOWS):
        c0 = g * c
        s = ext_ref[:, c0:c0 + c]
        shift = 1
        while shift < w:
            s = s + pltpu.roll(s, shift, 0)
            shift *= 2
        cnt = jnp.minimum(t + 1, w).astype(f32)
        pooled = s[POOL_HALO:, :] / cnt - xn[:, c0:c0 + c]
        ys.append(jnp.dot(pooled.astype(bf16), w_ref[g], preferred_element_type=f32))
    y = jnp.concatenate(ys, axis=-1) * scale_ref[...]
    h = x + y
    h_ref[0] = h
    hn_ref[0] = (h * _rms_scale(h) * g_ffn_ref[...]).astype(bf16)


def _pool_mixer(x, g_attn, w_pool, scale, g_ffn, rows):
    b, s, d = x.shape
    c = d // len(POOL_WINDOWS)
    hb = rows // POOL_HALO
    return pl.pallas_call(
        _pool_kernel,
        grid=(b, s // rows),
        in_specs=[
            pl.BlockSpec((1, rows, d), lambda bi, i: (bi, i, 0)),
            pl.BlockSpec((1, POOL_HALO, d), lambda bi, i: (bi, jnp.maximum(i * hb - 1, 0), 0)),
            _resident((1, d)),
            _resident((len(POOL_WINDOWS), c, c)),
            _resident((1, d)),
            _resident((1, d)),
        ],
        out_specs=[
            pl.BlockSpec((1, rows, d), lambda bi, i: (bi, i, 0)),
            pl.BlockSpec((1, rows, d), lambda bi, i: (bi, i, 0)),
        ],
        out_shape=[jax.ShapeDtypeStruct((b, s, d), f32), jax.ShapeDtypeStruct((b, s, d), bf16)],
        scratch_shapes=[pltpu.VMEM((rows + POOL_HALO, d), f32)],
        compiler_params=_params("arbitrary", "arbitrary"),
        name="pool_mixer",
    )(x, x, g_attn, w_pool, scale, g_ffn)


def _ffn_chunks(f, size):
    out, c0 = [], 0
    while c0 < f:
        out.append((c0, min(size, f - c0)))
        c0 += size
    return out


def _dense_ffn_kernel(h_ref, hn_ref, wg_ref, wu_ref, wd_ref, o_ref, *, chunk):
    hn = hn_ref[...]
    acc = h_ref[...]
    for c0, cw in _ffn_chunks(wg_ref.shape[1], chunk):
        g = jnp.dot(hn, wg_ref[:, c0:c0 + cw], preferred_element_type=f32)
        u = jnp.dot(hn, wu_ref[:, c0:c0 + cw], preferred_element_type=f32)
        a = (_silu(g) * u).astype(bf16)
        acc = acc + jnp.dot(a, wd_ref[c0:c0 + cw, :], preferred_element_type=f32)
    o_ref[...] = acc


def _dense_ffn(h, hn, wg, wu, wd, rows, chunk):
    t, d = h.shape
    f = wg.shape[1]
    return pl.pallas_call(
        functools.partial(_dense_ffn_kernel, chunk=chunk),
        grid=(t // rows,),
        in_specs=[
            pl.BlockSpec((rows, d), lambda i: (i, 0)),
            pl.BlockSpec((rows, d), lambda i: (i, 0)),
            _resident((d, f)),
            _resident((d, f)),
            _resident((f, d)),
        ],
        out_specs=pl.BlockSpec((rows, d), lambda i: (i, 0)),
        out_shape=jax.ShapeDtypeStruct((t, d), f32),
        compiler_params=_params("arbitrary"),
        name="dense_ffn",
    )(h, hn, wg, wu, wd)


def _rope(x, cos, sin_signed, lane):
    d = x.shape[-1]
    half = ROT_DIM // 2
    partner = jnp.where(lane < half, pltpu.roll(x, d - half, 1), pltpu.roll(x, half, 1))
    return x * cos + partner * sin_signed


def _qkv_kernel(h_ref, g_q_ref, g_kv_ref, wq_ref, wk_ref, wv_ref, cos_ref, sin_ref,
                qt_ref, k_ref, vt_ref, kmean_ref):
    i = pl.program_id(1)
    rows, d = h_ref.shape[1], h_ref.shape[2]
    heads = d // HEAD_DIM
    h = h_ref[0]
    xhat = h * _rms_scale(h)
    hq = (xhat * g_q_ref[...]).astype(bf16)
    hkv = (xhat * g_kv_ref[...]).astype(bf16)
    q = jnp.dot(hq, wq_ref[...], preferred_element_type=f32)
    k = jnp.dot(hkv, wk_ref[...], preferred_element_type=f32)
    v = jnp.dot(hkv, wv_ref[...], preferred_element_type=f32)
    cos = jnp.concatenate([cos_ref[...]] * heads, axis=-1)
    sin = jnp.concatenate([sin_ref[...]] * heads, axis=-1)
    lane = lax.broadcasted_iota(jnp.int32, (rows, d), 1) % HEAD_DIM
    q = _rope(q, cos, sin, lane) * (HEAD_DIM ** -0.5 * LOG2_E)
    k = _rope(k, cos, sin, lane)
    qt_ref[0] = q.T.astype(bf16)
    k_ref[0] = k.astype(bf16)
    vt_ref[0] = v.T.astype(bf16)

    nb = kmean_ref.shape[1]
    blk_iota = lax.broadcasted_iota(jnp.int32, (nb, d), 0)

    @pl.when(i == 0)
    def _():
        kmean_ref[0] = jnp.zeros((nb, d), f32)

    km = kmean_ref[0]
    for j in range(rows // BLOCK):
        mean_j = jnp.mean(k[j * BLOCK:(j + 1) * BLOCK, :], axis=0, keepdims=True)
        km = jnp.where(blk_iota == i * (rows // BLOCK) + j, mean_j, km)
    kmean_ref[0] = km


def _qkv_proj(h, g_q, g_kv, wq, wk, wv, cos, sin, rows):
    b, s, d = h.shape
    nb = s // BLOCK
    return pl.pallas_call(
        _qkv_kernel,
        grid=(b, s // rows),
        in_specs=[
            pl.BlockSpec((1, rows, d), lambda bi, i: (bi, i, 0)),
            _resident((1, d)),
            _resident((1, d)),
            _resident((d, d)),
            _resident((d, d)),
            _resident((d, d)),
            pl.BlockSpec((rows, HEAD_DIM), lambda bi, i: (i, 0)),
            pl.BlockSpec((rows, HEAD_DIM), lambda bi, i: (i, 0)),
        ],
        out_specs=[
            pl.BlockSpec((1, d, rows), lambda bi, i: (bi, 0, i)),
            pl.BlockSpec((1, rows, d), lambda bi, i: (bi, i, 0)),
            pl.BlockSpec((1, d, rows), lambda bi, i: (bi, 0, i)),
            pl.BlockSpec((1, nb, d), lambda bi, i: (bi, 0, 0)),
        ],
        out_shape=[
            jax.ShapeDtypeStruct((b, d, s), bf16),
            jax.ShapeDtypeStruct((b, s, d), bf16),
            jax.ShapeDtypeStruct((b, d, s), bf16),
            jax.ShapeDtypeStruct((b, nb, d), f32),
        ],
        compiler_params=_params("arbitrary", "arbitrary"),
        name="qkv_proj",
    )(h, g_q, g_kv, wq, wk, wv, cos, sin)


def _moba_kernel(qt_ref, k_ref, vt_ref, kmean_ref, o_ref, *scratch, heads):
    sel_refs, m_refs, l_refs, acc_refs = (scratch[i * heads:(i + 1) * heads] for i in range(4))
    j = pl.program_id(1)
    nb = kmean_ref.shape[1]
    hd = HEAD_DIM
    blk = lax.broadcasted_iota(jnp.int32, (nb, BLOCK), 0)
    past = blk < j
    kpos = lax.broadcasted_iota(jnp.int32, (BLOCK, BLOCK), 0)
    qpos = lax.broadcasted_iota(jnp.int32, (BLOCK, BLOCK), 1)
    causal = kpos <= qpos
    start = pl.multiple_of(j * BLOCK, BLOCK)

    def q_head(h):
        return qt_ref[0, h * hd:(h + 1) * hd, :]

    ones_rows = jnp.ones((ONES_ROWS, BLOCK), bf16)

    def score(h, off):
        return jnp.dot(k_ref[0, pl.ds(off, BLOCK), h * hd:(h + 1) * hd], q_head(h), preferred_element_type=f32)

    def weighted_values(h, p, off):
        vt_ones = jnp.concatenate([vt_ref[0, h * hd:(h + 1) * hd, pl.ds(off, BLOCK)], ones_rows], axis=0)
        out = jnp.dot(vt_ones, p, preferred_element_type=f32)
        return out[:hd], out[hd:hd + 1]

    def key_block(off, probs, accumulate):
        pending = {}
        for step in range(heads + MXU_LAG):
            if step < heads:
                pending[step] = probs(step, score(step, off))
            if step >= MXU_LAG:
                h = step - MXU_LAG
                p, extra = pending.pop(h)
                accumulate(h, extra, *weighted_values(h, p, off))

    for h in range(heads):
        km = kmean_ref[0, :, h * hd:(h + 1) * hd]
        km_hi = km.astype(bf16)
        km_lo = (km - km_hi.astype(f32)).astype(bf16)
        gate = (jnp.dot(km_hi, q_head(h), preferred_element_type=f32)
                + jnp.dot(km_lo, q_head(h), preferred_element_type=f32))
        gate = jnp.where(past, gate, NEG_INF)
        rank = jnp.zeros((nb, BLOCK), jnp.int32)
        for m in range(nb):
            gm = gate[m:m + 1, :]
            rank = rank + jnp.where(gm > gate, 1, jnp.where(gm == gate, jnp.where(blk > m, 1, 0), 0))
        sel_refs[h][...] = jnp.where(past, jnp.where(rank < TOP_BLOCKS, 1.0, 0.0), 0.0)

    def own_probs(h, s):
        s = jnp.where(causal, s, NEG_INF)
        m0 = jnp.max(s, axis=0, keepdims=True)
        m_refs[h][...] = m0
        return jnp.exp2((s - m0).astype(bf16)), None

    def own_accumulate(h, _, pv, psum):
        acc_refs[h][...] = pv
        l_refs[h][...] = psum

    key_block(start, own_probs, own_accumulate)

    def body(n, carry):
        def probs(h, s):
            chosen = sel_refs[h][pl.ds(n, 1), :] > 0.0
            m_old = m_refs[h][...]
            m_new = jnp.maximum(m_old, jnp.where(chosen, jnp.max(s, axis=0, keepdims=True), NEG_INF))
            m_refs[h][...] = m_new
            p = jnp.exp2((s - jnp.where(chosen, m_new, POS_INF)).astype(bf16))
            return p, jnp.exp2(m_old - m_new)

        def accumulate(h, alpha, pv, psum):
            acc_refs[h][...] = alpha * acc_refs[h][...] + pv
            l_refs[h][...] = alpha * l_refs[h][...] + psum

        key_block(pl.multiple_of(n * BLOCK, BLOCK), probs, accumulate)
        return carry

    lax.fori_loop(0, j, body, 0)
    for h in range(heads):
        o_ref[0, :, h * hd:(h + 1) * hd] = (acc_refs[h][...] / l_refs[h][...]).T.astype(o_ref.dtype)


def _moba_attention(qt, k, vt, kmean):
    b, d, s = qt.shape
    heads = d // HEAD_DIM
    nb = s // BLOCK
    return pl.pallas_call(
        functools.partial(_moba_kernel, heads=heads),
        grid=(b, nb),
        in_specs=[
            pl.BlockSpec((1, d, BLOCK), lambda bi, j: (bi, 0, j)),
            pl.BlockSpec((1, s, d), lambda bi, j: (bi, 0, 0), pipeline_mode=pl.Buffered(1)),
            pl.BlockSpec((1, d, s), lambda bi, j: (bi, 0, 0), pipeline_mode=pl.Buffered(1)),
            pl.BlockSpec((1, nb, d), lambda bi, j: (bi, 0, 0)),
        ],
        out_specs=pl.BlockSpec((1, BLOCK, d), lambda bi, j: (bi, j, 0)),
        out_shape=jax.ShapeDtypeStruct((b, s, d), bf16),
        scratch_shapes=(
            [pltpu.VMEM((nb, BLOCK), f32)] * heads
            + [pltpu.VMEM((1, BLOCK), f32)] * (2 * heads)
            + [pltpu.VMEM((HEAD_DIM, BLOCK), f32)] * heads
        ),
        compiler_params=_params("arbitrary", "arbitrary"),
        name="moba_attn",
    )(qt, k, vt, kmean)


def _out_router_kernel(o_ref, h_ref, wo_ref, g_ref, wr_ref, h_out_ref, idx_ref, gate_ref):
    rows = o_ref.shape[0]
    n_exp = wr_ref.shape[1]
    h = h_ref[...] + jnp.dot(o_ref[...], wo_ref[...], preferred_element_type=f32)
    h_out_ref[...] = h
    hn = h * _rms_scale(h) * g_ref[...]
    wr = wr_ref[...]
    w_hi = wr.astype(bf16)
    w_lo = (wr - w_hi.astype(f32)).astype(bf16)
    x_hi = hn.astype(bf16)
    x_lo = (hn - x_hi.astype(f32)).astype(bf16)
    both = jnp.dot(x_hi, jnp.concatenate([w_hi, w_lo], axis=1), preferred_element_type=f32)
    logits = both[:, :n_exp] + both[:, n_exp:] + jnp.dot(x_lo, w_hi, preferred_element_type=f32)
    e = lax.broadcasted_iota(jnp.int32, (rows, n_exp), 1)
    v1 = jnp.max(logits, axis=-1, keepdims=True)
    i1 = jnp.min(jnp.where(logits == v1, e, n_exp), axis=-1, keepdims=True)
    rest = jnp.where(e == i1, NEG_INF, logits)
    v2 = jnp.max(rest, axis=-1, keepdims=True)
    i2 = jnp.min(jnp.where(rest == v2, e, n_exp), axis=-1, keepdims=True)
    e2 = jnp.exp(v2 - v1)
    denom = 1.0 + e2
    col = lax.broadcasted_iota(jnp.int32, (rows, TOP_K), 1)
    idx_ref[...] = jnp.where(col == 0, i1, i2)
    gate_ref[...] = jnp.where(col == 0, 1.0 / denom, e2 / denom)


def _out_router(o, h, wo, g_ffn, w_router, rows):
    t, d = h.shape
    n_exp = w_router.shape[1]
    return pl.pallas_call(
        _out_router_kernel,
        grid=(t // rows,),
        in_specs=[
            pl.BlockSpec((rows, d), lambda i: (i, 0)),
            pl.BlockSpec((rows, d), lambda i: (i, 0)),
            _resident((d, d)),
            _resident((1, d)),
            _resident((d, n_exp)),
        ],
        out_specs=[
            pl.BlockSpec((rows, d), lambda i: (i, 0)),
            pl.BlockSpec((rows, TOP_K), lambda i: (i, 0)),
            pl.BlockSpec((rows, TOP_K), lambda i: (i, 0)),
        ],
        out_shape=[
            jax.ShapeDtypeStruct((t, d), f32),
            jax.ShapeDtypeStruct((t, TOP_K), jnp.int32),
            jax.ShapeDtypeStruct((t, TOP_K), f32),
        ],
        compiler_params=_params("arbitrary"),
        name="out_router",
    )(o, h, wo, g_ffn, w_router)


def _dispatch_kernel(pos_ref, pad_lo_ref, pad_hi_ref, h_ref, g_ref, x_hbm, buf_ref, zero_ref, sem, zsem):
    i = pl.program_id(0)
    n_steps = pl.num_programs(0)
    rows, d = h_ref.shape
    sub = d // LANES
    slot = i % 2

    def staged_copies_wait(buf):
        for _ in range(TOP_K):
            pltpu.make_async_copy(buf_ref.at[buf], x_hbm.at[pl.ds(0, rows * sub), :], sem.at[buf]).wait()

    @pl.when(i == 0)
    def _():
        zero_ref[...] = jnp.zeros(zero_ref.shape, f32)
        for e in range(pad_lo_ref.shape[0]):
            lo, hi = pad_lo_ref[e], pad_hi_ref[e]

            def fill(r, carry):
                pltpu.make_async_copy(zero_ref, x_hbm.at[pl.ds(pl.multiple_of(r * sub, sub), sub), :],
                                      zsem).start()
                return carry

            def drain(r, carry):
                pltpu.make_async_copy(zero_ref, x_hbm.at[pl.ds(0, sub), :], zsem).wait()
                return carry
            lax.fori_loop(lo, hi, fill, 0)
            lax.fori_loop(lo, hi, drain, 0)

        buf_ref[1] = jnp.zeros(buf_ref.shape[1:], f32)
        tail_lo = pad_hi_ref[pad_hi_ref.shape[0] - 1]
        n_tail = (x_hbm.shape[0] // sub - tail_lo) // rows

        def tail_copy(c):
            dst = pl.multiple_of((tail_lo + c * rows) * sub, sub)
            return pltpu.make_async_copy(buf_ref.at[1], x_hbm.at[pl.ds(dst, rows * sub), :], zsem)

        def tail_fill(c, carry):
            tail_copy(c).start()
            return carry

        def tail_drain(c, carry):
            tail_copy(c).wait()
            return carry
        lax.fori_loop(0, n_tail, tail_fill, 0)
        lax.fori_loop(0, n_tail, tail_drain, 0)

    @pl.when(i >= 2)
    def _():
        staged_copies_wait(slot)

    h = h_ref[...]
    _store_token_tiles(buf_ref.at[slot], h * _rms_scale(h) * g_ref[...])
    base = i * (TOP_K * rows)

    def trip(c, carry):
        for u in range(DMA_UNROLL):
            r = c * DMA_UNROLL + u
            src = buf_ref.at[slot, pl.ds(pl.multiple_of(r * sub, sub), sub), :]
            for k in range(TOP_K):
                dst = pl.multiple_of(pos_ref[base + k * rows + r] * sub, sub)
                pltpu.make_async_copy(src, x_hbm.at[pl.ds(dst, sub), :], sem.at[slot]).start(priority=k % 2)
        return carry
    lax.fori_loop(0, rows // DMA_UNROLL, trip, 0)

    @pl.when(i == n_steps - 1)
    def _():
        staged_copies_wait(slot)

        @pl.when(n_steps >= 2)
        def _():
            staged_copies_wait(1 - slot)


def _moe_dispatch(pos_steps, pad_lo, pad_hi, h, g_ffn, n_rows, rows):
    t, d = h.shape
    sub = d // LANES
    grid_spec = pltpu.PrefetchScalarGridSpec(
        num_scalar_prefetch=3,
        grid=(t // rows,),
        in_specs=[
            pl.BlockSpec((rows, d), lambda i, ps, lo, hi: (i, 0)),
            pl.BlockSpec((1, d), lambda i, ps, lo, hi: (0, 0)),
        ],
        out_specs=pl.BlockSpec(memory_space=pl.ANY),
        scratch_shapes=[
            pltpu.VMEM((2, rows * sub, LANES), f32),
            pltpu.VMEM((sub, LANES), f32),
            pltpu.SemaphoreType.DMA((2,)),
            pltpu.SemaphoreType.DMA,
        ],
    )
    return pl.pallas_call(
        _dispatch_kernel,
        grid_spec=grid_spec,
        out_shape=jax.ShapeDtypeStruct((n_rows * sub, LANES), f32),
        compiler_params=_params("arbitrary", disable_bounds_checks=True),
        name="moe_dispatch",
    )(pos_steps, pad_lo, pad_hi, h, g_ffn)


def _moe_gemm_kernel(tile_expert_ref, tile_valid_ref, tile_src_ref, x_ref, wg_ref, wu_ref, wd_ref,
                     y_ref, xb_ref, acc_ref):
    i = pl.program_id(0)
    f = pl.program_id(1)
    rows, d = xb_ref.shape
    sub = d // LANES
    valid = tile_valid_ref[i] > 0

    def swiglu_chunk():
        xb = xb_ref[...]
        g = jnp.dot(xb, wg_ref[0].astype(bf16), preferred_element_type=f32)
        u = jnp.dot(xb, wu_ref[0].astype(bf16), preferred_element_type=f32)
        a = (_silu(g) * u).astype(bf16)
        return jnp.dot(a, wd_ref[0].astype(bf16), preferred_element_type=f32)

    @pl.when(jnp.logical_and(f == 0, valid))
    def _():
        for c in range(sub):
            xb_ref[:, c * LANES:(c + 1) * LANES] = _load_token_tiles(x_ref, rows, sub, c).astype(bf16)
        acc_ref[...] = swiglu_chunk()

    last = pl.num_programs(1) - 1

    @pl.when(jnp.logical_and(jnp.logical_and(f > 0, f < last), valid))
    def _():
        acc_ref[...] += swiglu_chunk()

    @pl.when(jnp.logical_and(f == last, valid))
    def _():
        _store_token_tiles(y_ref, acc_ref[...] + swiglu_chunk())

    @pl.when(jnp.logical_and(f == last, jnp.logical_not(valid)))
    def _():
        y_ref[...] = jnp.zeros(y_ref.shape, f32)


def _moe_gemm(tile_expert, tile_valid, tile_src, x_tiles, wg, wu, wd, rows, chunk):
    n_tiles = tile_expert.shape[0]
    _, d, fe = wg.shape
    sub = d // LANES
    grid_spec = pltpu.PrefetchScalarGridSpec(
        num_scalar_prefetch=3,
        grid=(n_tiles, fe // chunk),
        in_specs=[
            pl.BlockSpec((rows * sub, LANES), lambda i, f, te, tv, ts: (ts[i], 0)),
            pl.BlockSpec((1, d, chunk), lambda i, f, te, tv, ts: (te[i], 0, f)),
            pl.BlockSpec((1, d, chunk), lambda i, f, te, tv, ts: (te[i], 0, f)),
            pl.BlockSpec((1, chunk, d), lambda i, f, te, tv, ts: (te[i], f, 0)),
        ],
        out_specs=pl.BlockSpec((rows * sub, LANES), lambda i, f, te, tv, ts: (i, 0)),
        scratch_shapes=[
            pltpu.VMEM((rows, d), bf16),
            pltpu.VMEM((rows, d), f32),
        ],
    )
    return pl.pallas_call(
        _moe_gemm_kernel,
        grid_spec=grid_spec,
        out_shape=jax.ShapeDtypeStruct((n_tiles * rows * sub, LANES), f32),
        compiler_params=_params("arbitrary", "arbitrary"),
        name="moe_gemm",
    )(tile_expert, tile_valid, tile_src, x_tiles, wg, wu, wd)


def _combine_kernel(pos_ref, h_ref, gate_ref, g_ref, y_hbm, o_ref, ybuf_ref, sem):
    i = pl.program_id(0)
    rows, d = h_ref.shape
    sub = d // LANES
    slot = i % 2

    def gather_rows(step, buf):
        base = step * (TOP_K * rows)

        def trip(c, carry):
            for u in range(DMA_UNROLL):
                r = c * DMA_UNROLL + u
                for k in range(TOP_K):
                    src = pl.multiple_of(pos_ref[base + k * rows + r] * sub, sub)
                    pltpu.make_async_copy(y_hbm.at[pl.ds(src, sub), :],
                                          ybuf_ref.at[buf, k, pl.ds(pl.multiple_of(r * sub, sub), sub), :],
                                          sem.at[buf]).start(priority=k % 2)
            return carry
        lax.fori_loop(0, rows // DMA_UNROLL, trip, 0)

    @pl.when(i == 0)
    def _():
        gather_rows(0, 0)

    @pl.when(i + 1 < pl.num_programs(0))
    def _():
        gather_rows(i + 1, 1 - slot)

    for k in range(TOP_K):
        pltpu.make_async_copy(y_hbm.at[pl.ds(0, rows * sub), :], ybuf_ref.at[slot, k], sem.at[slot]).wait()
    gate = gate_ref[...]
    chunks = []
    sumsq = jnp.zeros((rows, 1), f32)
    for c in range(sub):
        hc = h_ref[:, c * LANES:(c + 1) * LANES]
        for k in range(TOP_K):
            hc = hc + gate[:, k:k + 1] * _load_token_tiles(ybuf_ref.at[slot, k], rows, sub, c)
        sumsq = sumsq + jnp.sum(hc * hc, axis=-1, keepdims=True)
        chunks.append(hc)
    scale = lax.rsqrt(sumsq / d + EPS)
    for c in range(sub):
        o_ref[:, c * LANES:(c + 1) * LANES] = chunks[c] * scale * g_ref[:, c * LANES:(c + 1) * LANES]


def _moe_combine(pos, h, gates, g_final, y_tiles, rows):
    t, d = h.shape
    sub = d // LANES
    grid_spec = pltpu.PrefetchScalarGridSpec(
        num_scalar_prefetch=1,
        grid=(t // rows,),
        in_specs=[
            pl.BlockSpec((rows, d), lambda i, ps: (i, 0)),
            pl.BlockSpec((rows, TOP_K), lambda i, ps: (i, 0)),
            pl.BlockSpec((1, d), lambda i, ps: (0, 0)),
            pl.BlockSpec(memory_space=pl.ANY),
        ],
        out_specs=pl.BlockSpec((rows, d), lambda i, ps: (i, 0)),
        scratch_shapes=[pltpu.VMEM((2, TOP_K, rows * sub, LANES), f32), pltpu.SemaphoreType.DMA((2,))],
    )
    return pl.pallas_call(
        _combine_kernel,
        grid_spec=grid_spec,
        out_shape=jax.ShapeDtypeStruct((t, d), f32),
        compiler_params=_params("arbitrary", disable_bounds_checks=True),
        name="moe_combine",
    )(pos, h, gates, g_final, y_tiles)


def _routing_plan(idx, n_exp, rows, disp_rows, comb_rows):
    t = idx.shape[0]
    pairs = t * TOP_K
    n_tiles = -(-pairs // rows) + n_exp
    e_flat = idx.T.reshape(pairs)
    onehot = (e_flat[:, None] == jnp.arange(n_exp, dtype=jnp.int32)[None, :]).astype(jnp.int32)
    csum = jnp.cumsum(onehot, axis=0)
    counts = csum[-1]
    rank = jnp.sum(onehot * csum, axis=1) - 1
    padded = ((counts + rows - 1) // rows) * rows
    pend = jnp.cumsum(padded)
    poff = pend - padded
    pos = poff[e_flat] + rank
    tile_start = jnp.arange(n_tiles, dtype=jnp.int32) * rows
    tile_expert = jnp.minimum(jnp.sum((tile_start[:, None] >= pend[None, :]).astype(jnp.int32), axis=1),
                              n_exp - 1)
    tile_valid = (tile_start < pend[-1]).astype(jnp.int32)
    last_valid = jnp.maximum(jnp.sum(tile_valid) - 1, 0)
    tile_expert = jnp.where(tile_valid > 0, tile_expert, tile_expert[last_valid])
    tile_src = jnp.minimum(jnp.arange(n_tiles, dtype=jnp.int32), last_valid)

    def by_step(step_rows):
        return pos.reshape(TOP_K, t // step_rows, step_rows).transpose(1, 0, 2).reshape(pairs)

    return (tile_expert, tile_valid, tile_src, poff + counts, pend, n_tiles * rows,
            by_step(disp_rows), by_step(comb_rows))


def _rope_tables(s):
    half = ROT_DIM // 2
    inv = ROPE_THETA ** (-jnp.arange(half, dtype=f32) * 2.0 / ROT_DIM)
    ang = jnp.arange(s).astype(f32)[:, None] * inv[None, :]
    cos, sin = jnp.cos(ang), jnp.sin(ang)
    pad = HEAD_DIM - ROT_DIM
    cos_t = jnp.concatenate([cos, cos, jnp.ones((s, pad), f32)], axis=-1)
    sin_t = jnp.concatenate([-sin, sin, jnp.zeros((s, pad), f32)], axis=-1)
    return cos_t, sin_t


def kernel(x, attn_norm, ffn_norm, pool_w, pool_scale, wq, wo, kv_norm, wk, wv, ffn_w_gate, ffn_w_up,
           ffn_w_down, router_w, moe_w_gate, moe_w_up, moe_w_down, final_norm):
    b, s, d = x.shape
    t = b * s
    n_exp = router_w.shape[-1]
    assert attn_norm.shape[0] == 2 and pool_w.shape[0] == 1 and wq.shape[0] == 1 and router_w.shape[0] == 1
    tok_rows = 1024
    ffn_rows = 512
    moe_chunk = 512
    moe_rows = 1024
    comb_rows = 1024
    assert s % tok_rows == 0 and d % (HEAD_DIM * 2) == 0 and t % comb_rows == 0
    assert moe_w_gate.shape[-1] % moe_chunk == 0 and moe_w_gate.shape[-1] // moe_chunk >= 2
    assert d % (8 * LANES) == 0 and moe_rows % tok_rows == 0

    row = lambda v: v.reshape(1, -1)

    h1, hn1 = _pool_mixer(x, row(attn_norm[0]), pool_w[0].astype(bf16), row(pool_scale[0]),
                          row(ffn_norm[0]), tok_rows)
    h2 = _dense_ffn(h1.reshape(t, d), hn1.reshape(t, d), ffn_w_gate[0].astype(bf16),
                    ffn_w_up[0].astype(bf16), ffn_w_down[0].astype(bf16), ffn_rows, 512)

    cos_t, sin_t = _rope_tables(s)
    qt, k, vt, kmean = _qkv_proj(h2.reshape(b, s, d), row(attn_norm[1]), row(kv_norm), wq[0].astype(bf16),
                                 wk.astype(bf16), wv.astype(bf16), cos_t, sin_t, tok_rows)
    o = _moba_attention(qt, k, vt, kmean)

    h3, idx, gates = _out_router(o.reshape(t, d), h2, wo[0].astype(bf16), row(ffn_norm[1]), router_w[0],
                                 tok_rows)

    (tile_expert, tile_valid, tile_src, pad_lo, pad_hi, n_rows, pos_disp,
     pos_comb) = _routing_plan(idx, n_exp, moe_rows, tok_rows, comb_rows)
    x_tiles = _moe_dispatch(pos_disp, pad_lo, pad_hi, h3, row(ffn_norm[1]), n_rows, tok_rows)
    y_tiles = _moe_gemm(tile_expert, tile_valid, tile_src, x_tiles, moe_w_gate[0], moe_w_up[0],
                        moe_w_down[0], moe_rows, moe_chunk)
    out = _moe_combine(pos_comb, h3, gates, row(final_norm), y_tiles, comb_rows)
    return out.reshape(b, s, d)
```

```python
import functools

import jax
import jax.numpy as jnp
from jax import lax
from jax.experimental import pallas as pl
from jax.experimental.pallas import tpu as pltpu

EPS = 1e-6
POOL_WINDOWS = (2, 4, 8, 16)
POOL_HALO = 16
LANES = 128
HEAD_DIM = 128
BLOCK = 256
TOP_BLOCKS = 3
ROT_DIM = HEAD_DIM // 4
ROPE_THETA = 500000.0
TOP_K = 2

VMEM_LIMIT = 56 * 1024 * 1024
DMA_UNROLL = 8
NEG_INF = float("-inf")
POS_INF = float("inf")
MXU_LAG = 8
ONES_ROWS = 16
LOG2_E = 1.4426950408889634

f32 = jnp.float32
bf16 = jnp.bfloat16


def _resident(shape):
    n = len(shape)
    return pl.BlockSpec(shape, lambda *_: (0,) * n, pipeline_mode=pl.Buffered(1))


def _rms_scale(xf):
    return lax.rsqrt(jnp.mean(xf * xf, axis=-1, keepdims=True) + EPS)


def _silu(g):
    return g * (1.0 / (1.0 + jnp.exp(-g)))


def _store_token_tiles(ref, x):
    rows, d = x.shape
    sub = d // LANES
    for c in range(sub):
        ref[pl.ds(c, rows, stride=sub), :] = x[:, c * LANES:(c + 1) * LANES]


def _load_token_tiles(ref, rows, sub, chunk):
    return ref[pl.ds(chunk, rows, stride=sub), :]


def _params(*sem, **kw):
    return pltpu.CompilerParams(dimension_semantics=sem, vmem_limit_bytes=VMEM_LIMIT, **kw)


def _pool_kernel(x_ref, halo_ref, g_attn_ref, w_ref, scale_ref, g_ffn_ref, h_ref, hn_ref, ext_ref):
    i = pl.program_id(1)
    rows = x_ref.shape[1]
    d = x_ref.shape[2]
    c = d // len(POOL_WINDOWS)
    g_attn = g_attn_ref[...]
    x = x_ref[0]
    xn = x * _rms_scale(x) * g_attn
    halo = halo_ref[0]
    halo = halo * _rms_scale(halo) * g_attn
    ext_ref[0:POOL_HALO, :] = jnp.where(i == 0, 0.0, halo)
    ext_ref[POOL_HALO:, :] = xn
    t = i * rows + lax.broadcasted_iota(jnp.int32, (rows, 1), 0)
    ys = []
    for g, w in enumerate(POOL_WINDOWS):
        c0 = g * c
        s = ext_ref[:, c0:c0 + c]
        shift = 1
        while shift < w:
            s = s + pltpu.roll(s, shift, 0)
            shift *= 2
        cnt = jnp.minimum(t + 1, w).astype(f32)
        pooled = s[POOL_HALO:, :] / cnt - xn[:, c0:c0 + c]
        ys.append(jnp.dot(pooled.astype(bf16), w_ref[g], preferred_element_type=f32))
    y = jnp.concatenate(ys, axis=-1) * scale_ref[...]
    h = x + y
    h_ref[0] = h
    hn_ref[0] = (h * _rms_scale(h) * g_ffn_ref[...]).astype(bf16)


def _pool_mixer(x, g_attn, w_pool, scale, g_ffn, rows):
    b, s, d = x.shape
    c = d // len(POOL_WINDOWS)
    hb = rows // POOL_HALO
    return pl.pallas_call(
        _pool_kernel,
        grid=(b, s // rows),
        in_specs=[
            pl.BlockSpec((1, rows, d), lambda bi, i: (bi, i, 0)),
            pl.BlockSpec((1, POOL_HALO, d), lambda bi, i: (bi, jnp.maximum(i * hb - 1, 0), 0)),
            _resident((1, d)),
            _resident((len(POOL_WINDOWS), c, c)),
            _resident((1, d)),
            _resident((1, d)),
        ],
        out_specs=[
            pl.BlockSpec((1, rows, d), lambda bi, i: (bi, i, 0)),
            pl.BlockSpec((1, rows, d), lambda bi, i: (bi, i, 0)),
        ],
        out_shape=[jax.ShapeDtypeStruct((b, s, d), f32), jax.ShapeDtypeStruct((b, s, d), bf16)],
        scratch_shapes=[pltpu.VMEM((rows + POOL_HALO, d), f32)],
        compiler_params=_params("arbitrary", "arbitrary"),
        name="pool_mixer",
    )(x, x, g_attn, w_pool, scale, g_ffn)


def _ffn_chunks(f, size):
    out, c0 = [], 0
    while c0 < f:
        out.append((c0, min(size, f - c0)))
        c0 += size
    return out


def _dense_ffn_kernel(h_ref, hn_ref, wg_ref, wu_ref, wd_ref, o_ref, *, chunk):
    hn = hn_ref[...]
    acc = h_ref[...]
    for c0, cw in _ffn_chunks(wg_ref.shape[1], chunk):
        g = jnp.dot(hn, wg_ref[:, c0:c0 + cw], preferred_element_type=f32)
        u = jnp.dot(hn, wu_ref[:, c0:c0 + cw], preferred_element_type=f32)
        a = (_silu(g) * u).astype(bf16)
        acc = acc + jnp.dot(a, wd_ref[c0:c0 + cw, :], preferred_element_type=f32)
    o_ref[...] = acc


def _dense_ffn(h, hn, wg, wu, wd, rows, chunk):
    t, d = h.shape
    f = wg.shape[1]
    return pl.pallas_call(
        functools.partial(_dense_ffn_kernel, chunk=chunk),
        grid=(t // rows,),
        in_specs=[
            pl.BlockSpec((rows, d), lambda i: (i, 0)),
            pl.BlockSpec((rows, d), lambda i: (i, 0)),
            _resident((d, f)),
            _resident((d, f)),
            _resident((f, d)),
        ],
        out_specs=pl.BlockSpec((rows, d), lambda i: (i, 0)),
        out_shape=jax.ShapeDtypeStruct((t, d), f32),
        compiler_params=_params("arbitrary"),
        name="dense_ffn",
    )(h, hn, wg, wu, wd)


def _rope(x, cos, sin_signed, lane):
    d = x.shape[-1]
    half = ROT_DIM // 2
    partner = jnp.where(lane < half, pltpu.roll(x, d - half, 1), pltpu.roll(x, half, 1))
    return x * cos + partner * sin_signed


def _qkv_kernel(h_ref, g_q_ref, g_kv_ref, wq_ref, wk_ref, wv_ref, cos_ref, sin_ref,
                qt_ref, k_ref, vt_ref, kmean_ref):
    i = pl.program_id(1)
    rows, d = h_ref.shape[1], h_ref.shape[2]
    heads = d // HEAD_DIM
    h = h_ref[0]
    xhat = h * _rms_scale(h)
    hq = (xhat * g_q_ref[...]).astype(bf16)
    hkv = (xhat * g_kv_ref[...]).astype(bf16)
    q = jnp.dot(hq, wq_ref[...], preferred_element_type=f32)
    k = jnp.dot(hkv, wk_ref[...], preferred_element_type=f32)
    v = jnp.dot(hkv, wv_ref[...], preferred_element_type=f32)
    cos = jnp.concatenate([cos_ref[...]] * heads, axis=-1)
    sin = jnp.concatenate([sin_ref[...]] * heads, axis=-1)
    lane = lax.broadcasted_iota(jnp.int32, (rows, d), 1) % HEAD_DIM
    q = _rope(q, cos, sin, lane) * (HEAD_DIM ** -0.5 * LOG2_E)
    k = _rope(k, cos, sin, lane)
    qt_ref[0] = q.T.astype(bf16)
    k_ref[0] = k.astype(bf16)
    vt_ref[0] = v.T.astype(bf16)

    nb = kmean_ref.shape[1]
    blk_iota = lax.broadcasted_iota(jnp.int32, (nb, d), 0)

    @pl.when(i == 0)
    def _():
        kmean_ref[0] = jnp.zeros((nb, d), f32)

    km = kmean_ref[0]
    for j in range(rows // BLOCK):
        mean_j = jnp.mean(k[j * BLOCK:(j + 1) * BLOCK, :], axis=0, keepdims=True)
        km = jnp.where(blk_iota == i * (rows // BLOCK) + j, mean_j, km)
    kmean_ref[0] = km


def _qkv_proj(h, g_q, g_kv, wq, wk, wv, cos, sin, rows):
    b, s, d = h.shape
    nb = s // BLOCK
    return pl.pallas_call(
        _qkv_kernel,
        grid=(b, s // rows),
        in_specs=[
            pl.BlockSpec((1, rows, d), lambda bi, i: (bi, i, 0)),
            _resident((1, d)),
            _resident((1, d)),
            _resident((d, d)),
            _resident((d, d)),
            _resident((d, d)),
            pl.BlockSpec((rows, HEAD_DIM), lambda bi, i: (i, 0)),
            pl.BlockSpec((rows, HEAD_DIM), lambda bi, i: (i, 0)),
        ],
        out_specs=[
            pl.BlockSpec((1, d, rows), lambda bi, i: (bi, 0, i)),
            pl.BlockSpec((1, rows, d), lambda bi, i: (bi, i, 0)),
            pl.BlockSpec((1, d, rows), lambda bi, i: (bi, 0, i)),
            pl.BlockSpec((1, nb, d), lambda bi, i: (bi, 0, 0)),
        ],
        out_shape=[
            jax.ShapeDtypeStruct((b, d, s), bf16),
            jax.ShapeDtypeStruct((b, s, d), bf16),
            jax.ShapeDtypeStruct((b, d, s), bf16),
            jax.ShapeDtypeStruct((b, nb, d), f32),
        ],
        compiler_params=_params("arbitrary", "arbitrary"),
        name="qkv_proj",
    )(h, g_q, g_kv, wq, wk, wv, cos, sin)


def _moba_kernel(qt_ref, k_ref, vt_ref, kmean_ref, o_ref, *scratch, heads):
    sel_refs, m_refs, l_refs, acc_refs = (scratch[i * heads:(i + 1) * heads] for i in range(4))
    j = pl.program_id(1)
    nb = kmean_ref.shape[1]
    hd = HEAD_DIM
    blk = lax.broadcasted_iota(jnp.int32, (nb, BLOCK), 0)
    past = blk < j
    kpos = lax.broadcasted_iota(jnp.int32, (BLOCK, BLOCK), 0)
    qpos = lax.broadcasted_iota(jnp.int32, (BLOCK, BLOCK), 1)
    causal = kpos <= qpos
    start = pl.multiple_of(j * BLOCK, BLOCK)

    def q_head(h):
        return qt_ref[0, h * hd:(h + 1) * hd, :]

    ones_rows = jnp.ones((ONES_ROWS, BLOCK), bf16)

    def score(h, off):
        return jnp.dot(k_ref[0, pl.ds(off, BLOCK), h * hd:(h + 1) * hd], q_head(h), preferred_element_type=f32)

    def weighted_values(h, p, off):
        vt_ones = jnp.concatenate([vt_ref[0, h * hd:(h + 1) * hd, pl.ds(off, BLOCK)], ones_rows], axis=0)
        out = jnp.dot(vt_ones, p, preferred_element_type=f32)
        return out[:hd], out[hd:hd + 1]

    def key_block(off, probs, accumulate):
        pending = {}
        for step in range(heads + MXU_LAG):
            if step < heads:
                pending[step] = probs(step, score(step, off))
            if step >= MXU_LAG:
                h = step - MXU_LAG
                p, extra = pending.pop(h)
                accumulate(h, extra, *weighted_values(h, p, off))

    for h in range(heads):
        km = kmean_ref[0, :, h * hd:(h + 1) * hd]
        km_hi = km.astype(bf16)
        km_lo = (km - km_hi.astype(f32)).astype(bf16)
        gate = (jnp.dot(km_hi, q_head(h), preferred_element_type=f32)
                + jnp.dot(km_lo, q_head(h), preferred_element_type=f32))
        gate = jnp.where(past, gate, NEG_INF)
        rank = jnp.zeros((nb, BLOCK), jnp.int32)
        for m in range(nb):
            gm = gate[m:m + 1, :]
            rank = rank + jnp.where(gm > gate, 1, jnp.where(gm == gate, jnp.where(blk > m, 1, 0), 0))
        sel_refs[h][...] = jnp.where(past, jnp.where(rank < TOP_BLOCKS, 1.0, 0.0), 0.0)

    def own_probs(h, s):
        s = jnp.where(causal, s, NEG_INF)
        m0 = jnp.max(s, axis=0, keepdims=True)
        m_refs[h][...] = m0
        return jnp.exp2((s - m0).astype(bf16)), None

    def own_accumulate(h, _, pv, psum):
        acc_refs[h][...] = pv
        l_refs[h][...] = psum

    key_block(start, own_probs, own_accumulate)

    def body(n, carry):
        def probs(h, s):
            chosen = sel_refs[h][pl.ds(n, 1), :] > 0.0
            m_old = m_refs[h][...]
            m_new = jnp.maximum(m_old, jnp.where(chosen, jnp.max(s, axis=0, keepdims=True), NEG_INF))
            m_refs[h][...] = m_new
            p = jnp.exp2((s - jnp.where(chosen, m_new, POS_INF)).astype(bf16))
            return p, jnp.exp2(m_old - m_new)

        def accumulate(h, alpha, pv, psum):
            acc_refs[h][...] = alpha * acc_refs[h][...] + pv
            l_refs[h][...] = alpha * l_refs[h][...] + psum

        key_block(pl.multiple_of(n * BLOCK, BLOCK), probs, accumulate)
        return carry

    lax.fori_loop(0, j, body, 0)
    for h in range(heads):
        o_ref[0, :, h * hd:(h + 1) * hd] = (acc_refs[h][...] / l_refs[h][...]).T.astype(o_ref.dtype)


def _moba_attention(qt, k, vt, kmean):
    b, d, s = qt.shape
    heads = d // HEAD_DIM
    nb = s // BLOCK
    return pl.pallas_call(
        functools.partial(_moba_kernel, heads=heads),
        grid=(b, nb),
        in_specs=[
            pl.BlockSpec((1, d, BLOCK), lambda bi, j: (bi, 0, j)),
            pl.BlockSpec((1, s, d), lambda bi, j: (bi, 0, 0), pipeline_mode=pl.Buffered(1)),
            pl.BlockSpec((1, d, s), lambda bi, j: (bi, 0, 0), pipeline_mode=pl.Buffered(1)),
            pl.BlockSpec((1, nb, d), lambda bi, j: (bi, 0, 0)),
        ],
        out_specs=pl.BlockSpec((1, BLOCK, d), lambda bi, j: (bi, j, 0)),
        out_shape=jax.ShapeDtypeStruct((b, s, d), bf16),
        scratch_shapes=(
            [pltpu.VMEM((nb, BLOCK), f32)] * heads
            + [pltpu.VMEM((1, BLOCK), f32)] * (2 * heads)
            + [pltpu.VMEM((HEAD_DIM, BLOCK), f32)] * heads
        ),
        compiler_params=_params("arbitrary", "arbitrary"),
        name="moba_attn",
    )(qt, k, vt, kmean)


def _out_router_kernel(o_ref, h_ref, wo_ref, g_ref, wr_ref, h_out_ref, idx_ref, gate_ref):
    rows = o_ref.shape[0]
    n_exp = wr_ref.shape[1]
    h = h_ref[...] + jnp.dot(o_ref[...], wo_ref[...], preferred_element_type=f32)
    h_out_ref[...] = h
    hn = h * _rms_scale(h) * g_ref[...]
    wr = wr_ref[...]
    w_hi = wr.astype(bf16)
    w_lo = (wr - w_hi.astype(f32)).astype(bf16)
    x_hi = hn.astype(bf16)
    x_lo = (hn - x_hi.astype(f32)).astype(bf16)
    both = jnp.dot(x_hi, jnp.concatenate([w_hi, w_lo], axis=1), preferred_element_type=f32)
    logits = both[:, :n_exp] + both[:, n_exp:] + jnp.dot(x_lo, w_hi, preferred_element_type=f32)
    e = lax.broadcasted_iota(jnp.int32, (rows, n_exp), 1)
    v1 = jnp.max(logits, axis=-1, keepdims=True)
    i1 = jnp.min(jnp.where(logits == v1, e, n_exp), axis=-1, keepdims=True)
    rest = jnp.where(e == i1, NEG_INF, logits)
    v2 = jnp.max(rest, axis=-1, keepdims=True)
    i2 = jnp.min(jnp.where(rest == v2, e, n_exp), axis=-1, keepdims=True)
    e2 = jnp.exp(v2 - v1)
    denom = 1.0 + e2
    col = lax.broadcasted_iota(jnp.int32, (rows, TOP_K), 1)
    idx_ref[...] = jnp.where(col == 0, i1, i2)
    gate_ref[...] = jnp.where(col == 0, 1.0 / denom, e2 / denom)


def _out_router(o, h, wo, g_ffn, w_router, rows):
    t, d = h.shape
    n_exp = w_router.shape[1]
    return pl.pallas_call(
        _out_router_kernel,
        grid=(t // rows,),
        in_specs=[
            pl.BlockSpec((rows, d), lambda i: (i, 0)),
            pl.BlockSpec((rows, d), lambda i: (i, 0)),
            _resident((d, d)),
            _resident((1, d)),
            _resident((d, n_exp)),
        ],
        out_specs=[
            pl.BlockSpec((rows, d), lambda i: (i, 0)),
            pl.BlockSpec((rows, TOP_K), lambda i: (i, 0)),
            pl.BlockSpec((rows, TOP_K), lambda i: (i, 0)),
        ],
        out_shape=[
            jax.ShapeDtypeStruct((t, d), f32),
            jax.ShapeDtypeStruct((t, TOP_K), jnp.int32),
            jax.ShapeDtypeStruct((t, TOP_K), f32),
        ],
        compiler_params=_params("arbitrary"),
        name="out_router",
    )(o, h, wo, g_ffn, w_router)


def _dispatch_kernel(pos_ref, pad_lo_ref, pad_hi_ref, h_ref, g_ref, x_hbm, buf_ref, zero_ref, sem, zsem):
    i = pl.program_id(0)
    n_steps = pl.num_programs(0)
    rows, d = h_ref.shape
    sub = d // LANES
    slot = i % 2

    def staged_copies_wait(buf):
        for _ in range(TOP_K):
            pltpu.make_async_copy(buf_ref.at[buf], x_hbm.at[pl.ds(0, rows * sub), :], sem.at[buf]).wait()

    @pl.when(i == 0)
    def _():
        zero_ref[...] = jnp.zeros(zero_ref.shape, f32)
        for e in range(pad_lo_ref.shape[0]):
            lo, hi = pad_lo_ref[e], pad_hi_ref[e]

            def fill(r, carry):
                pltpu.make_async_copy(zero_ref, x_hbm.at[pl.ds(pl.multiple_of(r * sub, sub), sub), :],
                                      zsem).start()
                return carry

            def drain(r, carry):
                pltpu.make_async_copy(zero_ref, x_hbm.at[pl.ds(0, sub), :], zsem).wait()
                return carry
            lax.fori_loop(lo, hi, fill, 0)
            lax.fori_loop(lo, hi, drain, 0)

        buf_ref[1] = jnp.zeros(buf_ref.shape[1:], f32)
        tail_lo = pad_hi_ref[pad_hi_ref.shape[0] - 1]
        n_tail = (x_hbm.shape[0] // sub - tail_lo) // rows

        def tail_copy(c):
            dst = pl.multiple_of((tail_lo + c * rows) * sub, sub)
            return pltpu.make_async_copy(buf_ref.at[1], x_hbm.at[pl.ds(dst, rows * sub), :], zsem)

        def tail_fill(c, carry):
            tail_copy(c).start()
            return carry

        def tail_drain(c, carry):
            tail_copy(c).wait()
            return carry
        lax.fori_loop(0, n_tail, tail_fill, 0)
        lax.fori_loop(0, n_tail, tail_drain, 0)

    @pl.when(i >= 2)
    def _():
        staged_copies_wait(slot)

    h = h_ref[...]
    _store_token_tiles(buf_ref.at[slot], h * _rms_scale(h) * g_ref[...])
    base = i * (TOP_K * rows)

    def trip(c, carry):
        for u in range(DMA_UNROLL):
            r = c * DMA_UNROLL + u
            src = buf_ref.at[slot, pl.ds(pl.multiple_of(r * sub, sub), sub), :]
            for k in range(TOP_K):
                dst = pl.multiple_of(pos_ref[base + k * rows + r] * sub, sub)
                pltpu.make_async_copy(src, x_hbm.at[pl.ds(dst, sub), :], sem.at[slot]).start(priority=k % 2)
        return carry
    lax.fori_loop(0, rows // DMA_UNROLL, trip, 0)

    @pl.when(i == n_steps - 1)
    def _():
        staged_copies_wait(slot)

        @pl.when(n_steps >= 2)
        def _():
            staged_copies_wait(1 - slot)


def _moe_dispatch(pos_steps, pad_lo, pad_hi, h, g_ffn, n_rows, rows):
    t, d = h.shape
    sub = d // LANES
    grid_spec = pltpu.PrefetchScalarGridSpec(
        num_scalar_prefetch=3,
        grid=(t // rows,),
        in_specs=[
            pl.BlockSpec((rows, d), lambda i, ps, lo, hi: (i, 0)),
            pl.BlockSpec((1, d), lambda i, ps, lo, hi: (0, 0)),
        ],
        out_specs=pl.BlockSpec(memory_space=pl.ANY),
        scratch_shapes=[
            pltpu.VMEM((2, rows * sub, LANES), f32),
            pltpu.VMEM((sub, LANES), f32),
            pltpu.SemaphoreType.DMA((2,)),
            pltpu.SemaphoreType.DMA,
        ],
    )
    return pl.pallas_call(
        _dispatch_kernel,
        grid_spec=grid_spec,
        out_shape=jax.ShapeDtypeStruct((n_rows * sub, LANES), f32),
        compiler_params=_params("arbitrary", disable_bounds_checks=True),
        name="moe_dispatch",
    )(pos_steps, pad_lo, pad_hi, h, g_ffn)


def _moe_gemm_kernel(tile_expert_ref, tile_valid_ref, tile_src_ref, x_ref, wg_ref, wu_ref, wd_ref,
                     y_ref, xb_ref, acc_ref):
    i = pl.program_id(0)
    f = pl.program_id(1)
    rows, d = xb_ref.shape
    sub = d // LANES
    valid = tile_valid_ref[i] > 0

    def swiglu_chunk():
        xb = xb_ref[...]
        g = jnp.dot(xb, wg_ref[0].astype(bf16), preferred_element_type=f32)
        u = jnp.dot(xb, wu_ref[0].astype(bf16), preferred_element_type=f32)
        a = (_silu(g) * u).astype(bf16)
        return jnp.dot(a, wd_ref[0].astype(bf16), preferred_element_type=f32)

    @pl.when(jnp.logical_and(f == 0, valid))
    def _():
        for c in range(sub):
            xb_ref[:, c * LANES:(c + 1) * LANES] = _load_token_tiles(x_ref, rows, sub, c).astype(bf16)
        acc_ref[...] = swiglu_chunk()

    last = pl.num_programs(1) - 1

    @pl.when(jnp.logical_and(jnp.logical_and(f > 0, f < last), valid))
    def _():
        acc_ref[...] += swiglu_chunk()

    @pl.when(jnp.logical_and(f == last, valid))
    def _():
        _store_token_tiles(y_ref, acc_ref[...] + swiglu_chunk())

    @pl.when(jnp.logical_and(f == last, jnp.logical_not(valid)))
    def _():
        y_ref[...] = jnp.zeros(y_ref.shape, f32)


def _moe_gemm(tile_expert, tile_valid, tile_src, x_tiles, wg, wu, wd, rows, chunk):
    n_tiles = tile_expert.shape[0]
    _, d, fe = wg.shape
    sub = d // LANES
    grid_spec = pltpu.PrefetchScalarGridSpec(
        num_scalar_prefetch=3,
        grid=(n_tiles, fe // chunk),
        in_specs=[
            pl.BlockSpec((rows * sub, LANES), lambda i, f, te, tv, ts: (ts[i], 0)),
            pl.BlockSpec((1, d, chunk), lambda i, f, te, tv, ts: (te[i], 0, f)),
            pl.BlockSpec((1, d, chunk), lambda i, f, te, tv, ts: (te[i], 0, f)),
            pl.BlockSpec((1, chunk, d), lambda i, f, te, tv, ts: (te[i], f, 0)),
        ],
        out_specs=pl.BlockSpec((rows * sub, LANES), lambda i, f, te, tv, ts: (i, 0)),
        scratch_shapes=[
            pltpu.VMEM((rows, d), bf16),
            pltpu.VMEM((rows, d), f32),
        ],
    )
    return pl.pallas_call(
        _moe_gemm_kernel,
        grid_spec=grid_spec,
        out_shape=jax.ShapeDtypeStruct((n_tiles * rows * sub, LANES), f32),
        compiler_params=_params("arbitrary", "arbitrary"),
        name="moe_gemm",
    )(tile_expert, tile_valid, tile_src, x_tiles, wg, wu, wd)


def _combine_kernel(pos_ref, h_ref, gate_ref, g_ref, y_hbm, o_ref, ybuf_ref, sem):
    i = pl.program_id(0)
    rows, d = h_ref.shape
    sub = d // LANES
    slot = i % 2

    def gather_rows(step, buf):
        base = step * (TOP_K * rows)

        def trip(c, carry):
            for u in range(DMA_UNROLL):
                r = c * DMA_UNROLL + u
                for k in range(TOP_K):
                    src = pl.multiple_of(pos_ref[base + k * rows + r] * sub, sub)
                    pltpu.make_async_copy(y_hbm.at[pl.ds(src, sub), :],
                                          ybuf_ref.at[buf, k, pl.ds(pl.multiple_of(r * sub, sub), sub), :],
                                          sem.at[buf]).start(priority=k % 2)
            return carry
        lax.fori_loop(0, rows // DMA_UNROLL, trip, 0)

    @pl.when(i == 0)
    def _():
        gather_rows(0, 0)

    @pl.when(i + 1 < pl.num_programs(0))
    def _():
        gather_rows(i + 1, 1 - slot)

    for k in range(TOP_K):
        pltpu.make_async_copy(y_hbm.at[pl.ds(0, rows * sub), :], ybuf_ref.at[slot, k], sem.at[slot]).wait()
    gate = gate_ref[...]
    chunks = []
    sumsq = jnp.zeros((rows, 1), f32)
    for c in range(sub):
        hc = h_ref[:, c * LANES:(c + 1) * LANES]
        for k in range(TOP_K):
            hc = hc + gate[:, k:k + 1] * _load_token_tiles(ybuf_ref.at[slot, k], rows, sub, c)
        sumsq = sumsq + jnp.sum(hc * hc, axis=-1, keepdims=True)
        chunks.append(hc)
    scale = lax.rsqrt(sumsq / d + EPS)
    for c in range(sub):
        o_ref[:, c * LANES:(c + 1) * LANES] = chunks[c] * scale * g_ref[:, c * LANES:(c + 1) * LANES]


def _moe_combine(pos, h, gates, g_final, y_tiles, rows):
    t, d = h.shape
    sub = d // LANES
    grid_spec = pltpu.PrefetchScalarGridSpec(
        num_scalar_prefetch=1,
        grid=(t // rows,),
        in_specs=[
            pl.BlockSpec((rows, d), lambda i, ps: (i, 0)),
            pl.BlockSpec((rows, TOP_K), lambda i, ps: (i, 0)),
            pl.BlockSpec((1, d), lambda i, ps: (0, 0)),
            pl.BlockSpec(memory_space=pl.ANY),
        ],
        out_specs=pl.BlockSpec((rows, d), lambda i, ps: (i, 0)),
        scratch_shapes=[pltpu.VMEM((2, TOP_K, rows * sub, LANES), f32), pltpu.SemaphoreType.DMA((2,))],
    )
    return pl.pallas_call(
        _combine_kernel,
        grid_spec=grid_spec,
        out_shape=jax.ShapeDtypeStruct((t, d), f32),
        compiler_params=_params("arbitrary", disable_bounds_checks=True),
        name="moe_combine",
    )(pos, h, gates, g_final, y_tiles)


def _routing_plan(idx, n_exp, rows, disp_rows, comb_rows):
    t = idx.shape[0]
    pairs = t * TOP_K
    n_tiles = -(-pairs // rows) + n_exp
    e_flat = idx.T.reshape(pairs)
    onehot = (e_flat[:, None] == jnp.arange(n_exp, dtype=jnp.int32)[None, :]).astype(jnp.int32)
    csum = jnp.cumsum(onehot, axis=0)
    counts = csum[-1]
    rank = jnp.sum(onehot * csum, axis=1) - 1
    padded = ((counts + rows - 1) // rows) * rows
    pend = jnp.cumsum(padded)
    poff = pend - padded
    pos = poff[e_flat] + rank
    tile_start = jnp.arange(n_tiles, dtype=jnp.int32) * rows
    tile_expert = jnp.minimum(jnp.sum((tile_start[:, None] >= pend[None, :]).astype(jnp.int32), axis=1),
                              n_exp - 1)
    tile_valid = (tile_start < pend[-1]).astype(jnp.int32)
    last_valid = jnp.maximum(jnp.sum(tile_valid) - 1, 0)
    tile_expert = jnp.where(tile_valid > 0, tile_expert, tile_expert[last_valid])
    tile_src = jnp.minimum(jnp.arange(n_tiles, dtype=jnp.int32), last_valid)

    def by_step(step_rows):
        return pos.reshape(TOP_K, t // step_rows, step_rows).transpose(1, 0, 2).reshape(pairs)

    return (tile_expert, tile_valid, tile_src, poff + counts, pend, n_tiles * rows,
            by_step(disp_rows), by_step(comb_rows))


def _rope_tables(s):
    half = ROT_DIM // 2
    inv = ROPE_THETA ** (-jnp.arange(half, dtype=f32) * 2.0 / ROT_DIM)
    ang = jnp.arange(s).astype(f32)[:, None] * inv[None, :]
    cos, sin = jnp.cos(ang), jnp.sin(ang)
    pad = HEAD_DIM - ROT_DIM
    cos_t = jnp.concatenate([cos, cos, jnp.ones((s, pad), f32)], axis=-1)
    sin_t = jnp.concatenate([-sin, sin, jnp.zeros((s, pad), f32)], axis=-1)
    return cos_t, sin_t


def kernel(x, attn_norm, ffn_norm, pool_w, pool_scale, wq, wo, kv_norm, wk, wv, ffn_w_gate, ffn_w_up,
           ffn_w_down, router_w, moe_w_gate, moe_w_up, moe_w_down, final_norm):
    b, s, d = x.shape
    t = b * s
    n_exp = router_w.shape[-1]
    assert attn_norm.shape[0] == 2 and pool_w.shape[0] == 1 and wq.shape[0] == 1 and router_w.shape[0] == 1
    tok_rows = 1024
    ffn_rows = 512
    moe_chunk = 512
    moe_rows = 1024
    comb_rows = 512
    assert s % tok_rows == 0 and d % (HEAD_DIM * 2) == 0 and t % comb_rows == 0
    assert moe_w_gate.shape[-1] % moe_chunk == 0 and moe_w_gate.shape[-1] // moe_chunk >= 2
    assert d % (8 * LANES) == 0 and moe_rows % tok_rows == 0

    row = lambda v: v.reshape(1, -1)

    h1, hn1 = _pool_mixer(x, row(attn_norm[0]), pool_w[0].astype(bf16), row(pool_scale[0]),
                          row(ffn_norm[0]), tok_rows)
    h2 = _dense_ffn(h1.reshape(t, d), hn1.reshape(t, d), ffn_w_gate[0].astype(bf16),
                    ffn_w_up[0].astype(bf16), ffn_w_down[0].astype(bf16), ffn_rows, 512)

    cos_t, sin_t = _rope_tables(s)
    qt, k, vt, kmean = _qkv_proj(h2.reshape(b, s, d), row(attn_norm[1]), row(kv_norm), wq[0].astype(bf16),
                                 wk.astype(bf16), wv.astype(bf16), cos_t, sin_t, tok_rows)
    o = _moba_attention(qt, k, vt, kmean)

    h3, idx, gates = _out_router(o.reshape(t, d), h2, wo[0].astype(bf16), row(ffn_norm[1]), router_w[0],
                                 tok_rows)

    (tile_expert, tile_valid, tile_src, pad_lo, pad_hi, n_rows, pos_disp,
     pos_comb) = _routing_plan(idx, n_exp, moe_rows, tok_rows, comb_rows)
    x_tiles = _moe_dispatch(pos_disp, pad_lo, pad_hi, h3, row(ffn_norm[1]), n_rows, tok_rows)
    y_tiles = _moe_gemm(tile_expert, tile_valid, tile_src, x_tiles, moe_w_gate[0], moe_w_up[0],
                        moe_w_down[0], moe_rows, moe_chunk)
    out = _moe_combine(pos_comb, h3, gates, row(final_norm), y_tiles, comb_rows)
    return out.reshape(b, s, d)
```

```python
import functools

import jax
import jax.numpy as jnp
from jax import lax
from jax.experimental import pallas as pl
from jax.experimental.pallas import tpu as pltpu

EPS = 1e-6
POOL_WINDOWS = (2, 4, 8, 16)
POOL_HALO = 16
LANES = 128
HEAD_DIM = 128
BLOCK = 256
TOP_BLOCKS = 3
ROT_DIM = HEAD_DIM // 4
ROPE_THETA = 500000.0
TOP_K = 2

VMEM_LIMIT = 56 * 1024 * 1024
DMA_UNROLL = 8
NEG_INF = float("-inf")
POS_INF = float("inf")
MXU_LAG = 8
ONES_ROWS = 16
LOG2_E = 1.4426950408889634

f32 = jnp.float32
bf16 = jnp.bfloat16


def _resident(shape):
    n = len(shape)
    return pl.BlockSpec(shape, lambda *_: (0,) * n, pipeline_mode=pl.Buffered(1))


def _rms_scale(xf):
    return lax.rsqrt(jnp.mean(xf * xf, axis=-1, keepdims=True) + EPS)


def _silu(g):
    return g * (1.0 / (1.0 + jnp.exp(-g)))


def _store_token_tiles(ref, x):
    rows, d = x.shape
    sub = d // LANES
    for c in range(sub):
        ref[pl.ds(c, rows, stride=sub), :] = x[:, c * LANES:(c + 1) * LANES]


def _load_token_tiles(ref, rows, sub, chunk):
    return ref[pl.ds(chunk, rows, stride=sub), :]


def _params(*sem, **kw):
    return pltpu.CompilerParams(dimension_semantics=sem, vmem_limit_bytes=VMEM_LIMIT, **kw)


def _pool_kernel(x_ref, halo_ref, g_attn_ref, w_ref, scale_ref, g_ffn_ref, h_ref, hn_ref, ext_ref):
    i = pl.program_id(1)
    rows = x_ref.shape[1]
    d = x_ref.shape[2]
    c = d // len(POOL_WINDOWS)
    g_attn = g_attn_ref[...]
    x = x_ref[0]
    xn = x * _rms_scale(x) * g_attn
    halo = halo_ref[0]
    halo = halo * _rms_scale(halo) * g_attn
    ext_ref[0:POOL_HALO, :] = jnp.where(i == 0, 0.0, halo)
    ext_ref[POOL_HALO:, :] = xn
    t = i * rows + lax.broadcasted_iota(jnp.int32, (rows, 1), 0)
    ys = []
    for g, w in enumerate(POOL_WINDOWS):
        c0 = g * c
        s = ext_ref[:, c0:c0 + c]
        shift = 1
        while shift < w:
            s = s + pltpu.roll(s, shift, 0)
            shift *= 2
        cnt = jnp.minimum(t + 1, w).astype(f32)
        pooled = s[POOL_HALO:, :] / cnt - xn[:, c0:c0 + c]
        ys.append(jnp.dot(pooled.astype(bf16), w_ref[g], preferred_element_type=f32))
    y = jnp.concatenate(ys, axis=-1) * scale_ref[...]
    h = x + y
    h_ref[0] = h
    hn_ref[0] = (h * _rms_scale(h) * g_ffn_ref[...]).astype(bf16)


def _pool_mixer(x, g_attn, w_pool, scale, g_ffn, rows):
    b, s, d = x.shape
    c = d // len(POOL_WINDOWS)
    hb = rows // POOL_HALO
    return pl.pallas_call(
        _pool_kernel,
        grid=(b, s // rows),
        in_specs=[
            pl.BlockSpec((1, rows, d), lambda bi, i: (bi, i, 0)),
            pl.BlockSpec((1, POOL_HALO, d), lambda bi, i: (bi, jnp.maximum(i * hb - 1, 0), 0)),
            _resident((1, d)),
            _resident((len(POOL_WINDOWS), c, c)),
            _resident((1, d)),
            _resident((1, d)),
        ],
        out_specs=[
            pl.BlockSpec((1, rows, d), lambda bi, i: (bi, i, 0)),
            pl.BlockSpec((1, rows, d), lambda bi, i: (bi, i, 0)),
        ],
        out_shape=[jax.ShapeDtypeStruct((b, s, d), f32), jax.ShapeDtypeStruct((b, s, d), bf16)],
        scratch_shapes=[pltpu.VMEM((rows + POOL_HALO, d), f32)],
        compiler_params=_params("arbitrary", "arbitrary"),
        name="pool_mixer",
    )(x, x, g_attn, w_pool, scale, g_ffn)


def _ffn_chunks(f, size):
    out, c0 = [], 0
    while c0 < f:
        out.append((c0, min(size, f - c0)))
        c0 += size
    return out


def _dense_ffn_kernel(h_ref, hn_ref, wg_ref, wu_ref, wd_ref, o_ref, *, chunk):
    hn = hn_ref[...]
    acc = h_ref[...]
    for c0, cw in _ffn_chunks(wg_ref.shape[1], chunk):
        g = jnp.dot(hn, wg_ref[:, c0:c0 + cw], preferred_element_type=f32)
        u = jnp.dot(hn, wu_ref[:, c0:c0 + cw], preferred_element_type=f32)
        a = (_silu(g) * u).astype(bf16)
        acc = acc + jnp.dot(a, wd_ref[c0:c0 + cw, :], preferred_element_type=f32)
    o_ref[...] = acc


def _dense_ffn(h, hn, wg, wu, wd, rows, chunk):
    t, d = h.shape
    f = wg.shape[1]
    return pl.pallas_call(
        functools.partial(_dense_ffn_kernel, chunk=chunk),
        grid=(t // rows,),
        in_specs=[
            pl.BlockSpec((rows, d), lambda i: (i, 0)),
            pl.BlockSpec((rows, d), lambda i: (i, 0)),
            _resident((d, f)),
            _resident((d, f)),
            _resident((f, d)),
        ],
        out_specs=pl.BlockSpec((rows, d), lambda i: (i, 0)),
        out_shape=jax.ShapeDtypeStruct((t, d), f32),
        compiler_params=_params("arbitrary"),
        name="dense_ffn",
    )(h, hn, wg, wu, wd)


def _rope(x, cos, sin_signed, lane):
    d = x.shape[-1]
    half = ROT_DIM // 2
    partner = jnp.where(lane < half, pltpu.roll(x, d - half, 1), pltpu.roll(x, half, 1))
    return x * cos + partner * sin_signed


def _qkv_kernel(h_ref, g_q_ref, g_kv_ref, wq_ref, wk_ref, wv_ref, cos_ref, sin_ref,
                qt_ref, k_ref, vt_ref, kmean_ref):
    i = pl.program_id(1)
    rows, d = h_ref.shape[1], h_ref.shape[2]
    heads = d // HEAD_DIM
    h = h_ref[0]
    xhat = h * _rms_scale(h)
    hq = (xhat * g_q_ref[...]).astype(bf16)
    hkv = (xhat * g_kv_ref[...]).astype(bf16)
    q = jnp.dot(hq, wq_ref[...], preferred_element_type=f32)
    k = jnp.dot(hkv, wk_ref[...], preferred_element_type=f32)
    v = jnp.dot(hkv, wv_ref[...], preferred_element_type=f32)
    cos = jnp.concatenate([cos_ref[...]] * heads, axis=-1)
    sin = jnp.concatenate([sin_ref[...]] * heads, axis=-1)
    lane = lax.broadcasted_iota(jnp.int32, (rows, d), 1) % HEAD_DIM
    q = _rope(q, cos, sin, lane) * (HEAD_DIM ** -0.5 * LOG2_E)
    k = _rope(k, cos, sin, lane)
    qt_ref[0] = q.T.astype(bf16)
    k_ref[0] = k.astype(bf16)
    vt_ref[0] = v.T.astype(bf16)

    nb = kmean_ref.shape[1]
    blk_iota = lax.broadcasted_iota(jnp.int32, (nb, d), 0)

    @pl.when(i == 0)
    def _():
        kmean_ref[0] = jnp.zeros((nb, d), f32)

    km = kmean_ref[0]
    for j in range(rows // BLOCK):
        mean_j = jnp.mean(k[j * BLOCK:(j + 1) * BLOCK, :], axis=0, keepdims=True)
        km = jnp.where(blk_iota == i * (rows // BLOCK) + j, mean_j, km)
    kmean_ref[0] = km


def _qkv_proj(h, g_q, g_kv, wq, wk, wv, cos, sin, rows):
    b, s, d = h.shape
    nb = s // BLOCK
    return pl.pallas_call(
        _qkv_kernel,
        grid=(b, s // rows),
        in_specs=[
            pl.BlockSpec((1, rows, d), lambda bi, i: (bi, i, 0)),
            _resident((1, d)),
            _resident((1, d)),
            _resident((d, d)),
            _resident((d, d)),
            _resident((d, d)),
            pl.BlockSpec((rows, HEAD_DIM), lambda bi, i: (i, 0)),
            pl.BlockSpec((rows, HEAD_DIM), lambda bi, i: (i, 0)),
        ],
        out_specs=[
            pl.BlockSpec((1, d, rows), lambda bi, i: (bi, 0, i)),
            pl.BlockSpec((1, rows, d), lambda bi, i: (bi, i, 0)),
            pl.BlockSpec((1, d, rows), lambda bi, i: (bi, 0, i)),
            pl.BlockSpec((1, nb, d), lambda bi, i: (bi, 0, 0)),
        ],
        out_shape=[
            jax.ShapeDtypeStruct((b, d, s), bf16),
            jax.ShapeDtypeStruct((b, s, d), bf16),
            jax.ShapeDtypeStruct((b, d, s), bf16),
            jax.ShapeDtypeStruct((b, nb, d), f32),
        ],
        compiler_params=_params("arbitrary", "arbitrary"),
        name="qkv_proj",
    )(h, g_q, g_kv, wq, wk, wv, cos, sin)


def _moba_kernel(qt_ref, k_ref, vt_ref, kmean_ref, o_ref, *scratch, heads):
    sel_refs, m_refs, l_refs, acc_refs = (scratch[i * heads:(i + 1) * heads] for i in range(4))
    j = pl.program_id(1)
    nb = kmean_ref.shape[1]
    hd = HEAD_DIM
    blk = lax.broadcasted_iota(jnp.int32, (nb, BLOCK), 0)
    past = blk < j
    kpos = lax.broadcasted_iota(jnp.int32, (BLOCK, BLOCK), 0)
    qpos = lax.broadcasted_iota(jnp.int32, (BLOCK, BLOCK), 1)
    causal = kpos <= qpos
    start = pl.multiple_of(j * BLOCK, BLOCK)

    def q_head(h):
        return qt_ref[0, h * hd:(h + 1) * hd, :]

    ones_rows = jnp.ones((ONES_ROWS, BLOCK), bf16)

    def score(h, off):
        return jnp.dot(k_ref[0, pl.ds(off, BLOCK), h * hd:(h + 1) * hd], q_head(h), preferred_element_type=f32)

    def weighted_values(h, p, off):
        vt_ones = jnp.concatenate([vt_ref[0, h * hd:(h + 1) * hd, pl.ds(off, BLOCK)], ones_rows], axis=0)
        out = jnp.dot(vt_ones, p, preferred_element_type=f32)
        return out[:hd], out[hd:hd + 1]

    def key_block(off, probs, accumulate):
        pending = {}
        for step in range(heads + MXU_LAG):
            if step < heads:
                pending[step] = probs(step, score(step, off))
            if step >= MXU_LAG:
                h = step - MXU_LAG
                p, extra = pending.pop(h)
                accumulate(h, extra, *weighted_values(h, p, off))

    for h in range(heads):
        km = kmean_ref[0, :, h * hd:(h + 1) * hd]
        km_hi = km.astype(bf16)
        km_lo = (km - km_hi.astype(f32)).astype(bf16)
        gate = (jnp.dot(km_hi, q_head(h), preferred_element_type=f32)
                + jnp.dot(km_lo, q_head(h), preferred_element_type=f32))
        gate = jnp.where(past, gate, NEG_INF)
        rank = jnp.zeros((nb, BLOCK), jnp.int32)
        for m in range(nb):
            gm = gate[m:m + 1, :]
            rank = rank + jnp.where(gm > gate, 1, jnp.where(gm == gate, jnp.where(blk > m, 1, 0), 0))
        sel_refs[h][...] = jnp.where(past, jnp.where(rank < TOP_BLOCKS, 1.0, 0.0), 0.0)

    def own_probs(h, s):
        s = jnp.where(causal, s, NEG_INF)
        m0 = jnp.max(s, axis=0, keepdims=True)
        m_refs[h][...] = m0
        return jnp.exp2((s - m0).astype(bf16)), None

    def own_accumulate(h, _, pv, psum):
        acc_refs[h][...] = pv
        l_refs[h][...] = psum

    key_block(start, own_probs, own_accumulate)

    def body(n, carry):
        def probs(h, s):
            chosen = sel_refs[h][pl.ds(n, 1), :] > 0.0
            m_old = m_refs[h][...]
            m_new = jnp.maximum(m_old, jnp.where(chosen, jnp.max(s, axis=0, keepdims=True), NEG_INF))
            m_refs[h][...] = m_new
            p = jnp.exp2((s - jnp.where(chosen, m_new, POS_INF)).astype(bf16))
            return p, jnp.exp2(m_old - m_new)

        def accumulate(h, alpha, pv, psum):
            acc_refs[h][...] = alpha * acc_refs[h][...] + pv
            l_refs[h][...] = alpha * l_refs[h][...] + psum

        key_block(pl.multiple_of(n * BLOCK, BLOCK), probs, accumulate)
        return carry

    lax.fori_loop(0, j, body, 0)
    for h in range(heads):
        o_ref[0, :, h * hd:(h + 1) * hd] = (acc_refs[h][...] / l_refs[h][...]).T.astype(o_ref.dtype)


def _moba_attention(qt, k, vt, kmean):
    b, d, s = qt.shape
    heads = d // HEAD_DIM
    nb = s // BLOCK
    return pl.pallas_call(
        functools.partial(_moba_kernel, heads=heads),
        grid=(b, nb),
        in_specs=[
            pl.BlockSpec((1, d, BLOCK), lambda bi, j: (bi, 0, j)),
            pl.BlockSpec((1, s, d), lambda bi, j: (bi, 0, 0)),
            pl.BlockSpec((1, d, s), lambda bi, j: (bi, 0, 0)),
            pl.BlockSpec((1, nb, d), lambda bi, j: (bi, 0, 0)),
        ],
        out_specs=pl.BlockSpec((1, BLOCK, d), lambda bi, j: (bi, j, 0)),
        out_shape=jax.ShapeDtypeStruct((b, s, d), bf16),
        scratch_shapes=(
            [pltpu.VMEM((nb, BLOCK), f32)] * heads
            + [pltpu.VMEM((1, BLOCK), f32)] * (2 * heads)
            + [pltpu.VMEM((HEAD_DIM, BLOCK), f32)] * heads
        ),
        compiler_params=_params("arbitrary", "arbitrary"),
        name="moba_attn",
    )(qt, k, vt, kmean)


def _out_router_kernel(o_ref, h_ref, wo_ref, g_ref, wr_ref, h_out_ref, idx_ref, gate_ref):
    rows = o_ref.shape[0]
    n_exp = wr_ref.shape[1]
    h = h_ref[...] + jnp.dot(o_ref[...], wo_ref[...], preferred_element_type=f32)
    h_out_ref[...] = h
    hn = h * _rms_scale(h) * g_ref[...]
    wr = wr_ref[...]
    w_hi = wr.astype(bf16)
    w_lo = (wr - w_hi.astype(f32)).astype(bf16)
    x_hi = hn.astype(bf16)
    x_lo = (hn - x_hi.astype(f32)).astype(bf16)
    both = jnp.dot(x_hi, jnp.concatenate([w_hi, w_lo], axis=1), preferred_element_type=f32)
    logits = both[:, :n_exp] + both[:, n_exp:] + jnp.dot(x_lo, w_hi, preferred_element_type=f32)
    e = lax.broadcasted_iota(jnp.int32, (rows, n_exp), 1)
    v1 = jnp.max(logits, axis=-1, keepdims=True)
    i1 = jnp.min(jnp.where(logits == v1, e, n_exp), axis=-1, keepdims=True)
    rest = jnp.where(e == i1, NEG_INF, logits)
    v2 = jnp.max(rest, axis=-1, keepdims=True)
    i2 = jnp.min(jnp.where(rest == v2, e, n_exp), axis=-1, keepdims=True)
    e2 = jnp.exp(v2 - v1)
    denom = 1.0 + e2
    col = lax.broadcasted_iota(jnp.int32, (rows, TOP_K), 1)
    idx_ref[...] = jnp.where(col == 0, i1, i2)
    gate_ref[...] = jnp.where(col == 0, 1.0 / denom, e2 / denom)


def _out_router(o, h, wo, g_ffn, w_router, rows):
    t, d = h.shape
    n_exp = w_router.shape[1]
    return pl.pallas_call(
        _out_router_kernel,
        grid=(t // rows,),
        in_specs=[
            pl.BlockSpec((rows, d), lambda i: (i, 0)),
            pl.BlockSpec((rows, d), lambda i: (i, 0)),
            _resident((d, d)),
            _resident((1, d)),
            _resident((d, n_exp)),
        ],
        out_specs=[
            pl.BlockSpec((rows, d), lambda i: (i, 0)),
            pl.BlockSpec((rows, TOP_K), lambda i: (i, 0)),
            pl.BlockSpec((rows, TOP_K), lambda i: (i, 0)),
        ],
        out_shape=[
            jax.ShapeDtypeStruct((t, d), f32),
            jax.ShapeDtypeStruct((t, TOP_K), jnp.int32),
            jax.ShapeDtypeStruct((t, TOP_K), f32),
        ],
        compiler_params=_params("arbitrary"),
        name="out_router",
    )(o, h, wo, g_ffn, w_router)


def _dispatch_kernel(pos_ref, pad_lo_ref, pad_hi_ref, h_ref, g_ref, x_hbm, buf_ref, zero_ref, sem, zsem):
    i = pl.program_id(0)
    n_steps = pl.num_programs(0)
    rows, d = h_ref.shape
    sub = d // LANES
    slot = i % 2

    def staged_copies_wait(buf):
        for _ in range(TOP_K):
            pltpu.make_async_copy(buf_ref.at[buf], x_hbm.at[pl.ds(0, rows * sub), :], sem.at[buf]).wait()

    @pl.when(i == 0)
    def _():
        zero_ref[...] = jnp.zeros(zero_ref.shape, f32)
        for e in range(pad_lo_ref.shape[0]):
            lo, hi = pad_lo_ref[e], pad_hi_ref[e]

            def fill(r, carry):
                pltpu.make_async_copy(zero_ref, x_hbm.at[pl.ds(pl.multiple_of(r * sub, sub), sub), :],
                                      zsem).start()
                return carry

            def drain(r, carry):
                pltpu.make_async_copy(zero_ref, x_hbm.at[pl.ds(0, sub), :], zsem).wait()
                return carry
            lax.fori_loop(lo, hi, fill, 0)
            lax.fori_loop(lo, hi, drain, 0)

        buf_ref[1] = jnp.zeros(buf_ref.shape[1:], f32)
        tail_lo = pad_hi_ref[pad_hi_ref.shape[0] - 1]
        n_tail = (x_hbm.shape[0] // sub - tail_lo) // rows

        def tail_copy(c):
            dst = pl.multiple_of((tail_lo + c * rows) * sub, sub)
            return pltpu.make_async_copy(buf_ref.at[1], x_hbm.at[pl.ds(dst, rows * sub), :], zsem)

        def tail_fill(c, carry):
            tail_copy(c).start()
            return carry

        def tail_drain(c, carry):
            tail_copy(c).wait()
            return carry
        lax.fori_loop(0, n_tail, tail_fill, 0)
        lax.fori_loop(0, n_tail, tail_drain, 0)

    @pl.when(i >= 2)
    def _():
        staged_copies_wait(slot)

    h = h_ref[...]
    _store_token_tiles(buf_ref.at[slot], h * _rms_scale(h) * g_ref[...])
    base = i * (TOP_K * rows)

    def trip(c, carry):
        for u in range(DMA_UNROLL):
            r = c * DMA_UNROLL + u
            src = buf_ref.at[slot, pl.ds(pl.multiple_of(r * sub, sub), sub), :]
            for k in range(TOP_K):
                dst = pl.multiple_of(pos_ref[base + k * rows + r] * sub, sub)
                pltpu.make_async_copy(src, x_hbm.at[pl.ds(dst, sub), :], sem.at[slot]).start(priority=k % 2)
        return carry
    lax.fori_loop(0, rows // DMA_UNROLL, trip, 0)

    @pl.when(i == n_steps - 1)
    def _():
        staged_copies_wait(slot)

        @pl.when(n_steps >= 2)
        def _():
            staged_copies_wait(1 - slot)


def _moe_dispatch(pos_steps, pad_lo, pad_hi, h, g_ffn, n_rows, rows):
    t, d = h.shape
    sub = d // LANES
    grid_spec = pltpu.PrefetchScalarGridSpec(
        num_scalar_prefetch=3,
        grid=(t // rows,),
        in_specs=[
            pl.BlockSpec((rows, d), lambda i, ps, lo, hi: (i, 0)),
            pl.BlockSpec((1, d), lambda i, ps, lo, hi: (0, 0)),
        ],
        out_specs=pl.BlockSpec(memory_space=pl.ANY),
        scratch_shapes=[
            pltpu.VMEM((2, rows * sub, LANES), f32),
            pltpu.VMEM((sub, LANES), f32),
            pltpu.SemaphoreType.DMA((2,)),
            pltpu.SemaphoreType.DMA,
        ],
    )
    return pl.pallas_call(
        _dispatch_kernel,
        grid_spec=grid_spec,
        out_shape=jax.ShapeDtypeStruct((n_rows * sub, LANES), f32),
        compiler_params=_params("arbitrary", disable_bounds_checks=True),
        name="moe_dispatch",
    )(pos_steps, pad_lo, pad_hi, h, g_ffn)


def _moe_gemm_kernel(tile_expert_ref, tile_valid_ref, tile_src_ref, x_ref, wg_ref, wu_ref, wd_ref,
                     y_ref, xb_ref, acc_ref):
    i = pl.program_id(0)
    f = pl.program_id(1)
    rows, d = xb_ref.shape
    sub = d // LANES
    valid = tile_valid_ref[i] > 0

    def swiglu_chunk():
        xb = xb_ref[...]
        g = jnp.dot(xb, wg_ref[0].astype(bf16), preferred_element_type=f32)
        u = jnp.dot(xb, wu_ref[0].astype(bf16), preferred_element_type=f32)
        a = (_silu(g) * u).astype(bf16)
        return jnp.dot(a, wd_ref[0].astype(bf16), preferred_element_type=f32)

    @pl.when(jnp.logical_and(f == 0, valid))
    def _():
        for c in range(sub):
            xb_ref[:, c * LANES:(c + 1) * LANES] = _load_token_tiles(x_ref, rows, sub, c).astype(bf16)
        acc_ref[...] = swiglu_chunk()

    last = pl.num_programs(1) - 1

    @pl.when(jnp.logical_and(jnp.logical_and(f > 0, f < last), valid))
    def _():
        acc_ref[...] += swiglu_chunk()

    @pl.when(jnp.logical_and(f == last, valid))
    def _():
        _store_token_tiles(y_ref, acc_ref[...] + swiglu_chunk())

    @pl.when(jnp.logical_and(f == last, jnp.logical_not(valid)))
    def _():
        y_ref[...] = jnp.zeros(y_ref.shape, f32)


def _moe_gemm(tile_expert, tile_valid, tile_src, x_tiles, wg, wu, wd, rows, chunk):
    n_tiles = tile_expert.shape[0]
    _, d, fe = wg.shape
    sub = d // LANES
    grid_spec = pltpu.PrefetchScalarGridSpec(
        num_scalar_prefetch=3,
        grid=(n_tiles, fe // chunk),
        in_specs=[
            pl.BlockSpec((rows * sub, LANES), lambda i, f, te, tv, ts: (ts[i], 0)),
            pl.BlockSpec((1, d, chunk), lambda i, f, te, tv, ts: (te[i], 0, f)),
            pl.BlockSpec((1, d, chunk), lambda i, f, te, tv, ts: (te[i], 0, f)),
            pl.BlockSpec((1, chunk, d), lambda i, f, te, tv, ts: (te[i], f, 0)),
        ],
        out_specs=pl.BlockSpec((rows * sub, LANES), lambda i, f, te, tv, ts: (i, 0)),
        scratch_shapes=[
            pltpu.VMEM((rows, d), bf16),
            pltpu.VMEM((rows, d), f32),
        ],
    )
    return pl.pallas_call(
        _moe_gemm_kernel,
        grid_spec=grid_spec,
        out_shape=jax.ShapeDtypeStruct((n_tiles * rows * sub, LANES), f32),
        compiler_params=_params("arbitrary", "arbitrary"),
        name="moe_gemm",
    )(tile_expert, tile_valid, tile_src, x_tiles, wg, wu, wd)


def _combine_kernel(pos_ref, h_ref, gate_ref, g_ref, y_hbm, o_ref, ybuf_ref, sem):
    i = pl.program_id(0)
    rows, d = h_ref.shape
    sub = d // LANES
    slot = i % 2

    def gather_rows(step, buf):
        base = step * (TOP_K * rows)

        def trip(c, carry):
            for u in range(DMA_UNROLL):
                r = c * DMA_UNROLL + u
                for k in range(TOP_K):
                    src = pl.multiple_of(pos_ref[base + k * rows + r] * sub, sub)
                    pltpu.make_async_copy(y_hbm.at[pl.ds(src, sub), :],
                                          ybuf_ref.at[buf, k, pl.ds(pl.multiple_of(r * sub, sub), sub), :],
                                          sem.at[buf]).start(priority=k % 2)
            return carry
        lax.fori_loop(0, rows // DMA_UNROLL, trip, 0)

    @pl.when(i == 0)
    def _():
        gather_rows(0, 0)

    @pl.when(i + 1 < pl.num_programs(0))
    def _():
        gather_rows(i + 1, 1 - slot)

    for k in range(TOP_K):
        pltpu.make_async_copy(y_hbm.at[pl.ds(0, rows * sub), :], ybuf_ref.at[slot, k], sem.at[slot]).wait()
    gate = gate_ref[...]
    chunks = []
    sumsq = jnp.zeros((rows, 1), f32)
    for c in range(sub):
        hc = h_ref[:, c * LANES:(c + 1) * LANES]
        for k in range(TOP_K):
            hc = hc + gate[:, k:k + 1] * _load_token_tiles(ybuf_ref.at[slot, k], rows, sub, c)
        sumsq = sumsq + jnp.sum(hc * hc, axis=-1, keepdims=True)
        chunks.append(hc)
    scale = lax.rsqrt(sumsq / d + EPS)
    for c in range(sub):
        o_ref[:, c * LANES:(c + 1) * LANES] = chunks[c] * scale * g_ref[:, c * LANES:(c + 1) * LANES]


def _moe_combine(pos, h, gates, g_final, y_tiles, rows):
    t, d = h.shape
    sub = d // LANES
    grid_spec = pltpu.PrefetchScalarGridSpec(
        num_scalar_prefetch=1,
        grid=(t // rows,),
        in_specs=[
            pl.BlockSpec((rows, d), lambda i, ps: (i, 0)),
            pl.BlockSpec((rows, TOP_K), lambda i, ps: (i, 0)),
            pl.BlockSpec((1, d), lambda i, ps: (0, 0)),
            pl.BlockSpec(memory_space=pl.ANY),
        ],
        out_specs=pl.BlockSpec((rows, d), lambda i, ps: (i, 0)),
        scratch_shapes=[pltpu.VMEM((2, TOP_K, rows * sub, LANES), f32), pltpu.SemaphoreType.DMA((2,))],
    )
    return pl.pallas_call(
        _combine_kernel,
        grid_spec=grid_spec,
        out_shape=jax.ShapeDtypeStruct((t, d), f32),
        compiler_params=_params("arbitrary", disable_bounds_checks=True),
        name="moe_combine",
    )(pos, h, gates, g_final, y_tiles)


def _routing_plan(idx, n_exp, rows, disp_rows, comb_rows):
    t = idx.shape[0]
    pairs = t * TOP_K
    n_tiles = -(-pairs // rows) + n_exp
    e_flat = idx.T.reshape(pairs)
    onehot = (e_flat[:, None] == jnp.arange(n_exp, dtype=jnp.int32)[None, :]).astype(jnp.int32)
    csum = jnp.cumsum(onehot, axis=0)
    counts = csum[-1]
    rank = jnp.sum(onehot * csum, axis=1) - 1
    padded = ((counts + rows - 1) // rows) * rows
    pend = jnp.cumsum(padded)
    poff = pend - padded
    pos = poff[e_flat] + rank
    tile_start = jnp.arange(n_tiles, dtype=jnp.int32) * rows
    tile_expert = jnp.minimum(jnp.sum((tile_start[:, None] >= pend[None, :]).astype(jnp.int32), axis=1),
                              n_exp - 1)
    tile_valid = (tile_start < pend[-1]).astype(jnp.int32)
    last_valid = jnp.maximum(jnp.sum(tile_valid) - 1, 0)
    tile_expert = jnp.where(tile_valid > 0, tile_expert, tile_expert[last_valid])
    tile_src = jnp.minimum(jnp.arange(n_tiles, dtype=jnp.int32), last_valid)

    def by_step(step_rows):
        return pos.reshape(TOP_K, t // step_rows, step_rows).transpose(1, 0, 2).reshape(pairs)

    return (tile_expert, tile_valid, tile_src, poff + counts, pend, n_tiles * rows,
            by_step(disp_rows), by_step(comb_rows))


def _rope_tables(s):
    half = ROT_DIM // 2
    inv = ROPE_THETA ** (-jnp.arange(half, dtype=f32) * 2.0 / ROT_DIM)
    ang = jnp.arange(s).astype(f32)[:, None] * inv[None, :]
    cos, sin = jnp.cos(ang), jnp.sin(ang)
    pad = HEAD_DIM - ROT_DIM
    cos_t = jnp.concatenate([cos, cos, jnp.ones((s, pad), f32)], axis=-1)
    sin_t = jnp.concatenate([-sin, sin, jnp.zeros((s, pad), f32)], axis=-1)
    return cos_t, sin_t


def kernel(x, attn_norm, ffn_norm, pool_w, pool_scale, wq, wo, kv_norm, wk, wv, ffn_w_gate, ffn_w_up,
           ffn_w_down, router_w, moe_w_gate, moe_w_up, moe_w_down, final_norm):
    b, s, d = x.shape
    t = b * s
    n_exp = router_w.shape[-1]
    assert attn_norm.shape[0] == 2 and pool_w.shape[0] == 1 and wq.shape[0] == 1 and router_w.shape[0] == 1
    tok_rows = 1024
    ffn_rows = 512
    moe_chunk = 512
    moe_rows = 1024
    comb_rows = 512
    assert s % tok_rows == 0 and d % (HEAD_DIM * 2) == 0 and t % comb_rows == 0
    assert moe_w_gate.shape[-1] % moe_chunk == 0 and moe_w_gate.shape[-1] // moe_chunk >= 2
    assert d % (8 * LANES) == 0 and moe_rows % tok_rows == 0

    row = lambda v: v.reshape(1, -1)

    h1, hn1 = _pool_mixer(x, row(attn_norm[0]), pool_w[0].astype(bf16), row(pool_scale[0]),
                          row(ffn_norm[0]), tok_rows)
    h2 = _dense_ffn(h1.reshape(t, d), hn1.reshape(t, d), ffn_w_gate[0].astype(bf16),
                    ffn_w_up[0].astype(bf16), ffn_w_down[0].astype(bf16), ffn_rows, 512)

    cos_t, sin_t = _rope_tables(s)
    qt, k, vt, kmean = _qkv_proj(h2.reshape(b, s, d), row(attn_norm[1]), row(kv_norm), wq[0].astype(bf16),
                                 wk.astype(bf16), wv.astype(bf16), cos_t, sin_t, tok_rows)
    o = _moba_attention(qt, k, vt, kmean)

    h3, idx, gates = _out_router(o.reshape(t, d), h2, wo[0].astype(bf16), row(ffn_norm[1]), router_w[0],
                                 tok_rows)

    (tile_expert, tile_valid, tile_src, pad_lo, pad_hi, n_rows, pos_disp,
     pos_comb) = _routing_plan(idx, n_exp, moe_rows, tok_rows, comb_rows)
    x_tiles = _moe_dispatch(pos_disp, pad_lo, pad_hi, h3, row(ffn_norm[1]), n_rows, tok_rows)
    y_tiles = _moe_gemm(tile_expert, tile_valid, tile_src, x_tiles, moe_w_gate[0], moe_w_up[0],
                        moe_w_down[0], moe_rows, moe_chunk)
    out = _moe_combine(pos_comb, h3, gates, row(final_norm), y_tiles, comb_rows)
    return out.reshape(b, s, d)
```

```python
import functools

import jax
import jax.numpy as jnp
from jax import lax
from jax.experimental import pallas as pl
from jax.experimental.pallas import tpu as pltpu

EPS = 1e-6
POOL_WINDOWS = (2, 4, 8, 16)
POOL_HALO = 16
LANES = 128
HEAD_DIM = 128
BLOCK = 256
TOP_BLOCKS = 3
ROT_DIM = HEAD_DIM // 4
ROPE_THETA = 500000.0
TOP_K = 2

VMEM_LIMIT = 56 * 1024 * 1024
DMA_UNROLL = 8
ZERO_ROWS = 8
NEG_INF = float("-inf")
POS_INF = float("inf")
MXU_LAG = 8
ONES_ROWS = 16
LOG2_E = 1.4426950408889634

f32 = jnp.float32
bf16 = jnp.bfloat16


def _resident(shape):
    n = len(shape)
    return pl.BlockSpec(shape, lambda *_: (0,) * n, pipeline_mode=pl.Buffered(1))


def _rms_scale(xf):
    return lax.rsqrt(jnp.mean(xf * xf, axis=-1, keepdims=True) + EPS)


def _silu(g):
    return g * (1.0 / (1.0 + jnp.exp(-g)))


def _store_token_tiles(ref, x):
    rows, d = x.shape
    sub = d // LANES
    for c in range(sub):
        ref[pl.ds(c, rows, stride=sub), :] = x[:, c * LANES:(c + 1) * LANES]


def _load_token_tiles(ref, rows, sub, chunk):
    return ref[pl.ds(chunk, rows, stride=sub), :]


def _params(*sem, **kw):
    return pltpu.CompilerParams(dimension_semantics=sem, vmem_limit_bytes=VMEM_LIMIT, **kw)


def _pool_kernel(x_ref, halo_ref, g_attn_ref, w_ref, scale_ref, g_ffn_ref, h_ref, hn_ref, ext_ref):
    i = pl.program_id(1)
    rows = x_ref.shape[1]
    d = x_ref.shape[2]
    c = d // len(POOL_WINDOWS)
    g_attn = g_attn_ref[...]
    x = x_ref[0]
    xn = x * _rms_scale(x) * g_attn
    halo = halo_ref[0]
    halo = halo * _rms_scale(halo) * g_attn
    ext_ref[0:POOL_HALO, :] = jnp.where(i == 0, 0.0, halo)
    ext_ref[POOL_HALO:, :] = xn
    t = i * rows + lax.broadcasted_iota(jnp.int32, (rows, 1), 0)
    ys = []
    for g, w in enumerate(POOL_WINDOWS):
        c0 = g * c
        s = ext_ref[:, c0:c0 + c]
        shift = 1
        while shift < w:
            s = s + pltpu.roll(s, shift, 0)
            shift *= 2
        cnt = jnp.minimum(t + 1, w).astype(f32)
        pooled = s[POOL_HALO:, :] / cnt - xn[:, c0:c0 + c]
        ys.append(jnp.dot(pooled.astype(bf16), w_ref[g], preferred_element_type=f32))
    y = jnp.concatenate(ys, axis=-1) * scale_ref[...]
    h = x + y
    h_ref[0] = h
    hn_ref[0] = (h * _rms_scale(h) * g_ffn_ref[...]).astype(bf16)


def _pool_mixer(x, g_attn, w_pool, scale, g_ffn, rows):
    b, s, d = x.shape
    c = d // len(POOL_WINDOWS)
    hb = rows // POOL_HALO
    return pl.pallas_call(
        _pool_kernel,
        grid=(b, s // rows),
        in_specs=[
            pl.BlockSpec((1, rows, d), lambda bi, i: (bi, i, 0)),
            pl.BlockSpec((1, POOL_HALO, d), lambda bi, i: (bi, jnp.maximum(i * hb - 1, 0), 0)),
            _resident((1, d)),
            _resident((len(POOL_WINDOWS), c, c)),
            _resident((1, d)),
            _resident((1, d)),
        ],
        out_specs=[
            pl.BlockSpec((1, rows, d), lambda bi, i: (bi, i, 0)),
            pl.BlockSpec((1, rows, d), lambda bi, i: (bi, i, 0)),
        ],
        out_shape=[jax.ShapeDtypeStruct((b, s, d), f32), jax.ShapeDtypeStruct((b, s, d), bf16)],
        scratch_shapes=[pltpu.VMEM((rows + POOL_HALO, d), f32)],
        compiler_params=_params("arbitrary", "arbitrary"),
        name="pool_mixer",
    )(x, x, g_attn, w_pool, scale, g_ffn)


def _ffn_chunks(f, size):
    out, c0 = [], 0
    while c0 < f:
        out.append((c0, min(size, f - c0)))
        c0 += size
    return out


def _dense_ffn_kernel(h_ref, hn_ref, wg_ref, wu_ref, wd_ref, o_ref, *, chunk):
    hn = hn_ref[...]
    acc = h_ref[...]
    for c0, cw in _ffn_chunks(wg_ref.shape[1], chunk):
        g = jnp.dot(hn, wg_ref[:, c0:c0 + cw], preferred_element_type=f32)
        u = jnp.dot(hn, wu_ref[:, c0:c0 + cw], preferred_element_type=f32)
        a = (_silu(g) * u).astype(bf16)
        acc = acc + jnp.dot(a, wd_ref[c0:c0 + cw, :], preferred_element_type=f32)
    o_ref[...] = acc


def _dense_ffn(h, hn, wg, wu, wd, rows, chunk):
    t, d = h.shape
    f = wg.shape[1]
    return pl.pallas_call(
        functools.partial(_dense_ffn_kernel, chunk=chunk),
        grid=(t // rows,),
        in_specs=[
            pl.BlockSpec((rows, d), lambda i: (i, 0)),
            pl.BlockSpec((rows, d), lambda i: (i, 0)),
            _resident((d, f)),
            _resident((d, f)),
            _resident((f, d)),
        ],
        out_specs=pl.BlockSpec((rows, d), lambda i: (i, 0)),
        out_shape=jax.ShapeDtypeStruct((t, d), f32),
        compiler_params=_params("arbitrary"),
        name="dense_ffn",
    )(h, hn, wg, wu, wd)


def _rope(x, cos, sin_signed, lane):
    d = x.shape[-1]
    half = ROT_DIM // 2
    partner = jnp.where(lane < half, pltpu.roll(x, d - half, 1), pltpu.roll(x, half, 1))
    return x * cos + partner * sin_signed


def _qkv_kernel(h_ref, g_q_ref, g_kv_ref, wq_ref, wk_ref, wv_ref, cos_ref, sin_ref,
                qt_ref, k_ref, vt_ref, kmean_ref):
    i = pl.program_id(1)
    rows, d = h_ref.shape[1], h_ref.shape[2]
    heads = d // HEAD_DIM
    h = h_ref[0]
    xhat = h * _rms_scale(h)
    hq = (xhat * g_q_ref[...]).astype(bf16)
    hkv = (xhat * g_kv_ref[...]).astype(bf16)
    q = jnp.dot(hq, wq_ref[...], preferred_element_type=f32)
    k = jnp.dot(hkv, wk_ref[...], preferred_element_type=f32)
    v = jnp.dot(hkv, wv_ref[...], preferred_element_type=f32)
    cos = jnp.concatenate([cos_ref[...]] * heads, axis=-1)
    sin = jnp.concatenate([sin_ref[...]] * heads, axis=-1)
    lane = lax.broadcasted_iota(jnp.int32, (rows, d), 1) % HEAD_DIM
    q = _rope(q, cos, sin, lane) * (HEAD_DIM ** -0.5 * LOG2_E)
    k = _rope(k, cos, sin, lane)
    qt_ref[0] = q.T.astype(bf16)
    k_ref[0] = k.astype(bf16)
    vt_ref[0] = v.T.astype(bf16)

    nb = kmean_ref.shape[1]
    blk_iota = lax.broadcasted_iota(jnp.int32, (nb, d), 0)

    @pl.when(i == 0)
    def _():
        kmean_ref[0] = jnp.zeros((nb, d), f32)

    km = kmean_ref[0]
    for j in range(rows // BLOCK):
        mean_j = jnp.mean(k[j * BLOCK:(j + 1) * BLOCK, :], axis=0, keepdims=True)
        km = jnp.where(blk_iota == i * (rows // BLOCK) + j, mean_j, km)
    kmean_ref[0] = km


def _qkv_proj(h, g_q, g_kv, wq, wk, wv, cos, sin, rows):
    b, s, d = h.shape
    nb = s // BLOCK
    return pl.pallas_call(
        _qkv_kernel,
        grid=(b, s // rows),
        in_specs=[
            pl.BlockSpec((1, rows, d), lambda bi, i: (bi, i, 0)),
            _resident((1, d)),
            _resident((1, d)),
            _resident((d, d)),
            _resident((d, d)),
            _resident((d, d)),
            pl.BlockSpec((rows, HEAD_DIM), lambda bi, i: (i, 0)),
            pl.BlockSpec((rows, HEAD_DIM), lambda bi, i: (i, 0)),
        ],
        out_specs=[
            pl.BlockSpec((1, d, rows), lambda bi, i: (bi, 0, i)),
            pl.BlockSpec((1, rows, d), lambda bi, i: (bi, i, 0)),
            pl.BlockSpec((1, d, rows), lambda bi, i: (bi, 0, i)),
            pl.BlockSpec((1, nb, d), lambda bi, i: (bi, 0, 0)),
        ],
        out_shape=[
            jax.ShapeDtypeStruct((b, d, s), bf16),
            jax.ShapeDtypeStruct((b, s, d), bf16),
            jax.ShapeDtypeStruct((b, d, s), bf16),
            jax.ShapeDtypeStruct((b, nb, d), f32),
        ],
        compiler_params=_params("arbitrary", "arbitrary"),
        name="qkv_proj",
    )(h, g_q, g_kv, wq, wk, wv, cos, sin)


def _moba_kernel(qt_ref, k_ref, vt_ref, kmean_ref, o_ref, *scratch, heads):
    sel_refs, m_refs, l_refs, acc_refs = (scratch[i * heads:(i + 1) * heads] for i in range(4))
    j = pl.program_id(1)
    nb = kmean_ref.shape[1]
    hd = HEAD_DIM
    blk = lax.broadcasted_iota(jnp.int32, (nb, BLOCK), 0)
    past = blk < j
    kpos = lax.broadcasted_iota(jnp.int32, (BLOCK, BLOCK), 0)
    qpos = lax.broadcasted_iota(jnp.int32, (BLOCK, BLOCK), 1)
    causal = kpos <= qpos
    start = pl.multiple_of(j * BLOCK, BLOCK)

    def q_head(h):
        return qt_ref[0, h * hd:(h + 1) * hd, :]

    ones_rows = jnp.ones((ONES_ROWS, BLOCK), bf16)

    def score(h, off):
        return jnp.dot(k_ref[0, pl.ds(off, BLOCK), h * hd:(h + 1) * hd], q_head(h), preferred_element_type=f32)

    def weighted_values(h, p, off):
        vt_ones = jnp.concatenate([vt_ref[0, h * hd:(h + 1) * hd, pl.ds(off, BLOCK)], ones_rows], axis=0)
        out = jnp.dot(vt_ones, p, preferred_element_type=f32)
        return out[:hd], out[hd:hd + 1]

    def key_block(off, probs, accumulate):
        pending = {}
        for step in range(heads + MXU_LAG):
            if step < heads:
                pending[step] = probs(step, score(step, off))
            if step >= MXU_LAG:
                h = step - MXU_LAG
                p, extra = pending.pop(h)
                accumulate(h, extra, *weighted_values(h, p, off))

    for h in range(heads):
        km = kmean_ref[0, :, h * hd:(h + 1) * hd]
        km_hi = km.astype(bf16)
        km_lo = (km - km_hi.astype(f32)).astype(bf16)
        gate = (jnp.dot(km_hi, q_head(h), preferred_element_type=f32)
                + jnp.dot(km_lo, q_head(h), preferred_element_type=f32))
        gate = jnp.where(past, gate, NEG_INF)
        rank = jnp.zeros((nb, BLOCK), jnp.int32)
        for m in range(nb):
            gm = gate[m:m + 1, :]
            rank = rank + jnp.where(gm > gate, 1, jnp.where(gm == gate, jnp.where(blk > m, 1, 0), 0))
        sel_refs[h][...] = jnp.where(past, jnp.where(rank < TOP_BLOCKS, 1.0, 0.0), 0.0)

    def own_probs(h, s):
        s = jnp.where(causal, s, NEG_INF)
        m0 = jnp.max(s, axis=0, keepdims=True)
        m_refs[h][...] = m0
        return jnp.exp2((s - m0).astype(bf16)), None

    def own_accumulate(h, _, pv, psum):
        acc_refs[h][...] = pv
        l_refs[h][...] = psum

    key_block(start, own_probs, own_accumulate)

    def body(n, carry):
        def probs(h, s):
            chosen = sel_refs[h][pl.ds(n, 1), :] > 0.0
            m_old = m_refs[h][...]
            m_new = jnp.maximum(m_old, jnp.where(chosen, jnp.max(s, axis=0, keepdims=True), NEG_INF))
            m_refs[h][...] = m_new
            p = jnp.exp2((s - jnp.where(chosen, m_new, POS_INF)).astype(bf16))
            return p, jnp.exp2(m_old - m_new)

        def accumulate(h, alpha, pv, psum):
            acc_refs[h][...] = alpha * acc_refs[h][...] + pv
            l_refs[h][...] = alpha * l_refs[h][...] + psum

        key_block(pl.multiple_of(n * BLOCK, BLOCK), probs, accumulate)
        return carry

    lax.fori_loop(0, j, body, 0)
    for h in range(heads):
        o_ref[0, :, h * hd:(h + 1) * hd] = (acc_refs[h][...] / l_refs[h][...]).T.astype(o_ref.dtype)


def _moba_attention(qt, k, vt, kmean):
    b, d, s = qt.shape
    heads = d // HEAD_DIM
    nb = s // BLOCK
    return pl.pallas_call(
        functools.partial(_moba_kernel, heads=heads),
        grid=(b, nb),
        in_specs=[
            pl.BlockSpec((1, d, BLOCK), lambda bi, j: (bi, 0, j)),
            pl.BlockSpec((1, s, d), lambda bi, j: (bi, 0, 0)),
            pl.BlockSpec((1, d, s), lambda bi, j: (bi, 0, 0)),
            pl.BlockSpec((1, nb, d), lambda bi, j: (bi, 0, 0)),
        ],
        out_specs=pl.BlockSpec((1, BLOCK, d), lambda bi, j: (bi, j, 0)),
        out_shape=jax.ShapeDtypeStruct((b, s, d), bf16),
        scratch_shapes=(
            [pltpu.VMEM((nb, BLOCK), f32)] * heads
            + [pltpu.VMEM((1, BLOCK), f32)] * (2 * heads)
            + [pltpu.VMEM((HEAD_DIM, BLOCK), f32)] * heads
        ),
        compiler_params=_params("arbitrary", "arbitrary"),
        name="moba_attn",
    )(qt, k, vt, kmean)


def _out_router_kernel(o_ref, h_ref, wo_ref, g_ref, wr_ref, h_out_ref, idx_ref, gate_ref):
    rows = o_ref.shape[0]
    n_exp = wr_ref.shape[1]
    h = h_ref[...] + jnp.dot(o_ref[...], wo_ref[...], preferred_element_type=f32)
    h_out_ref[...] = h
    hn = h * _rms_scale(h) * g_ref[...]
    wr = wr_ref[...]
    w_hi = wr.astype(bf16)
    w_lo = (wr - w_hi.astype(f32)).astype(bf16)
    x_hi = hn.astype(bf16)
    x_lo = (hn - x_hi.astype(f32)).astype(bf16)
    both = jnp.dot(x_hi, jnp.concatenate([w_hi, w_lo], axis=1), preferred_element_type=f32)
    logits = both[:, :n_exp] + both[:, n_exp:] + jnp.dot(x_lo, w_hi, preferred_element_type=f32)
    e = lax.broadcasted_iota(jnp.int32, (rows, n_exp), 1)
    v1 = jnp.max(logits, axis=-1, keepdims=True)
    i1 = jnp.min(jnp.where(logits == v1, e, n_exp), axis=-1, keepdims=True)
    rest = jnp.where(e == i1, NEG_INF, logits)
    v2 = jnp.max(rest, axis=-1, keepdims=True)
    i2 = jnp.min(jnp.where(rest == v2, e, n_exp), axis=-1, keepdims=True)
    e2 = jnp.exp(v2 - v1)
    denom = 1.0 + e2
    col = lax.broadcasted_iota(jnp.int32, (rows, TOP_K), 1)
    idx_ref[...] = jnp.where(col == 0, i1, i2)
    gate_ref[...] = jnp.where(col == 0, 1.0 / denom, e2 / denom)


def _out_router(o, h, wo, g_ffn, w_router, rows):
    t, d = h.shape
    n_exp = w_router.shape[1]
    return pl.pallas_call(
        _out_router_kernel,
        grid=(t // rows,),
        in_specs=[
            pl.BlockSpec((rows, d), lambda i: (i, 0)),
            pl.BlockSpec((rows, d), lambda i: (i, 0)),
            _resident((d, d)),
            _resident((1, d)),
            _resident((d, n_exp)),
        ],
        out_specs=[
            pl.BlockSpec((rows, d), lambda i: (i, 0)),
            pl.BlockSpec((rows, TOP_K), lambda i: (i, 0)),
            pl.BlockSpec((rows, TOP_K), lambda i: (i, 0)),
        ],
        out_shape=[
            jax.ShapeDtypeStruct((t, d), f32),
            jax.ShapeDtypeStruct((t, TOP_K), jnp.int32),
            jax.ShapeDtypeStruct((t, TOP_K), f32),
        ],
        compiler_params=_params("arbitrary"),
        name="out_router",
    )(o, h, wo, g_ffn, w_router)


def _dispatch_kernel(pos_ref, pad_lo_ref, pad_hi_ref, h_ref, g_ref, x_hbm, buf_ref, zero_ref, sem, zsem):
    i = pl.program_id(0)
    n_steps = pl.num_programs(0)
    rows, d = h_ref.shape
    sub = d // LANES
    slot = i % 2

    def staged_copies_wait(buf):
        for _ in range(TOP_K):
            pltpu.make_async_copy(buf_ref.at[buf], x_hbm.at[pl.ds(0, rows * sub), :], sem.at[buf]).wait()

    @pl.when(i == 0)
    def _():
        zero_ref[...] = jnp.zeros(zero_ref.shape, f32)
        for e in range(pad_lo_ref.shape[0]):
            lo, hi = pad_lo_ref[e], pad_hi_ref[e]
            mid = jnp.minimum(((lo + ZERO_ROWS - 1) // ZERO_ROWS) * ZERO_ROWS, hi)

            def one_row(r):
                return pltpu.make_async_copy(zero_ref.at[pl.ds(0, sub), :],
                                             x_hbm.at[pl.ds(pl.multiple_of(r * sub, sub), sub), :], zsem)

            def row_group(c):
                dst = pl.multiple_of(c * (ZERO_ROWS * sub), ZERO_ROWS * sub)
                return pltpu.make_async_copy(zero_ref, x_hbm.at[pl.ds(dst, ZERO_ROWS * sub), :], zsem)

            def run(copy_of, first, stop):
                def start(r, carry):
                    copy_of(r).start()
                    return carry

                def wait(r, carry):
                    copy_of(r).wait()
                    return carry
                lax.fori_loop(first, stop, start, 0)
                lax.fori_loop(first, stop, wait, 0)

            run(one_row, lo, mid)
            run(row_group, mid // ZERO_ROWS, hi // ZERO_ROWS)

        buf_ref[1] = jnp.zeros(buf_ref.shape[1:], f32)
        tail_lo = pad_hi_ref[pad_hi_ref.shape[0] - 1]
        n_tail = (x_hbm.shape[0] // sub - tail_lo) // rows

        def tail_copy(c):
            dst = pl.multiple_of((tail_lo + c * rows) * sub, sub)
            return pltpu.make_async_copy(buf_ref.at[1], x_hbm.at[pl.ds(dst, rows * sub), :], zsem)

        def tail_fill(c, carry):
            tail_copy(c).start()
            return carry

        def tail_drain(c, carry):
            tail_copy(c).wait()
            return carry
        lax.fori_loop(0, n_tail, tail_fill, 0)
        lax.fori_loop(0, n_tail, tail_drain, 0)

    @pl.when(i >= 2)
    def _():
        staged_copies_wait(slot)

    h = h_ref[...]
    _store_token_tiles(buf_ref.at[slot], h * _rms_scale(h) * g_ref[...])
    base = i * (TOP_K * rows)

    def trip(c, carry):
        for u in range(DMA_UNROLL):
            r = c * DMA_UNROLL + u
            src = buf_ref.at[slot, pl.ds(pl.multiple_of(r * sub, sub), sub), :]
            for k in range(TOP_K):
                dst = pl.multiple_of(pos_ref[base + k * rows + r] * sub, sub)
                pltpu.make_async_copy(src, x_hbm.at[pl.ds(dst, sub), :], sem.at[slot]).start(priority=k % 2)
        return carry
    lax.fori_loop(0, rows // DMA_UNROLL, trip, 0)

    @pl.when(i == n_steps - 1)
    def _():
        staged_copies_wait(slot)

        @pl.when(n_steps >= 2)
        def _():
            staged_copies_wait(1 - slot)


def _moe_dispatch(pos_steps, pad_lo, pad_hi, h, g_ffn, n_rows, rows):
    t, d = h.shape
    sub = d // LANES
    grid_spec = pltpu.PrefetchScalarGridSpec(
        num_scalar_prefetch=3,
        grid=(t // rows,),
        in_specs=[
            pl.BlockSpec((rows, d), lambda i, ps, lo, hi: (i, 0)),
            pl.BlockSpec((1, d), lambda i, ps, lo, hi: (0, 0)),
        ],
        out_specs=pl.BlockSpec(memory_space=pl.ANY),
        scratch_shapes=[
            pltpu.VMEM((2, rows * sub, LANES), f32),
            pltpu.VMEM((ZERO_ROWS * sub, LANES), f32),
            pltpu.SemaphoreType.DMA((2,)),
            pltpu.SemaphoreType.DMA,
        ],
    )
    return pl.pallas_call(
        _dispatch_kernel,
        grid_spec=grid_spec,
        out_shape=jax.ShapeDtypeStruct((n_rows * sub, LANES), f32),
        compiler_params=_params("arbitrary", disable_bounds_checks=True),
        name="moe_dispatch",
    )(pos_steps, pad_lo, pad_hi, h, g_ffn)


def _moe_gemm_kernel(tile_expert_ref, tile_valid_ref, tile_src_ref, x_ref, wg_ref, wu_ref, wd_ref,
                     y_ref, xb_ref, acc_ref):
    i = pl.program_id(0)
    f = pl.program_id(1)
    rows, d = xb_ref.shape
    sub = d // LANES
    valid = tile_valid_ref[i] > 0

    def swiglu_chunk():
        xb = xb_ref[...]
        g = jnp.dot(xb, wg_ref[0].astype(bf16), preferred_element_type=f32)
        u = jnp.dot(xb, wu_ref[0].astype(bf16), preferred_element_type=f32)
        a = (_silu(g) * u).astype(bf16)
        return jnp.dot(a, wd_ref[0].astype(bf16), preferred_element_type=f32)

    @pl.when(jnp.logical_and(f == 0, valid))
    def _():
        for c in range(sub):
            xb_ref[:, c * LANES:(c + 1) * LANES] = _load_token_tiles(x_ref, rows, sub, c).astype(bf16)
        acc_ref[...] = swiglu_chunk()

    last = pl.num_programs(1) - 1

    @pl.when(jnp.logical_and(jnp.logical_and(f > 0, f < last), valid))
    def _():
        acc_ref[...] += swiglu_chunk()

    @pl.when(jnp.logical_and(f == last, valid))
    def _():
        _store_token_tiles(y_ref, acc_ref[...] + swiglu_chunk())

    @pl.when(jnp.logical_and(f == last, jnp.logical_not(valid)))
    def _():
        y_ref[...] = jnp.zeros(y_ref.shape, f32)


def _moe_gemm(tile_expert, tile_valid, tile_src, x_tiles, wg, wu, wd, rows, chunk):
    n_tiles = tile_expert.shape[0]
    _, d, fe = wg.shape
    sub = d // LANES
    grid_spec = pltpu.PrefetchScalarGridSpec(
        num_scalar_prefetch=3,
        grid=(n_tiles, fe // chunk),
        in_specs=[
            pl.BlockSpec((rows * sub, LANES), lambda i, f, te, tv, ts: (ts[i], 0)),
            pl.BlockSpec((1, d, chunk), lambda i, f, te, tv, ts: (te[i], 0, f)),
            pl.BlockSpec((1, d, chunk), lambda i, f, te, tv, ts: (te[i], 0, f)),
            pl.BlockSpec((1, chunk, d), lambda i, f, te, tv, ts: (te[i], f, 0)),
        ],
        out_specs=pl.BlockSpec((rows * sub, LANES), lambda i, f, te, tv, ts: (i, 0)),
        scratch_shapes=[
            pltpu.VMEM((rows, d), bf16),
            pltpu.VMEM((rows, d), f32),
        ],
    )
    return pl.pallas_call(
        _moe_gemm_kernel,
        grid_spec=grid_spec,
        out_shape=jax.ShapeDtypeStruct((n_tiles * rows * sub, LANES), f32),
        compiler_params=_params("arbitrary", "arbitrary"),
        name="moe_gemm",
    )(tile_expert, tile_valid, tile_src, x_tiles, wg, wu, wd)


def _combine_kernel(pos_ref, h_ref, gate_ref, g_ref, y_hbm, o_ref, ybuf_ref, sem):
    i = pl.program_id(0)
    rows, d = h_ref.shape
    sub = d // LANES
    slot = i % 2

    def gather_rows(step, buf):
        base = step * (TOP_K * rows)

        def trip(c, carry):
            for u in range(DMA_UNROLL):
                r = c * DMA_UNROLL + u
                for k in range(TOP_K):
                    src = pl.multiple_of(pos_ref[base + k * rows + r] * sub, sub)
                    pltpu.make_async_copy(y_hbm.at[pl.ds(src, sub), :],
                                          ybuf_ref.at[buf, k, pl.ds(pl.multiple_of(r * sub, sub), sub), :],
                                          sem.at[buf]).start(priority=k % 2)
            return carry
        lax.fori_loop(0, rows // DMA_UNROLL, trip, 0)

    @pl.when(i == 0)
    def _():
        gather_rows(0, 0)

    @pl.when(i + 1 < pl.num_programs(0))
    def _():
        gather_rows(i + 1, 1 - slot)

    for k in range(TOP_K):
        pltpu.make_async_copy(y_hbm.at[pl.ds(0, rows * sub), :], ybuf_ref.at[slot, k], sem.at[slot]).wait()
    gate = gate_ref[...]
    chunks = []
    sumsq = jnp.zeros((rows, 1), f32)
    for c in range(sub):
        hc = h_ref[:, c * LANES:(c + 1) * LANES]
        for k in range(TOP_K):
            hc = hc + gate[:, k:k + 1] * _load_token_tiles(ybuf_ref.at[slot, k], rows, sub, c)
        sumsq = sumsq + jnp.sum(hc * hc, axis=-1, keepdims=True)
        chunks.append(hc)
    scale = lax.rsqrt(sumsq / d + EPS)
    for c in range(sub):
        o_ref[:, c * LANES:(c + 1) * LANES] = chunks[c] * scale * g_ref[:, c * LANES:(c + 1) * LANES]


def _moe_combine(pos, h, gates, g_final, y_tiles, rows):
    t, d = h.shape
    sub = d // LANES
    grid_spec = pltpu.PrefetchScalarGridSpec(
        num_scalar_prefetch=1,
        grid=(t // rows,),
        in_specs=[
            pl.BlockSpec((rows, d), lambda i, ps: (i, 0)),
            pl.BlockSpec((rows, TOP_K), lambda i, ps: (i, 0)),
            pl.BlockSpec((1, d), lambda i, ps: (0, 0)),
            pl.BlockSpec(memory_space=pl.ANY),
        ],
        out_specs=pl.BlockSpec((rows, d), lambda i, ps: (i, 0)),
        scratch_shapes=[pltpu.VMEM((2, TOP_K, rows * sub, LANES), f32), pltpu.SemaphoreType.DMA((2,))],
    )
    return pl.pallas_call(
        _combine_kernel,
        grid_spec=grid_spec,
        out_shape=jax.ShapeDtypeStruct((t, d), f32),
        compiler_params=_params("arbitrary", disable_bounds_checks=True),
        name="moe_combine",
    )(pos, h, gates, g_final, y_tiles)


def _routing_plan(idx, n_exp, rows, disp_rows, comb_rows):
    t = idx.shape[0]
    pairs = t * TOP_K
    n_tiles = -(-pairs // rows) + n_exp
    e_flat = idx.T.reshape(pairs)
    onehot = (e_flat[:, None] == jnp.arange(n_exp, dtype=jnp.int32)[None, :]).astype(jnp.int32)
    csum = jnp.cumsum(onehot, axis=0)
    counts = csum[-1]
    rank = jnp.sum(onehot * csum, axis=1) - 1
    padded = ((counts + rows - 1) // rows) * rows
    pend = jnp.cumsum(padded)
    poff = pend - padded
    pos = poff[e_flat] + rank
    tile_start = jnp.arange(n_tiles, dtype=jnp.int32) * rows
    tile_expert = jnp.minimum(jnp.sum((tile_start[:, None] >= pend[None, :]).astype(jnp.int32), axis=1),
                              n_exp - 1)
    tile_valid = (tile_start < pend[-1]).astype(jnp.int32)
    last_valid = jnp.maximum(jnp.sum(tile_valid) - 1, 0)
    tile_expert = jnp.where(tile_valid > 0, tile_expert, tile_expert[last_valid])
    tile_src = jnp.minimum(jnp.arange(n_tiles, dtype=jnp.int32), last_valid)

    def by_step(step_rows):
        return pos.reshape(TOP_K, t // step_rows, step_rows).transpose(1, 0, 2).reshape(pairs)

    return (tile_expert, tile_valid, tile_src, poff + counts, pend, n_tiles * rows,
            by_step(disp_rows), by_step(comb_rows))


def _rope_tables(s):
    half = ROT_DIM // 2
    inv = ROPE_THETA ** (-jnp.arange(half, dtype=f32) * 2.0 / ROT_DIM)
    ang = jnp.arange(s).astype(f32)[:, None] * inv[None, :]
    cos, sin = jnp.cos(ang), jnp.sin(ang)
    pad = HEAD_DIM - ROT_DIM
    cos_t = jnp.concatenate([cos, cos, jnp.ones((s, pad), f32)], axis=-1)
    sin_t = jnp.concatenate([-sin, sin, jnp.zeros((s, pad), f32)], axis=-1)
    return cos_t, sin_t


def kernel(x, attn_norm, ffn_norm, pool_w, pool_scale, wq, wo, kv_norm, wk, wv, ffn_w_gate, ffn_w_up,
           ffn_w_down, router_w, moe_w_gate, moe_w_up, moe_w_down, final_norm):
    b, s, d = x.shape
    t = b * s
    n_exp = router_w.shape[-1]
    assert attn_norm.shape[0] == 2 and pool_w.shape[0] == 1 and wq.shape[0] == 1 and router_w.shape[0] == 1
    tok_rows = 1024
    ffn_rows = 512
    moe_chunk = 512
    moe_rows = 1024
    comb_rows = 512
    assert s % tok_rows == 0 and d % (HEAD_DIM * 2) == 0 and t % comb_rows == 0
    assert moe_w_gate.shape[-1] % moe_chunk == 0 and moe_w_gate.shape[-1] // moe_chunk >= 2
    assert d % (8 * LANES) == 0 and moe_rows % tok_rows == 0

    row = lambda v: v.reshape(1, -1)

    h1, hn1 = _pool_mixer(x, row(attn_norm[0]), pool_w[0].astype(bf16), row(pool_scale[0]),
                          row(ffn_norm[0]), tok_rows)
    h2 = _dense_ffn(h1.reshape(t, d), hn1.reshape(t, d), ffn_w_gate[0].astype(bf16),
                    ffn_w_up[0].astype(bf16), ffn_w_down[0].astype(bf16), ffn_rows, 512)

    cos_t, sin_t = _rope_tables(s)
    qt, k, vt, kmean = _qkv_proj(h2.reshape(b, s, d), row(attn_norm[1]), row(kv_norm), wq[0].astype(bf16),
                                 wk.astype(bf16), wv.astype(bf16), cos_t, sin_t, tok_rows)
    o = _moba_attention(qt, k, vt, kmean)

    h3, idx, gates = _out_router(o.reshape(t, d), h2, wo[0].astype(bf16), row(ffn_norm[1]), router_w[0],
                                 tok_rows)

    (tile_expert, tile_valid, tile_src, pad_lo, pad_hi, n_rows, pos_disp,
     pos_comb) = _routing_plan(idx, n_exp, moe_rows, tok_rows, comb_rows)
    x_tiles = _moe_dispatch(pos_disp, pad_lo, pad_hi, h3, row(ffn_norm[1]), n_rows, tok_rows)
    y_tiles = _moe_gemm(tile_expert, tile_valid, tile_src, x_tiles, moe_w_gate[0], moe_w_up[0],
                        moe_w_down[0], moe_rows, moe_chunk)
    out = _moe_combine(pos_comb, h3, gates, row(final_norm), y_tiles, comb_rows)
    return out.reshape(b, s, d)
```

```python
import functools

import jax
import jax.numpy as jnp
from jax import lax
from jax.experimental import pallas as pl
from jax.experimental.pallas import tpu as pltpu

EPS = 1e-6
POOL_WINDOWS = (2, 4, 8, 16)
POOL_HALO = 16
LANES = 128
HEAD_DIM = 128
BLOCK = 256
TOP_BLOCKS = 3
ROT_DIM = HEAD_DIM // 4
ROPE_THETA = 500000.0
TOP_K = 2

VMEM_LIMIT = 56 * 1024 * 1024
DMA_UNROLL = 8
ZERO_ROWS = 8
NEG_INF = float("-inf")
POS_INF = float("inf")
MXU_LAG = 8
ONES_ROWS = 16
LOG2_E = 1.4426950408889634

f32 = jnp.float32
bf16 = jnp.bfloat16


def _resident(shape):
    n = len(shape)
    return pl.BlockSpec(shape, lambda *_: (0,) * n, pipeline_mode=pl.Buffered(1))


def _rms_scale(xf):
    return lax.rsqrt(jnp.mean(xf * xf, axis=-1, keepdims=True) + EPS)


def _silu(g):
    return g * (1.0 / (1.0 + jnp.exp(-g)))


def _store_token_tiles(ref, x):
    rows, d = x.shape
    sub = d // LANES
    for c in range(sub):
        ref[pl.ds(c, rows, stride=sub), :] = x[:, c * LANES:(c + 1) * LANES]


def _load_token_tiles(ref, rows, sub, chunk):
    return ref[pl.ds(chunk, rows, stride=sub), :]


def _params(*sem, **kw):
    return pltpu.CompilerParams(dimension_semantics=sem, vmem_limit_bytes=VMEM_LIMIT, **kw)


def _pool_kernel(x_ref, halo_ref, g_attn_ref, w_ref, scale_ref, g_ffn_ref, h_ref, hn_ref, ext_ref):
    i = pl.program_id(1)
    rows = x_ref.shape[1]
    d = x_ref.shape[2]
    c = d // len(POOL_WINDOWS)
    g_attn = g_attn_ref[...]
    x = x_ref[0]
    xn = x * _rms_scale(x) * g_attn
    halo = halo_ref[0]
    halo = halo * _rms_scale(halo) * g_attn
    ext_ref[0:POOL_HALO, :] = jnp.where(i == 0, 0.0, halo)
    ext_ref[POOL_HALO:, :] = xn
    t = i * rows + lax.broadcasted_iota(jnp.int32, (rows, 1), 0)
    ys = []
    for g, w in enumerate(POOL_WINDOWS):
        c0 = g * c
        s = ext_ref[:, c0:c0 + c]
        shift = 1
        while shift < w:
            s = s + pltpu.roll(s, shift, 0)
            shift *= 2
        cnt = jnp.minimum(t + 1, w).astype(f32)
        pooled = s[POOL_HALO:, :] / cnt - xn[:, c0:c0 + c]
        ys.append(jnp.dot(pooled.astype(bf16), w_ref[g], preferred_element_type=f32))
    y = jnp.concatenate(ys, axis=-1) * scale_ref[...]
    h = x + y
    h_ref[0] = h
    hn_ref[0] = (h * _rms_scale(h) * g_ffn_ref[...]).astype(bf16)


def _pool_mixer(x, g_attn, w_pool, scale, g_ffn, rows):
    b, s, d = x.shape
    c = d // len(POOL_WINDOWS)
    hb = rows // POOL_HALO
    return pl.pallas_call(
        _pool_kernel,
        grid=(b, s // rows),
        in_specs=[
            pl.BlockSpec((1, rows, d), lambda bi, i: (bi, i, 0)),
            pl.BlockSpec((1, POOL_HALO, d), lambda bi, i: (bi, jnp.maximum(i * hb - 1, 0), 0)),
            _resident((1, d)),
            _resident((len(POOL_WINDOWS), c, c)),
            _resident((1, d)),
            _resident((1, d)),
        ],
        out_specs=[
            pl.BlockSpec((1, rows, d), lambda bi, i: (bi, i, 0)),
            pl.BlockSpec((1, rows, d), lambda bi, i: (bi, i, 0)),
        ],
        out_shape=[jax.ShapeDtypeStruct((b, s, d), f32), jax.ShapeDtypeStruct((b, s, d), bf16)],
        scratch_shapes=[pltpu.VMEM((rows + POOL_HALO, d), f32)],
        compiler_params=_params("arbitrary", "arbitrary"),
        name="pool_mixer",
    )(x, x, g_attn, w_pool, scale, g_ffn)


def _ffn_chunks(f, size):
    out, c0 = [], 0
    while c0 < f:
        out.append((c0, min(size, f - c0)))
        c0 += size
    return out


def _dense_ffn_kernel(h_ref, hn_ref, wg_ref, wu_ref, wd_ref, o_ref, *, chunk):
    hn = hn_ref[...]
    acc = h_ref[...]
    for c0, cw in _ffn_chunks(wg_ref.shape[1], chunk):
        g = jnp.dot(hn, wg_ref[:, c0:c0 + cw], preferred_element_type=f32)
        u = jnp.dot(hn, wu_ref[:, c0:c0 + cw], preferred_element_type=f32)
        a = (_silu(g) * u).astype(bf16)
        acc = acc + jnp.dot(a, wd_ref[c0:c0 + cw, :], preferred_element_type=f32)
    o_ref[...] = acc


def _dense_ffn(h, hn, wg, wu, wd, rows, chunk):
    t, d = h.shape
    f = wg.shape[1]
    return pl.pallas_call(
        functools.partial(_dense_ffn_kernel, chunk=chunk),
        grid=(t // rows,),
        in_specs=[
            pl.BlockSpec((rows, d), lambda i: (i, 0)),
            pl.BlockSpec((rows, d), lambda i: (i, 0)),
            _resident((d, f)),
            _resident((d, f)),
            _resident((f, d)),
        ],
        out_specs=pl.BlockSpec((rows, d), lambda i: (i, 0)),
        out_shape=jax.ShapeDtypeStruct((t, d), f32),
        compiler_params=_params("arbitrary"),
        name="dense_ffn",
    )(h, hn, wg, wu, wd)


def _rope(x, cos, sin_signed, lane):
    d = x.shape[-1]
    half = ROT_DIM // 2
    partner = jnp.where(lane < half, pltpu.roll(x, d - half, 1), pltpu.roll(x, half, 1))
    return x * cos + partner * sin_signed


def _qkv_kernel(h_ref, g_q_ref, g_kv_ref, wq_ref, wk_ref, wv_ref, cos_ref, sin_ref,
                qt_ref, k_ref, vt_ref, kmean_ref):
    i = pl.program_id(1)
    rows, d = h_ref.shape[1], h_ref.shape[2]
    heads = d // HEAD_DIM
    h = h_ref[0]
    xhat = h * _rms_scale(h)
    hq = (xhat * g_q_ref[...]).astype(bf16)
    hkv = (xhat * g_kv_ref[...]).astype(bf16)
    q = jnp.dot(hq, wq_ref[...], preferred_element_type=f32)
    k = jnp.dot(hkv, wk_ref[...], preferred_element_type=f32)
    v = jnp.dot(hkv, wv_ref[...], preferred_element_type=f32)
    cos = jnp.concatenate([cos_ref[...]] * heads, axis=-1)
    sin = jnp.concatenate([sin_ref[...]] * heads, axis=-1)
    lane = lax.broadcasted_iota(jnp.int32, (rows, d), 1) % HEAD_DIM
    q = _rope(q, cos, sin, lane) * (HEAD_DIM ** -0.5 * LOG2_E)
    k = _rope(k, cos, sin, lane)
    qt_ref[0] = q.T.astype(bf16)
    k_ref[0] = k.astype(bf16)
    vt_ref[0] = v.T.astype(bf16)

    nb = kmean_ref.shape[1]
    blk_iota = lax.broadcasted_iota(jnp.int32, (nb, d), 0)

    @pl.when(i == 0)
    def _():
        kmean_ref[0] = jnp.zeros((nb, d), f32)

    km = kmean_ref[0]
    for j in range(rows // BLOCK):
        mean_j = jnp.mean(k[j * BLOCK:(j + 1) * BLOCK, :], axis=0, keepdims=True)
        km = jnp.where(blk_iota == i * (rows // BLOCK) + j, mean_j, km)
    kmean_ref[0] = km


def _qkv_proj(h, g_q, g_kv, wq, wk, wv, cos, sin, rows):
    b, s, d = h.shape
    nb = s // BLOCK
    return pl.pallas_call(
        _qkv_kernel,
        grid=(b, s // rows),
        in_specs=[
            pl.BlockSpec((1, rows, d), lambda bi, i: (bi, i, 0)),
            _resident((1, d)),
            _resident((1, d)),
            _resident((d, d)),
            _resident((d, d)),
            _resident((d, d)),
            pl.BlockSpec((rows, HEAD_DIM), lambda bi, i: (i, 0)),
            pl.BlockSpec((rows, HEAD_DIM), lambda bi, i: (i, 0)),
        ],
        out_specs=[
            pl.BlockSpec((1, d, rows), lambda bi, i: (bi, 0, i)),
            pl.BlockSpec((1, rows, d), lambda bi, i: (bi, i, 0)),
            pl.BlockSpec((1, d, rows), lambda bi, i: (bi, 0, i)),
            pl.BlockSpec((1, nb, d), lambda bi, i: (bi, 0, 0)),
        ],
        out_shape=[
            jax.ShapeDtypeStruct((b, d, s), bf16),
            jax.ShapeDtypeStruct((b, s, d), bf16),
            jax.ShapeDtypeStruct((b, d, s), bf16),
            jax.ShapeDtypeStruct((b, nb, d), f32),
        ],
        compiler_params=_params("arbitrary", "arbitrary"),
        name="qkv_proj",
    )(h, g_q, g_kv, wq, wk, wv, cos, sin)


def _moba_kernel(qt_ref, k_ref, vt_ref, kmean_ref, o_ref, *scratch, heads):
    sel_refs, m_refs, l_refs, acc_refs = (scratch[i * heads:(i + 1) * heads] for i in range(4))
    j = pl.program_id(1)
    nb = kmean_ref.shape[1]
    hd = HEAD_DIM
    blk = lax.broadcasted_iota(jnp.int32, (nb, BLOCK), 0)
    past = blk < j
    kpos = lax.broadcasted_iota(jnp.int32, (BLOCK, BLOCK), 0)
    qpos = lax.broadcasted_iota(jnp.int32, (BLOCK, BLOCK), 1)
    causal = kpos <= qpos
    start = pl.multiple_of(j * BLOCK, BLOCK)

    def q_head(h):
        return qt_ref[0, h * hd:(h + 1) * hd, :]

    ones_rows = jnp.ones((ONES_ROWS, BLOCK), bf16)

    def score(h, off):
        return jnp.dot(k_ref[0, pl.ds(off, BLOCK), h * hd:(h + 1) * hd], q_head(h), preferred_element_type=f32)

    def weighted_values(h, p, off):
        vt_ones = jnp.concatenate([vt_ref[0, h * hd:(h + 1) * hd, pl.ds(off, BLOCK)], ones_rows], axis=0)
        out = jnp.dot(vt_ones, p, preferred_element_type=f32)
        return out[:hd], out[hd:hd + 1]

    def key_block(off, probs, accumulate):
        pending = {}
        for step in range(heads + MXU_LAG):
            if step < heads:
                pending[step] = probs(step, score(step, off))
            if step >= MXU_LAG:
                h = step - MXU_LAG
                p, extra = pending.pop(h)
                accumulate(h, extra, *weighted_values(h, p, off))

    for h in range(heads):
        km = kmean_ref[0, :, h * hd:(h + 1) * hd]
        km_hi = km.astype(bf16)
        km_lo = (km - km_hi.astype(f32)).astype(bf16)
        gate = (jnp.dot(km_hi, q_head(h), preferred_element_type=f32)
                + jnp.dot(km_lo, q_head(h), preferred_element_type=f32))
        gate = jnp.where(past, gate, NEG_INF)
        rank = jnp.zeros((nb, BLOCK), jnp.int32)
        for m in range(nb):
            gm = gate[m:m + 1, :]
            rank = rank + jnp.where(gm > gate, 1, jnp.where(gm == gate, jnp.where(blk > m, 1, 0), 0))
        sel_refs[h][...] = jnp.where(past, jnp.where(rank < TOP_BLOCKS, 1.0, 0.0), 0.0)

    def own_probs(h, s):
        s = jnp.where(causal, s, NEG_INF)
        m0 = jnp.max(s, axis=0, keepdims=True)
        m_refs[h][...] = m0
        return jnp.exp2((s - m0).astype(bf16)), None

    def own_accumulate(h, _, pv, psum):
        acc_refs[h][...] = pv
        l_refs[h][...] = psum

    key_block(start, own_probs, own_accumulate)

    def body(n, carry):
        def probs(h, s):
            chosen = sel_refs[h][pl.ds(n, 1), :] > 0.0
            m_old = m_refs[h][...]
            m_new = jnp.maximum(m_old, jnp.where(chosen, jnp.max(s, axis=0, keepdims=True), NEG_INF))
            m_refs[h][...] = m_new
            p = jnp.exp2((s - jnp.where(chosen, m_new, POS_INF)).astype(bf16))
            return p, jnp.exp2(m_old - m_new)

        def accumulate(h, alpha, pv, psum):
            acc_refs[h][...] = alpha * acc_refs[h][...] + pv
            l_refs[h][...] = alpha * l_refs[h][...] + psum

        key_block(pl.multiple_of(n * BLOCK, BLOCK), probs, accumulate)
        return carry

    lax.fori_loop(0, j, body, 0)
    for h in range(heads):
        o_ref[0, :, h * hd:(h + 1) * hd] = (acc_refs[h][...] / l_refs[h][...]).T.astype(o_ref.dtype)


def _moba_attention(qt, k, vt, kmean):
    b, d, s = qt.shape
    heads = d // HEAD_DIM
    nb = s // BLOCK
    return pl.pallas_call(
        functools.partial(_moba_kernel, heads=heads),
        grid=(b, nb),
        in_specs=[
            pl.BlockSpec((1, d, BLOCK), lambda bi, j: (bi, 0, j)),
            pl.BlockSpec((1, s, d), lambda bi, j: (bi, 0, 0)),
            pl.BlockSpec((1, d, s), lambda bi, j: (bi, 0, 0)),
            pl.BlockSpec((1, nb, d), lambda bi, j: (bi, 0, 0)),
        ],
        out_specs=pl.BlockSpec((1, BLOCK, d), lambda bi, j: (bi, j, 0)),
        out_shape=jax.ShapeDtypeStruct((b, s, d), bf16),
        scratch_shapes=(
            [pltpu.VMEM((nb, BLOCK), f32)] * heads
            + [pltpu.VMEM((1, BLOCK), f32)] * (2 * heads)
            + [pltpu.VMEM((HEAD_DIM, BLOCK), f32)] * heads
        ),
        compiler_params=_params("arbitrary", "arbitrary"),
        name="moba_attn",
    )(qt, k, vt, kmean)


def _out_router_kernel(o_ref, h_ref, wo_ref, g_ref, wr_ref, h_out_ref, idx_ref, gate_ref):
    rows = o_ref.shape[0]
    n_exp = wr_ref.shape[1]
    h = h_ref[...] + jnp.dot(o_ref[...], wo_ref[...], preferred_element_type=f32)
    h_out_ref[...] = h
    hn = h * _rms_scale(h) * g_ref[...]
    wr = wr_ref[...]
    w_hi = wr.astype(bf16)
    w_lo = (wr - w_hi.astype(f32)).astype(bf16)
    x_hi = hn.astype(bf16)
    x_lo = (hn - x_hi.astype(f32)).astype(bf16)
    both = jnp.dot(x_hi, jnp.concatenate([w_hi, w_lo], axis=1), preferred_element_type=f32)
    logits = both[:, :n_exp] + both[:, n_exp:] + jnp.dot(x_lo, w_hi, preferred_element_type=f32)
    e = lax.broadcasted_iota(jnp.int32, (rows, n_exp), 1)
    v1 = jnp.max(logits, axis=-1, keepdims=True)
    i1 = jnp.min(jnp.where(logits == v1, e, n_exp), axis=-1, keepdims=True)
    rest = jnp.where(e == i1, NEG_INF, logits)
    v2 = jnp.max(rest, axis=-1, keepdims=True)
    i2 = jnp.min(jnp.where(rest == v2, e, n_exp), axis=-1, keepdims=True)
    e2 = jnp.exp(v2 - v1)
    denom = 1.0 + e2
    col = lax.broadcasted_iota(jnp.int32, (rows, TOP_K), 1)
    idx_ref[...] = jnp.where(col == 0, i1, i2)
    gate_ref[...] = jnp.where(col == 0, 1.0 / denom, e2 / denom)


def _out_router(o, h, wo, g_ffn, w_router, rows):
    t, d = h.shape
    n_exp = w_router.shape[1]
    return pl.pallas_call(
        _out_router_kernel,
        grid=(t // rows,),
        in_specs=[
            pl.BlockSpec((rows, d), lambda i: (i, 0)),
            pl.BlockSpec((rows, d), lambda i: (i, 0)),
            _resident((d, d)),
            _resident((1, d)),
            _resident((d, n_exp)),
        ],
        out_specs=[
            pl.BlockSpec((rows, d), lambda i: (i, 0)),
            pl.BlockSpec((rows, TOP_K), lambda i: (i, 0)),
            pl.BlockSpec((rows, TOP_K), lambda i: (i, 0)),
        ],
        out_shape=[
            jax.ShapeDtypeStruct((t, d), f32),
            jax.ShapeDtypeStruct((t, TOP_K), jnp.int32),
            jax.ShapeDtypeStruct((t, TOP_K), f32),
        ],
        compiler_params=_params("arbitrary"),
        name="out_router",
    )(o, h, wo, g_ffn, w_router)


def _dispatch_kernel(pos_ref, pad_lo_ref, pad_hi_ref, h_ref, g_ref, x_hbm, buf_ref, zero_ref, sem, zsem):
    i = pl.program_id(0)
    n_steps = pl.num_programs(0)
    rows, d = h_ref.shape
    sub = d // LANES
    slot = i % 2

    def staged_copies_wait(buf):
        for _ in range(TOP_K):
            pltpu.make_async_copy(buf_ref.at[buf], x_hbm.at[pl.ds(0, rows * sub), :], sem.at[buf]).wait()

    @pl.when(i == 0)
    def _():
        zero_ref[...] = jnp.zeros(zero_ref.shape, f32)
        for e in range(pad_lo_ref.shape[0]):
            lo, hi = pad_lo_ref[e], pad_hi_ref[e]
            mid = jnp.minimum(((lo + ZERO_ROWS - 1) // ZERO_ROWS) * ZERO_ROWS, hi)

            def one_row(r):
                return pltpu.make_async_copy(zero_ref.at[pl.ds(0, sub), :],
                                             x_hbm.at[pl.ds(pl.multiple_of(r * sub, sub), sub), :], zsem)

            def row_group(c):
                dst = pl.multiple_of(c * (ZERO_ROWS * sub), ZERO_ROWS * sub)
                return pltpu.make_async_copy(zero_ref, x_hbm.at[pl.ds(dst, ZERO_ROWS * sub), :], zsem)

            def run(copy_of, first, stop):
                def start(r, carry):
                    copy_of(r).start()
                    return carry

                def wait(r, carry):
                    copy_of(r).wait()
                    return carry
                lax.fori_loop(first, stop, start, 0)
                lax.fori_loop(first, stop, wait, 0)

            run(one_row, lo, mid)
            run(row_group, mid // ZERO_ROWS, hi // ZERO_ROWS)

        buf_ref[1] = jnp.zeros(buf_ref.shape[1:], f32)
        tail_lo = pad_hi_ref[pad_hi_ref.shape[0] - 1]
        n_tail = (x_hbm.shape[0] // sub - tail_lo) // rows

        def tail_copy(c):
            dst = pl.multiple_of((tail_lo + c * rows) * sub, sub)
            return pltpu.make_async_copy(buf_ref.at[1], x_hbm.at[pl.ds(dst, rows * sub), :], zsem)

        def tail_fill(c, carry):
            tail_copy(c).start()
            return carry

        def tail_drain(c, carry):
            tail_copy(c).wait()
            return carry
        lax.fori_loop(0, n_tail, tail_fill, 0)
        lax.fori_loop(0, n_tail, tail_drain, 0)

    @pl.when(i >= 2)
    def _():
        staged_copies_wait(slot)

    h = h_ref[...]
    _store_token_tiles(buf_ref.at[slot], h * _rms_scale(h) * g_ref[...])
    base = i * (TOP_K * rows)

    def trip(c, carry):
        for u in range(DMA_UNROLL):
            r = c * DMA_UNROLL + u
            src = buf_ref.at[slot, pl.ds(pl.multiple_of(r * sub, sub), sub), :]
            for k in range(TOP_K):
                dst = pl.multiple_of(pos_ref[base + k * rows + r] * sub, sub)
                pltpu.make_async_copy(src, x_hbm.at[pl.ds(dst, sub), :], sem.at[slot]).start(priority=k % 2)
        return carry
    lax.fori_loop(0, rows // DMA_UNROLL, trip, 0)

    @pl.when(i == n_steps - 1)
    def _():
        staged_copies_wait(slot)

        @pl.when(n_steps >= 2)
        def _():
            staged_copies_wait(1 - slot)


def _moe_dispatch(pos_steps, pad_lo, pad_hi, h, g_ffn, n_rows, rows):
    t, d = h.shape
    sub = d // LANES
    grid_spec = pltpu.PrefetchScalarGridSpec(
        num_scalar_prefetch=3,
        grid=(t // rows,),
        in_specs=[
            pl.BlockSpec((rows, d), lambda i, ps, lo, hi: (i, 0)),
            pl.BlockSpec((1, d), lambda i, ps, lo, hi: (0, 0)),
        ],
        out_specs=pl.BlockSpec(memory_space=pl.ANY),
        scratch_shapes=[
            pltpu.VMEM((2, rows * sub, LANES), f32),
            pltpu.VMEM((ZERO_ROWS * sub, LANES), f32),
            pltpu.SemaphoreType.DMA((2,)),
            pltpu.SemaphoreType.DMA,
        ],
    )
    return pl.pallas_call(
        _dispatch_kernel,
        grid_spec=grid_spec,
        out_shape=jax.ShapeDtypeStruct((n_rows * sub, LANES), f32),
        compiler_params=_params("arbitrary", disable_bounds_checks=True),
        name="moe_dispatch",
    )(pos_steps, pad_lo, pad_hi, h, g_ffn)


def _moe_gemm_kernel(tile_expert_ref, tile_valid_ref, tile_src_ref, x_ref, wg_ref, wu_ref, wd_ref,
                     y_ref, xb_ref, acc_ref):
    i = pl.program_id(0)
    f = pl.program_id(1)
    rows, d = xb_ref.shape
    sub = d // LANES
    valid = tile_valid_ref[i] > 0

    def swiglu_chunk():
        xb = xb_ref[...]
        g = jnp.dot(xb, wg_ref[0].astype(bf16), preferred_element_type=f32)
        u = jnp.dot(xb, wu_ref[0].astype(bf16), preferred_element_type=f32)
        a = (_silu(g) * u).astype(bf16)
        return jnp.dot(a, wd_ref[0].astype(bf16), preferred_element_type=f32)

    @pl.when(jnp.logical_and(f == 0, valid))
    def _():
        for c in range(sub):
            xb_ref[:, c * LANES:(c + 1) * LANES] = _load_token_tiles(x_ref, rows, sub, c).astype(bf16)
        acc_ref[...] = swiglu_chunk()

    last = pl.num_programs(1) - 1

    @pl.when(jnp.logical_and(jnp.logical_and(f > 0, f < last), valid))
    def _():
        acc_ref[...] += swiglu_chunk()

    @pl.when(jnp.logical_and(f == last, valid))
    def _():
        _store_token_tiles(y_ref, acc_ref[...] + swiglu_chunk())

    @pl.when(jnp.logical_and(f == last, jnp.logical_not(valid)))
    def _():
        y_ref[...] = jnp.zeros(y_ref.shape, f32)


def _moe_gemm(tile_expert, tile_valid, tile_src, x_tiles, wg, wu, wd, rows, chunk):
    n_tiles = tile_expert.shape[0]
    _, d, fe = wg.shape
    sub = d // LANES
    n_chunks = fe // chunk

    def chunk_of(i, f, tv):
        return jnp.where(tv[i] > 0, f, n_chunks - 1)

    grid_spec = pltpu.PrefetchScalarGridSpec(
        num_scalar_prefetch=3,
        grid=(n_tiles, n_chunks),
        in_specs=[
            pl.BlockSpec((rows * sub, LANES), lambda i, f, te, tv, ts: (ts[i], 0)),
            pl.BlockSpec((1, d, chunk), lambda i, f, te, tv, ts: (te[i], 0, chunk_of(i, f, tv))),
            pl.BlockSpec((1, d, chunk), lambda i, f, te, tv, ts: (te[i], 0, chunk_of(i, f, tv))),
            pl.BlockSpec((1, chunk, d), lambda i, f, te, tv, ts: (te[i], chunk_of(i, f, tv), 0)),
        ],
        out_specs=pl.BlockSpec((rows * sub, LANES), lambda i, f, te, tv, ts: (i, 0)),
        scratch_shapes=[
            pltpu.VMEM((rows, d), bf16),
            pltpu.VMEM((rows, d), f32),
        ],
    )
    return pl.pallas_call(
        _moe_gemm_kernel,
        grid_spec=grid_spec,
        out_shape=jax.ShapeDtypeStruct((n_tiles * rows * sub, LANES), f32),
        compiler_params=_params("arbitrary", "arbitrary"),
        name="moe_gemm",
    )(tile_expert, tile_valid, tile_src, x_tiles, wg, wu, wd)


def _combine_kernel(pos_ref, h_ref, gate_ref, g_ref, y_hbm, o_ref, ybuf_ref, sem):
    i = pl.program_id(0)
    rows, d = h_ref.shape
    sub = d // LANES
    slot = i % 2

    def gather_rows(step, buf):
        base = step * (TOP_K * rows)

        def trip(c, carry):
            for u in range(DMA_UNROLL):
                r = c * DMA_UNROLL + u
                for k in range(TOP_K):
                    src = pl.multiple_of(pos_ref[base + k * rows + r] * sub, sub)
                    pltpu.make_async_copy(y_hbm.at[pl.ds(src, sub), :],
                                          ybuf_ref.at[buf, k, pl.ds(pl.multiple_of(r * sub, sub), sub), :],
                                          sem.at[buf]).start(priority=k % 2)
            return carry
        lax.fori_loop(0, rows // DMA_UNROLL, trip, 0)

    @pl.when(i == 0)
    def _():
        gather_rows(0, 0)

    @pl.when(i + 1 < pl.num_programs(0))
    def _():
        gather_rows(i + 1, 1 - slot)

    for k in range(TOP_K):
        pltpu.make_async_copy(y_hbm.at[pl.ds(0, rows * sub), :], ybuf_ref.at[slot, k], sem.at[slot]).wait()
    gate = gate_ref[...]
    chunks = []
    sumsq = jnp.zeros((rows, 1), f32)
    for c in range(sub):
        hc = h_ref[:, c * LANES:(c + 1) * LANES]
        for k in range(TOP_K):
            hc = hc + gate[:, k:k + 1] * _load_token_tiles(ybuf_ref.at[slot, k], rows, sub, c)
        sumsq = sumsq + jnp.sum(hc * hc, axis=-1, keepdims=True)
        chunks.append(hc)
    scale = lax.rsqrt(sumsq / d + EPS)
    for c in range(sub):
        o_ref[:, c * LANES:(c + 1) * LANES] = chunks[c] * scale * g_ref[:, c * LANES:(c + 1) * LANES]


def _moe_combine(pos, h, gates, g_final, y_tiles, rows):
    t, d = h.shape
    sub = d // LANES
    grid_spec = pltpu.PrefetchScalarGridSpec(
        num_scalar_prefetch=1,
        grid=(t // rows,),
        in_specs=[
            pl.BlockSpec((rows, d), lambda i, ps: (i, 0)),
            pl.BlockSpec((rows, TOP_K), lambda i, ps: (i, 0)),
            pl.BlockSpec((1, d), lambda i, ps: (0, 0)),
            pl.BlockSpec(memory_space=pl.ANY),
        ],
        out_specs=pl.BlockSpec((rows, d), lambda i, ps: (i, 0)),
        scratch_shapes=[pltpu.VMEM((2, TOP_K, rows * sub, LANES), f32), pltpu.SemaphoreType.DMA((2,))],
    )
    return pl.pallas_call(
        _combine_kernel,
        grid_spec=grid_spec,
        out_shape=jax.ShapeDtypeStruct((t, d), f32),
        compiler_params=_params("arbitrary", disable_bounds_checks=True),
        name="moe_combine",
    )(pos, h, gates, g_final, y_tiles)


def _routing_plan(idx, n_exp, rows, disp_rows, comb_rows):
    t = idx.shape[0]
    pairs = t * TOP_K
    n_tiles = -(-pairs // rows) + n_exp
    e_flat = idx.T.reshape(pairs)
    onehot = (e_flat[:, None] == jnp.arange(n_exp, dtype=jnp.int32)[None, :]).astype(jnp.int32)
    csum = jnp.cumsum(onehot, axis=0)
    counts = csum[-1]
    rank = jnp.sum(onehot * csum, axis=1) - 1
    padded = ((counts + rows - 1) // rows) * rows
    pend = jnp.cumsum(padded)
    poff = pend - padded
    pos = poff[e_flat] + rank
    tile_start = jnp.arange(n_tiles, dtype=jnp.int32) * rows
    tile_expert = jnp.minimum(jnp.sum((tile_start[:, None] >= pend[None, :]).astype(jnp.int32), axis=1),
                              n_exp - 1)
    tile_valid = (tile_start < pend[-1]).astype(jnp.int32)
    last_valid = jnp.maximum(jnp.sum(tile_valid) - 1, 0)
    tile_expert = jnp.where(tile_valid > 0, tile_expert, tile_expert[last_valid])
    tile_src = jnp.minimum(jnp.arange(n_tiles, dtype=jnp.int32), last_valid)

    def by_step(step_rows):
        return pos.reshape(TOP_K, t // step_rows, step_rows).transpose(1, 0, 2).reshape(pairs)

    return (tile_expert, tile_valid, tile_src, poff + counts, pend, n_tiles * rows,
            by_step(disp_rows), by_step(comb_rows))


def _rope_tables(s):
    half = ROT_DIM // 2
    inv = ROPE_THETA ** (-jnp.arange(half, dtype=f32) * 2.0 / ROT_DIM)
    ang = jnp.arange(s).astype(f32)[:, None] * inv[None, :]
    cos, sin = jnp.cos(ang), jnp.sin(ang)
    pad = HEAD_DIM - ROT_DIM
    cos_t = jnp.concatenate([cos, cos, jnp.ones((s, pad), f32)], axis=-1)
    sin_t = jnp.concatenate([-sin, sin, jnp.zeros((s, pad), f32)], axis=-1)
    return cos_t, sin_t


def kernel(x, attn_norm, ffn_norm, pool_w, pool_scale, wq, wo, kv_norm, wk, wv, ffn_w_gate, ffn_w_up,
           ffn_w_down, router_w, moe_w_gate, moe_w_up, moe_w_down, final_norm):
    b, s, d = x.shape
    t = b * s
    n_exp = router_w.shape[-1]
    assert attn_norm.shape[0] == 2 and pool_w.shape[0] == 1 and wq.shape[0] == 1 and router_w.shape[0] == 1
    tok_rows = 1024
    ffn_rows = 512
    moe_chunk = 512
    moe_rows = 1024
    comb_rows = 512
    assert s % tok_rows == 0 and d % (HEAD_DIM * 2) == 0 and t % comb_rows == 0
    assert moe_w_gate.shape[-1] % moe_chunk == 0 and moe_w_gate.shape[-1] // moe_chunk >= 2
    assert d % (8 * LANES) == 0 and moe_rows % tok_rows == 0

    row = lambda v: v.reshape(1, -1)

    h1, hn1 = _pool_mixer(x, row(attn_norm[0]), pool_w[0].astype(bf16), row(pool_scale[0]),
                          row(ffn_norm[0]), tok_rows)
    h2 = _dense_ffn(h1.reshape(t, d), hn1.reshape(t, d), ffn_w_gate[0].astype(bf16),
                    ffn_w_up[0].astype(bf16), ffn_w_down[0].astype(bf16), ffn_rows, 512)

    cos_t, sin_t = _rope_tables(s)
    qt, k, vt, kmean = _qkv_proj(h2.reshape(b, s, d), row(attn_norm[1]), row(kv_norm), wq[0].astype(bf16),
                                 wk.astype(bf16), wv.astype(bf16), cos_t, sin_t, tok_rows)
    o = _moba_attention(qt, k, vt, kmean)

    h3, idx, gates = _out_router(o.reshape(t, d), h2, wo[0].astype(bf16), row(ffn_norm[1]), router_w[0],
                                 tok_rows)

    (tile_expert, tile_valid, tile_src, pad_lo, pad_hi, n_rows, pos_disp,
     pos_comb) = _routing_plan(idx, n_exp, moe_rows, tok_rows, comb_rows)
    x_tiles = _moe_dispatch(pos_disp, pad_lo, pad_hi, h3, row(ffn_norm[1]), n_rows, tok_rows)
    y_tiles = _moe_gemm(tile_expert, tile_valid, tile_src, x_tiles, moe_w_gate[0], moe_w_up[0],
                        moe_w_down[0], moe_rows, moe_chunk)
    out = _moe_combine(pos_comb, h3, gates, row(final_norm), y_tiles, comb_rows)
    return out.reshape(b, s, d)
```

```python
import functools

import jax
import jax.numpy as jnp
from jax import lax
from jax.experimental import pallas as pl
from jax.experimental.pallas import tpu as pltpu

EPS = 1e-6
POOL_WINDOWS = (2, 4, 8, 16)
POOL_HALO = 16
LANES = 128
HEAD_DIM = 128
BLOCK = 256
TOP_BLOCKS = 3
ROT_DIM = HEAD_DIM // 4
ROPE_THETA = 500000.0
TOP_K = 2

VMEM_LIMIT = 56 * 1024 * 1024
DMA_UNROLL = 16
ZERO_ROWS = 8
NEG_INF = float("-inf")
POS_INF = float("inf")
MXU_LAG = 8
ONES_ROWS = 16
LOG2_E = 1.4426950408889634

f32 = jnp.float32
bf16 = jnp.bfloat16


def _resident(shape):
    n = len(shape)
    return pl.BlockSpec(shape, lambda *_: (0,) * n, pipeline_mode=pl.Buffered(1))


def _rms_scale(xf):
    return lax.rsqrt(jnp.mean(xf * xf, axis=-1, keepdims=True) + EPS)


def _silu(g):
    return g * (1.0 / (1.0 + jnp.exp(-g)))


def _store_token_tiles(ref, x):
    rows, d = x.shape
    sub = d // LANES
    for c in range(sub):
        ref[pl.ds(c, rows, stride=sub), :] = x[:, c * LANES:(c + 1) * LANES]


def _load_token_tiles(ref, rows, sub, chunk):
    return ref[pl.ds(chunk, rows, stride=sub), :]


def _params(*sem, **kw):
    return pltpu.CompilerParams(dimension_semantics=sem, vmem_limit_bytes=VMEM_LIMIT, **kw)


def _pool_kernel(x_ref, halo_ref, g_attn_ref, w_ref, scale_ref, g_ffn_ref, h_ref, hn_ref, ext_ref):
    i = pl.program_id(1)
    rows = x_ref.shape[1]
    d = x_ref.shape[2]
    c = d // len(POOL_WINDOWS)
    g_attn = g_attn_ref[...]
    x = x_ref[0]
    xn = x * _rms_scale(x) * g_attn
    halo = halo_ref[0]
    halo = halo * _rms_scale(halo) * g_attn
    ext_ref[0:POOL_HALO, :] = jnp.where(i == 0, 0.0, halo)
    ext_ref[POOL_HALO:, :] = xn
    t = i * rows + lax.broadcasted_iota(jnp.int32, (rows, 1), 0)
    ys = []
    for g, w in enumerate(POOL_WINDOWS):
        c0 = g * c
        s = ext_ref[:, c0:c0 + c]
        shift = 1
        while shift < w:
            s = s + pltpu.roll(s, shift, 0)
            shift *= 2
        cnt = jnp.minimum(t + 1, w).astype(f32)
        pooled = s[POOL_HALO:, :] / cnt - xn[:, c0:c0 + c]
        ys.append(jnp.dot(pooled.astype(bf16), w_ref[g], preferred_element_type=f32))
    y = jnp.concatenate(ys, axis=-1) * scale_ref[...]
    h = x + y
    h_ref[0] = h
    hn_ref[0] = (h * _rms_scale(h) * g_ffn_ref[...]).astype(bf16)


def _pool_mixer(x, g_attn, w_pool, scale, g_ffn, rows):
    b, s, d = x.shape
    c = d // len(POOL_WINDOWS)
    hb = rows // POOL_HALO
    return pl.pallas_call(
        _pool_kernel,
        grid=(b, s // rows),
        in_specs=[
            pl.BlockSpec((1, rows, d), lambda bi, i: (bi, i, 0)),
            pl.BlockSpec((1, POOL_HALO, d), lambda bi, i: (bi, jnp.maximum(i * hb - 1, 0), 0)),
            _resident((1, d)),
            _resident((len(POOL_WINDOWS), c, c)),
            _resident((1, d)),
            _resident((1, d)),
        ],
        out_specs=[
            pl.BlockSpec((1, rows, d), lambda bi, i: (bi, i, 0)),
            pl.BlockSpec((1, rows, d), lambda bi, i: (bi, i, 0)),
        ],
        out_shape=[jax.ShapeDtypeStruct((b, s, d), f32), jax.ShapeDtypeStruct((b, s, d), bf16)],
        scratch_shapes=[pltpu.VMEM((rows + POOL_HALO, d), f32)],
        compiler_params=_params("arbitrary", "arbitrary"),
        name="pool_mixer",
    )(x, x, g_attn, w_pool, scale, g_ffn)


def _ffn_chunks(f, size):
    out, c0 = [], 0
    while c0 < f:
        out.append((c0, min(size, f - c0)))
        c0 += size
    return out


def _dense_ffn_kernel(h_ref, hn_ref, wg_ref, wu_ref, wd_ref, o_ref, *, chunk):
    hn = hn_ref[...]
    acc = h_ref[...]
    for c0, cw in _ffn_chunks(wg_ref.shape[1], chunk):
        g = jnp.dot(hn, wg_ref[:, c0:c0 + cw], preferred_element_type=f32)
        u = jnp.dot(hn, wu_ref[:, c0:c0 + cw], preferred_element_type=f32)
        a = (_silu(g) * u).astype(bf16)
        acc = acc + jnp.dot(a, wd_ref[c0:c0 + cw, :], preferred_element_type=f32)
    o_ref[...] = acc


def _dense_ffn(h, hn, wg, wu, wd, rows, chunk):
    t, d = h.shape
    f = wg.shape[1]
    return pl.pallas_call(
        functools.partial(_dense_ffn_kernel, chunk=chunk),
        grid=(t // rows,),
        in_specs=[
            pl.BlockSpec((rows, d), lambda i: (i, 0)),
            pl.BlockSpec((rows, d), lambda i: (i, 0)),
            _resident((d, f)),
            _resident((d, f)),
            _resident((f, d)),
        ],
        out_specs=pl.BlockSpec((rows, d), lambda i: (i, 0)),
        out_shape=jax.ShapeDtypeStruct((t, d), f32),
        compiler_params=_params("arbitrary"),
        name="dense_ffn",
    )(h, hn, wg, wu, wd)


def _rope(x, cos, sin_signed, lane):
    d = x.shape[-1]
    half = ROT_DIM // 2
    partner = jnp.where(lane < half, pltpu.roll(x, d - half, 1), pltpu.roll(x, half, 1))
    return x * cos + partner * sin_signed


def _qkv_kernel(h_ref, g_q_ref, g_kv_ref, wq_ref, wk_ref, wv_ref, cos_ref, sin_ref,
                qt_ref, k_ref, vt_ref, kmean_ref):
    i = pl.program_id(1)
    rows, d = h_ref.shape[1], h_ref.shape[2]
    heads = d // HEAD_DIM
    h = h_ref[0]
    xhat = h * _rms_scale(h)
    hq = (xhat * g_q_ref[...]).astype(bf16)
    hkv = (xhat * g_kv_ref[...]).astype(bf16)
    q = jnp.dot(hq, wq_ref[...], preferred_element_type=f32)
    k = jnp.dot(hkv, wk_ref[...], preferred_element_type=f32)
    v = jnp.dot(hkv, wv_ref[...], preferred_element_type=f32)
    cos = jnp.concatenate([cos_ref[...]] * heads, axis=-1)
    sin = jnp.concatenate([sin_ref[...]] * heads, axis=-1)
    lane = lax.broadcasted_iota(jnp.int32, (rows, d), 1) % HEAD_DIM
    q = _rope(q, cos, sin, lane) * (HEAD_DIM ** -0.5 * LOG2_E)
    k = _rope(k, cos, sin, lane)
    qt_ref[0] = q.T.astype(bf16)
    k_ref[0] = k.astype(bf16)
    vt_ref[0] = v.T.astype(bf16)

    nb = kmean_ref.shape[1]
    blk_iota = lax.broadcasted_iota(jnp.int32, (nb, d), 0)

    @pl.when(i == 0)
    def _():
        kmean_ref[0] = jnp.zeros((nb, d), f32)

    km = kmean_ref[0]
    for j in range(rows // BLOCK):
        mean_j = jnp.mean(k[j * BLOCK:(j + 1) * BLOCK, :], axis=0, keepdims=True)
        km = jnp.where(blk_iota == i * (rows // BLOCK) + j, mean_j, km)
    kmean_ref[0] = km


def _qkv_proj(h, g_q, g_kv, wq, wk, wv, cos, sin, rows):
    b, s, d = h.shape
    nb = s // BLOCK
    return pl.pallas_call(
        _qkv_kernel,
        grid=(b, s // rows),
        in_specs=[
            pl.BlockSpec((1, rows, d), lambda bi, i: (bi, i, 0)),
            _resident((1, d)),
            _resident((1, d)),
            _resident((d, d)),
            _resident((d, d)),
            _resident((d, d)),
            pl.BlockSpec((rows, HEAD_DIM), lambda bi, i: (i, 0)),
            pl.BlockSpec((rows, HEAD_DIM), lambda bi, i: (i, 0)),
        ],
        out_specs=[
            pl.BlockSpec((1, d, rows), lambda bi, i: (bi, 0, i)),
            pl.BlockSpec((1, rows, d), lambda bi, i: (bi, i, 0)),
            pl.BlockSpec((1, d, rows), lambda bi, i: (bi, 0, i)),
            pl.BlockSpec((1, nb, d), lambda bi, i: (bi, 0, 0)),
        ],
        out_shape=[
            jax.ShapeDtypeStruct((b, d, s), bf16),
            jax.ShapeDtypeStruct((b, s, d), bf16),
            jax.ShapeDtypeStruct((b, d, s), bf16),
            jax.ShapeDtypeStruct((b, nb, d), f32),
        ],
        compiler_params=_params("arbitrary", "arbitrary"),
        name="qkv_proj",
    )(h, g_q, g_kv, wq, wk, wv, cos, sin)


def _moba_kernel(qt_ref, k_ref, vt_ref, kmean_ref, o_ref, *scratch, heads):
    sel_refs, m_refs, l_refs, acc_refs = (scratch[i * heads:(i + 1) * heads] for i in range(4))
    j = pl.program_id(1)
    nb = kmean_ref.shape[1]
    hd = HEAD_DIM
    blk = lax.broadcasted_iota(jnp.int32, (nb, BLOCK), 0)
    past = blk < j
    kpos = lax.broadcasted_iota(jnp.int32, (BLOCK, BLOCK), 0)
    qpos = lax.broadcasted_iota(jnp.int32, (BLOCK, BLOCK), 1)
    causal = kpos <= qpos
    start = pl.multiple_of(j * BLOCK, BLOCK)

    def q_head(h):
        return qt_ref[0, h * hd:(h + 1) * hd, :]

    ones_rows = jnp.ones((ONES_ROWS, BLOCK), bf16)

    def score(h, off):
        return jnp.dot(k_ref[0, pl.ds(off, BLOCK), h * hd:(h + 1) * hd], q_head(h), preferred_element_type=f32)

    def weighted_values(h, p, off):
        vt_ones = jnp.concatenate([vt_ref[0, h * hd:(h + 1) * hd, pl.ds(off, BLOCK)], ones_rows], axis=0)
        out = jnp.dot(vt_ones, p, preferred_element_type=f32)
        return out[:hd], out[hd:hd + 1]

    def key_block(off, probs, accumulate):
        pending = {}
        for step in range(heads + MXU_LAG):
            if step < heads:
                pending[step] = probs(step, score(step, off))
            if step >= MXU_LAG:
                h = step - MXU_LAG
                p, extra = pending.pop(h)
                accumulate(h, extra, *weighted_values(h, p, off))

    for h in range(heads):
        km = kmean_ref[0, :, h * hd:(h + 1) * hd]
        km_hi = km.astype(bf16)
        km_lo = (km - km_hi.astype(f32)).astype(bf16)
        gate = (jnp.dot(km_hi, q_head(h), preferred_element_type=f32)
                + jnp.dot(km_lo, q_head(h), preferred_element_type=f32))
        gate = jnp.where(past, gate, NEG_INF)
        rank = jnp.zeros((nb, BLOCK), jnp.int32)
        for m in range(nb):
            gm = gate[m:m + 1, :]
            rank = rank + jnp.where(gm > gate, 1, jnp.where(gm == gate, jnp.where(blk > m, 1, 0), 0))
        sel_refs[h][...] = jnp.where(past, jnp.where(rank < TOP_BLOCKS, 1.0, 0.0), 0.0)

    def own_probs(h, s):
        s = jnp.where(causal, s, NEG_INF)
        m0 = jnp.max(s, axis=0, keepdims=True)
        m_refs[h][...] = m0
        return jnp.exp2((s - m0).astype(bf16)), None

    def own_accumulate(h, _, pv, psum):
        acc_refs[h][...] = pv
        l_refs[h][...] = psum

    key_block(start, own_probs, own_accumulate)

    def body(n, carry):
        def probs(h, s):
            chosen = sel_refs[h][pl.ds(n, 1), :] > 0.0
            m_old = m_refs[h][...]
            m_new = jnp.maximum(m_old, jnp.where(chosen, jnp.max(s, axis=0, keepdims=True), NEG_INF))
            m_refs[h][...] = m_new
            p = jnp.exp2((s - jnp.where(chosen, m_new, POS_INF)).astype(bf16))
            return p, jnp.exp2(m_old - m_new)

        def accumulate(h, alpha, pv, psum):
            acc_refs[h][...] = alpha * acc_refs[h][...] + pv
            l_refs[h][...] = alpha * l_refs[h][...] + psum

        key_block(pl.multiple_of(n * BLOCK, BLOCK), probs, accumulate)
        return carry

    lax.fori_loop(0, j, body, 0)
    for h in range(heads):
        o_ref[0, :, h * hd:(h + 1) * hd] = (acc_refs[h][...] / l_refs[h][...]).T.astype(o_ref.dtype)


def _moba_attention(qt, k, vt, kmean):
    b, d, s = qt.shape
    heads = d // HEAD_DIM
    nb = s // BLOCK
    return pl.pallas_call(
        functools.partial(_moba_kernel, heads=heads),
        grid=(b, nb),
        in_specs=[
            pl.BlockSpec((1, d, BLOCK), lambda bi, j: (bi, 0, j)),
            pl.BlockSpec((1, s, d), lambda bi, j: (bi, 0, 0)),
            pl.BlockSpec((1, d, s), lambda bi, j: (bi, 0, 0)),
            pl.BlockSpec((1, nb, d), lambda bi, j: (bi, 0, 0)),
        ],
        out_specs=pl.BlockSpec((1, BLOCK, d), lambda bi, j: (bi, j, 0)),
        out_shape=jax.ShapeDtypeStruct((b, s, d), bf16),
        scratch_shapes=(
            [pltpu.VMEM((nb, BLOCK), f32)] * heads
            + [pltpu.VMEM((1, BLOCK), f32)] * (2 * heads)
            + [pltpu.VMEM((HEAD_DIM, BLOCK), f32)] * heads
        ),
        compiler_params=_params("arbitrary", "arbitrary"),
        name="moba_attn",
    )(qt, k, vt, kmean)


def _out_router_kernel(o_ref, h_ref, wo_ref, g_ref, wr_ref, h_out_ref, idx_ref, gate_ref):
    rows = o_ref.shape[0]
    n_exp = wr_ref.shape[1]
    h = h_ref[...] + jnp.dot(o_ref[...], wo_ref[...], preferred_element_type=f32)
    h_out_ref[...] = h
    hn = h * _rms_scale(h) * g_ref[...]
    wr = wr_ref[...]
    w_hi = wr.astype(bf16)
    w_lo = (wr - w_hi.astype(f32)).astype(bf16)
    x_hi = hn.astype(bf16)
    x_lo = (hn - x_hi.astype(f32)).astype(bf16)
    both = jnp.dot(x_hi, jnp.concatenate([w_hi, w_lo], axis=1), preferred_element_type=f32)
    logits = both[:, :n_exp] + both[:, n_exp:] + jnp.dot(x_lo, w_hi, preferred_element_type=f32)
    e = lax.broadcasted_iota(jnp.int32, (rows, n_exp), 1)
    v1 = jnp.max(logits, axis=-1, keepdims=True)
    i1 = jnp.min(jnp.where(logits == v1, e, n_exp), axis=-1, keepdims=True)
    rest = jnp.where(e == i1, NEG_INF, logits)
    v2 = jnp.max(rest, axis=-1, keepdims=True)
    i2 = jnp.min(jnp.where(rest == v2, e, n_exp), axis=-1, keepdims=True)
    e2 = jnp.exp(v2 - v1)
    denom = 1.0 + e2
    col = lax.broadcasted_iota(jnp.int32, (rows, TOP_K), 1)
    idx_ref[...] = jnp.where(col == 0, i1, i2)
    gate_ref[...] = jnp.where(col == 0, 1.0 / denom, e2 / denom)


def _out_router(o, h, wo, g_ffn, w_router, rows):
    t, d = h.shape
    n_exp = w_router.shape[1]
    return pl.pallas_call(
        _out_router_kernel,
        grid=(t // rows,),
        in_specs=[
            pl.BlockSpec((rows, d), lambda i: (i, 0)),
            pl.BlockSpec((rows, d), lambda i: (i, 0)),
            _resident((d, d)),
            _resident((1, d)),
            _resident((d, n_exp)),
        ],
        out_specs=[
            pl.BlockSpec((rows, d), lambda i: (i, 0)),
            pl.BlockSpec((rows, TOP_K), lambda i: (i, 0)),
            pl.BlockSpec((rows, TOP_K), lambda i: (i, 0)),
        ],
        out_shape=[
            jax.ShapeDtypeStruct((t, d), f32),
            jax.ShapeDtypeStruct((t, TOP_K), jnp.int32),
            jax.ShapeDtypeStruct((t, TOP_K), f32),
        ],
        compiler_params=_params("arbitrary"),
        name="out_router",
    )(o, h, wo, g_ffn, w_router)


def _dispatch_kernel(pos_ref, pad_lo_ref, pad_hi_ref, h_ref, g_ref, x_hbm, buf_ref, zero_ref, sem, zsem):
    i = pl.program_id(0)
    n_steps = pl.num_programs(0)
    rows, d = h_ref.shape
    sub = d // LANES
    slot = i % 2

    def staged_copies_wait(buf):
        for _ in range(TOP_K):
            pltpu.make_async_copy(buf_ref.at[buf], x_hbm.at[pl.ds(0, rows * sub), :], sem.at[buf]).wait()

    @pl.when(i == 0)
    def _():
        zero_ref[...] = jnp.zeros(zero_ref.shape, f32)
        for e in range(pad_lo_ref.shape[0]):
            lo, hi = pad_lo_ref[e], pad_hi_ref[e]
            mid = jnp.minimum(((lo + ZERO_ROWS - 1) // ZERO_ROWS) * ZERO_ROWS, hi)

            def one_row(r):
                return pltpu.make_async_copy(zero_ref.at[pl.ds(0, sub), :],
                                             x_hbm.at[pl.ds(pl.multiple_of(r * sub, sub), sub), :], zsem)

            def row_group(c):
                dst = pl.multiple_of(c * (ZERO_ROWS * sub), ZERO_ROWS * sub)
                return pltpu.make_async_copy(zero_ref, x_hbm.at[pl.ds(dst, ZERO_ROWS * sub), :], zsem)

            def run(copy_of, first, stop):
                def start(r, carry):
                    copy_of(r).start()
                    return carry

                def wait(r, carry):
                    copy_of(r).wait()
                    return carry
                lax.fori_loop(first, stop, start, 0)
                lax.fori_loop(first, stop, wait, 0)

            run(one_row, lo, mid)
            run(row_group, mid // ZERO_ROWS, hi // ZERO_ROWS)

        buf_ref[1] = jnp.zeros(buf_ref.shape[1:], f32)
        tail_lo = pad_hi_ref[pad_hi_ref.shape[0] - 1]
        n_tail = (x_hbm.shape[0] // sub - tail_lo) // rows

        def tail_copy(c):
            dst = pl.multiple_of((tail_lo + c * rows) * sub, sub)
            return pltpu.make_async_copy(buf_ref.at[1], x_hbm.at[pl.ds(dst, rows * sub), :], zsem)

        def tail_fill(c, carry):
            tail_copy(c).start()
            return carry

        def tail_drain(c, carry):
            tail_copy(c).wait()
            return carry
        lax.fori_loop(0, n_tail, tail_fill, 0)
        lax.fori_loop(0, n_tail, tail_drain, 0)

    @pl.when(i >= 2)
    def _():
        staged_copies_wait(slot)

    h = h_ref[...]
    _store_token_tiles(buf_ref.at[slot], h * _rms_scale(h) * g_ref[...])
    base = i * (TOP_K * rows)

    def trip(c, carry):
        for u in range(DMA_UNROLL):
            r = c * DMA_UNROLL + u
            src = buf_ref.at[slot, pl.ds(pl.multiple_of(r * sub, sub), sub), :]
            for k in range(TOP_K):
                dst = pl.multiple_of(pos_ref[base + k * rows + r] * sub, sub)
                pltpu.make_async_copy(src, x_hbm.at[pl.ds(dst, sub), :], sem.at[slot]).start(priority=k % 2)
        return carry
    lax.fori_loop(0, rows // DMA_UNROLL, trip, 0)

    @pl.when(i == n_steps - 1)
    def _():
        staged_copies_wait(slot)

        @pl.when(n_steps >= 2)
        def _():
            staged_copies_wait(1 - slot)


def _moe_dispatch(pos_steps, pad_lo, pad_hi, h, g_ffn, n_rows, rows):
    t, d = h.shape
    sub = d // LANES
    grid_spec = pltpu.PrefetchScalarGridSpec(
        num_scalar_prefetch=3,
        grid=(t // rows,),
        in_specs=[
            pl.BlockSpec((rows, d), lambda i, ps, lo, hi: (i, 0)),
            pl.BlockSpec((1, d), lambda i, ps, lo, hi: (0, 0)),
        ],
        out_specs=pl.BlockSpec(memory_space=pl.ANY),
        scratch_shapes=[
            pltpu.VMEM((2, rows * sub, LANES), f32),
            pltpu.VMEM((ZERO_ROWS * sub, LANES), f32),
            pltpu.SemaphoreType.DMA((2,)),
            pltpu.SemaphoreType.DMA,
        ],
    )
    return pl.pallas_call(
        _dispatch_kernel,
        grid_spec=grid_spec,
        out_shape=jax.ShapeDtypeStruct((n_rows * sub, LANES), f32),
        compiler_params=_params("arbitrary", disable_bounds_checks=True),
        name="moe_dispatch",
    )(pos_steps, pad_lo, pad_hi, h, g_ffn)


def _moe_gemm_kernel(tile_expert_ref, tile_valid_ref, tile_src_ref, x_ref, wg_ref, wu_ref, wd_ref,
                     y_ref, xb_ref, acc_ref):
    i = pl.program_id(0)
    f = pl.program_id(1)
    rows, d = xb_ref.shape
    sub = d // LANES
    valid = tile_valid_ref[i] > 0

    def swiglu_chunk():
        xb = xb_ref[...]
        g = jnp.dot(xb, wg_ref[0].astype(bf16), preferred_element_type=f32)
        u = jnp.dot(xb, wu_ref[0].astype(bf16), preferred_element_type=f32)
        a = (_silu(g) * u).astype(bf16)
        return jnp.dot(a, wd_ref[0].astype(bf16), preferred_element_type=f32)

    @pl.when(jnp.logical_and(f == 0, valid))
    def _():
        for c in range(sub):
            xb_ref[:, c * LANES:(c + 1) * LANES] = _load_token_tiles(x_ref, rows, sub, c).astype(bf16)
        acc_ref[...] = swiglu_chunk()

    last = pl.num_programs(1) - 1

    @pl.when(jnp.logical_and(jnp.logical_and(f > 0, f < last), valid))
    def _():
        acc_ref[...] += swiglu_chunk()

    @pl.when(jnp.logical_and(f == last, valid))
    def _():
        _store_token_tiles(y_ref, acc_ref[...] + swiglu_chunk())

    @pl.when(jnp.logical_and(f == last, jnp.logical_not(valid)))
    def _():
        y_ref[...] = jnp.zeros(y_ref.shape, f32)


def _moe_gemm(tile_expert, tile_valid, tile_src, x_tiles, wg, wu, wd, rows, chunk):
    n_tiles = tile_expert.shape[0]
    _, d, fe = wg.shape
    sub = d // LANES
    n_chunks = fe // chunk

    def chunk_of(i, f, tv):
        return jnp.where(tv[i] > 0, f, n_chunks - 1)

    grid_spec = pltpu.PrefetchScalarGridSpec(
        num_scalar_prefetch=3,
        grid=(n_tiles, n_chunks),
        in_specs=[
            pl.BlockSpec((rows * sub, LANES), lambda i, f, te, tv, ts: (ts[i], 0)),
            pl.BlockSpec((1, d, chunk), lambda i, f, te, tv, ts: (te[i], 0, chunk_of(i, f, tv))),
            pl.BlockSpec((1, d, chunk), lambda i, f, te, tv, ts: (te[i], 0, chunk_of(i, f, tv))),
            pl.BlockSpec((1, chunk, d), lambda i, f, te, tv, ts: (te[i], chunk_of(i, f, tv), 0)),
        ],
        out_specs=pl.BlockSpec((rows * sub, LANES), lambda i, f, te, tv, ts: (i, 0)),
        scratch_shapes=[
            pltpu.VMEM((rows, d), bf16),
            pltpu.VMEM((rows, d), f32),
        ],
    )
    return pl.pallas_call(
        _moe_gemm_kernel,
        grid_spec=grid_spec,
        out_shape=jax.ShapeDtypeStruct((n_tiles * rows * sub, LANES), f32),
        compiler_params=_params("arbitrary", "arbitrary"),
        name="moe_gemm",
    )(tile_expert, tile_valid, tile_src, x_tiles, wg, wu, wd)


def _combine_kernel(pos_ref, h_ref, gate_ref, g_ref, y_hbm, o_ref, ybuf_ref, sem):
    i = pl.program_id(0)
    rows, d = h_ref.shape
    sub = d // LANES
    slot = i % 2

    def gather_rows(step, buf):
        base = step * (TOP_K * rows)

        def trip(c, carry):
            for u in range(DMA_UNROLL):
                r = c * DMA_UNROLL + u
                for k in range(TOP_K):
                    src = pl.multiple_of(pos_ref[base + k * rows + r] * sub, sub)
                    pltpu.make_async_copy(y_hbm.at[pl.ds(src, sub), :],
                                          ybuf_ref.at[buf, k, pl.ds(pl.multiple_of(r * sub, sub), sub), :],
                                          sem.at[buf]).start(priority=k % 2)
            return carry
        lax.fori_loop(0, rows // DMA_UNROLL, trip, 0)

    @pl.when(i == 0)
    def _():
        gather_rows(0, 0)

    @pl.when(i + 1 < pl.num_programs(0))
    def _():
        gather_rows(i + 1, 1 - slot)

    for k in range(TOP_K):
        pltpu.make_async_copy(y_hbm.at[pl.ds(0, rows * sub), :], ybuf_ref.at[slot, k], sem.at[slot]).wait()
    gate = gate_ref[...]
    chunks = []
    sumsq = jnp.zeros((rows, 1), f32)
    for c in range(sub):
        hc = h_ref[:, c * LANES:(c + 1) * LANES]
        for k in range(TOP_K):
            hc = hc + gate[:, k:k + 1] * _load_token_tiles(ybuf_ref.at[slot, k], rows, sub, c)
        sumsq = sumsq + jnp.sum(hc * hc, axis=-1, keepdims=True)
        chunks.append(hc)
    scale = lax.rsqrt(sumsq / d + EPS)
    for c in range(sub):
        o_ref[:, c * LANES:(c + 1) * LANES] = chunks[c] * scale * g_ref[:, c * LANES:(c + 1) * LANES]


def _moe_combine(pos, h, gates, g_final, y_tiles, rows):
    t, d = h.shape
    sub = d // LANES
    grid_spec = pltpu.PrefetchScalarGridSpec(
        num_scalar_prefetch=1,
        grid=(t // rows,),
        in_specs=[
            pl.BlockSpec((rows, d), lambda i, ps: (i, 0)),
            pl.BlockSpec((rows, TOP_K), lambda i, ps: (i, 0)),
            pl.BlockSpec((1, d), lambda i, ps: (0, 0)),
            pl.BlockSpec(memory_space=pl.ANY),
        ],
        out_specs=pl.BlockSpec((rows, d), lambda i, ps: (i, 0)),
        scratch_shapes=[pltpu.VMEM((2, TOP_K, rows * sub, LANES), f32), pltpu.SemaphoreType.DMA((2,))],
    )
    return pl.pallas_call(
        _combine_kernel,
        grid_spec=grid_spec,
        out_shape=jax.ShapeDtypeStruct((t, d), f32),
        compiler_params=_params("arbitrary", disable_bounds_checks=True),
        name="moe_combine",
    )(pos, h, gates, g_final, y_tiles)


def _routing_plan(idx, n_exp, rows, disp_rows, comb_rows):
    t = idx.shape[0]
    pairs = t * TOP_K
    n_tiles = -(-pairs // rows) + n_exp
    e_flat = idx.T.reshape(pairs)
    onehot = (e_flat[:, None] == jnp.arange(n_exp, dtype=jnp.int32)[None, :]).astype(jnp.int32)
    csum = jnp.cumsum(onehot, axis=0)
    counts = csum[-1]
    rank = jnp.sum(onehot * csum, axis=1) - 1
    padded = ((counts + rows - 1) // rows) * rows
    pend = jnp.cumsum(padded)
    poff = pend - padded
    pos = poff[e_flat] + rank
    tile_start = jnp.arange(n_tiles, dtype=jnp.int32) * rows
    tile_expert = jnp.minimum(jnp.sum((tile_start[:, None] >= pend[None, :]).astype(jnp.int32), axis=1),
                              n_exp - 1)
    tile_valid = (tile_start < pend[-1]).astype(jnp.int32)
    last_valid = jnp.maximum(jnp.sum(tile_valid) - 1, 0)
    tile_expert = jnp.where(tile_valid > 0, tile_expert, tile_expert[last_valid])
    tile_src = jnp.minimum(jnp.arange(n_tiles, dtype=jnp.int32), last_valid)

    def by_step(step_rows):
        return pos.reshape(TOP_K, t // step_rows, step_rows).transpose(1, 0, 2).reshape(pairs)

    return (tile_expert, tile_valid, tile_src, poff + counts, pend, n_tiles * rows,
            by_step(disp_rows), by_step(comb_rows))


def _rope_tables(s):
    half = ROT_DIM // 2
    inv = ROPE_THETA ** (-jnp.arange(half, dtype=f32) * 2.0 / ROT_DIM)
    ang = jnp.arange(s).astype(f32)[:, None] * inv[None, :]
    cos, sin = jnp.cos(ang), jnp.sin(ang)
    pad = HEAD_DIM - ROT_DIM
    cos_t = jnp.concatenate([cos, cos, jnp.ones((s, pad), f32)], axis=-1)
    sin_t = jnp.concatenate([-sin, sin, jnp.zeros((s, pad), f32)], axis=-1)
    return cos_t, sin_t


def kernel(x, attn_norm, ffn_norm, pool_w, pool_scale, wq, wo, kv_norm, wk, wv, ffn_w_gate, ffn_w_up,
           ffn_w_down, router_w, moe_w_gate, moe_w_up, moe_w_down, final_norm):
    b, s, d = x.shape
    t = b * s
    n_exp = router_w.shape[-1]
    assert attn_norm.shape[0] == 2 and pool_w.shape[0] == 1 and wq.shape[0] == 1 and router_w.shape[0] == 1
    tok_rows = 1024
    ffn_rows = 512
    moe_chunk = 512
    moe_rows = 1024
    comb_rows = 512
    assert s % tok_rows == 0 and d % (HEAD_DIM * 2) == 0 and t % comb_rows == 0
    assert moe_w_gate.shape[-1] % moe_chunk == 0 and moe_w_gate.shape[-1] // moe_chunk >= 2
    assert d % (8 * LANES) == 0 and moe_rows % tok_rows == 0

    row = lambda v: v.reshape(1, -1)

    h1, hn1 = _pool_mixer(x, row(attn_norm[0]), pool_w[0].astype(bf16), row(pool_scale[0]),
                          row(ffn_norm[0]), tok_rows)
    h2 = _dense_ffn(h1.reshape(t, d), hn1.reshape(t, d), ffn_w_gate[0].astype(bf16),
                    ffn_w_up[0].astype(bf16), ffn_w_down[0].astype(bf16), ffn_rows, 512)

    cos_t, sin_t = _rope_tables(s)
    qt, k, vt, kmean = _qkv_proj(h2.reshape(b, s, d), row(attn_norm[1]), row(kv_norm), wq[0].astype(bf16),
                                 wk.astype(bf16), wv.astype(bf16), cos_t, sin_t, tok_rows)
    o = _moba_attention(qt, k, vt, kmean)

    h3, idx, gates = _out_router(o.reshape(t, d), h2, wo[0].astype(bf16), row(ffn_norm[1]), router_w[0],
                                 tok_rows)

    (tile_expert, tile_valid, tile_src, pad_lo, pad_hi, n_rows, pos_disp,
     pos_comb) = _routing_plan(idx, n_exp, moe_rows, tok_rows, comb_rows)
    x_tiles = _moe_dispatch(pos_disp, pad_lo, pad_hi, h3, row(ffn_norm[1]), n_rows, tok_rows)
    y_tiles = _moe_gemm(tile_expert, tile_valid, tile_src, x_tiles, moe_w_gate[0], moe_w_up[0],
                        moe_w_down[0], moe_rows, moe_chunk)
    out = _moe_combine(pos_comb, h3, gates, row(final_norm), y_tiles, comb_rows)
    return out.reshape(b, s, d)
```
